```python
import math
import jax, jax.numpy as jnp
from jax import lax
import numpy as np

D_MODEL = 2048
BATCH = 4
SEQ = 2048
DEPTH = 4
DEC_BATCH = 32
DEC_SEQ = 8
PAST_LEN = 16384
PAGE_SIZE = 128

N_EVEN = (DEPTH + 1) // 2
N_ODD = DEPTH // 2
D_FF = ((8 * D_MODEL // 3 + 127) // 128) * 128
D_CONV = D_MODEL // 2
CONV_K = 3
D_POOL = D_MODEL // 2
POOL_WINDOWS = (2, 4, 8, 16)
N_POOL_GROUPS = len(POOL_WINDOWS)
POOL_GW = D_POOL // N_POOL_GROUPS
POOL_CTX = max(POOL_WINDOWS) - 1
HEAD_DIM = 64
N_HEADS = D_MODEL // HEAD_DIM
N_KV_HEADS = N_HEADS // 8
GQ = N_HEADS // N_KV_HEADS
WINDOW = 128
BLOCK = WINDOW
KV_BUF = min(WINDOW, PAST_LEN)
ATTN_SCALE = HEAD_DIM ** -0.5
N_BUCKETS = 32
MAX_DISTANCE = 128
EPS = 1e-6
NEG = -1e30

kernel_name = "hybrid_conv_pool_swa_macaron_step"


def rms_norm(x, g):
    xf = x.astype(jnp.float32)
    y = xf * lax.rsqrt(jnp.mean(xf * xf, axis=-1, keepdims=True) + EPS)
    return (y * g.astype(jnp.float32)).astype(x.dtype)


def swiglu(x, w_in, w_out):
    gate, up = jnp.split(x @ w_in, 2, axis=-1)
    return (jax.nn.silu(gate) * up) @ w_out


def short_conv(v, ctx, w):
    L = v.shape[1]
    ext = jnp.concatenate([ctx.astype(v.dtype), v], axis=1)
    y = w[0] * ext[:, 0:L]
    for k in range(1, CONV_K):
        y = y + w[k] * ext[:, k:k + L]
    return y, ext[:, -(CONV_K - 1):]


def pool_mix(u, ctx, p0, w_grp, scale):
    N, L, _ = u.shape
    ext = jnp.concatenate([ctx.astype(u.dtype), u], axis=1)
    extf = ext.astype(jnp.float32)
    cs = jnp.concatenate([jnp.zeros((N, 1, D_POOL), jnp.float32), jnp.cumsum(extf, axis=1)], axis=1)
    end = cs[:, POOL_CTX + 1:POOL_CTX + 1 + L]
    pos = p0 + jnp.arange(L)
    uf = u.astype(jnp.float32)
    outs = []
    for gi, w in enumerate(POOL_WINDOWS):
        c0, c1 = gi * POOL_GW, (gi + 1) * POOL_GW
        start = cs[:, POOL_CTX + 1 - w:POOL_CTX + 1 - w + L, c0:c1]
        cnt = jnp.minimum(w, pos + 1).astype(jnp.float32)[None, :, None]
        outs.append((end[..., c0:c1] - start) / cnt - uf[..., c0:c1])
    d = jnp.stack(outs, axis=2).astype(u.dtype)
    y = jnp.einsum('nlgc,gcd->nlgd', d, w_grp).reshape(N, L, D_POOL)
    return y * scale, ext[:, -POOL_CTX:]


def t5_bucket(dist):
    max_exact = N_BUCKETS // 2
    n = jnp.maximum(dist, 0)
    ratio = jnp.log(jnp.maximum(n, 1).astype(jnp.float32) / max_exact) / math.log(MAX_DISTANCE / max_exact)
    large = jnp.minimum(max_exact + (ratio * (N_BUCKETS - max_exact)).astype(jnp.int32), N_BUCKETS - 1)
    return jnp.where(n < max_exact, n, large)


def sink_attend(q, k, v, dist, valid, sinks, rel_bias):
    Qn, Sn = dist.shape
    s = jnp.einsum('nbqkgd,nbskd->nbkgqs', q, k, preferred_element_type=jnp.float32) * ATTN_SCALE
    bias = rel_bias[t5_bucket(dist)].astype(jnp.float32)
    s = s + bias.transpose(2, 0, 1).reshape(N_KV_HEADS, GQ, Qn, Sn)
    s = jnp.where(valid[None, :, None, None], s, NEG)
    sink = sinks.astype(jnp.float32).reshape(N_KV_HEADS, GQ)[:, :, None, None]
    m = jnp.maximum(jnp.max(s, axis=-1, keepdims=True), sink)
    p = jnp.exp(s - m)
    p = p / (jnp.sum(p, axis=-1, keepdims=True) + jnp.exp(sink - m))
    return jnp.einsum('nbkgqs,nbskd->nbqkgd', p.astype(v.dtype), v)


def attn_prompt(q, k, v, sinks, rel_bias):
    N, L = q.shape[:2]
    nb = L // BLOCK
    qb = q.reshape(N, nb, BLOCK, N_KV_HEADS, GQ, HEAD_DIM)

    def band(t):
        tb = t.reshape(N, nb, BLOCK, N_KV_HEADS, HEAD_DIM)
        prev = jnp.concatenate([jnp.zeros_like(tb[:, :1]), tb[:, :-1]], axis=1)
        return jnp.concatenate([prev, tb], axis=2)

    qi = jnp.arange(BLOCK)[:, None]
    sj = jnp.arange(2 * BLOCK)[None, :]
    dist = BLOCK + qi - sj
    blk = jnp.arange(nb)[:, None, None]
    valid = (dist >= 0) & (dist <= WINDOW) & ((blk - 1) * BLOCK + sj >= 0)
    o = sink_attend(qb, band(k), band(v), dist, valid, sinks, rel_bias)
    return o.reshape(N, L, N_HEADS * HEAD_DIM)


def attn_sample(q, k, v, k_ctx, v_ctx, sinks, rel_bias):
    N, L = q.shape[:2]
    W = k_ctx.shape[1]
    kk = jnp.concatenate([k_ctx.astype(k.dtype), k], axis=1)
    vv = jnp.concatenate([v_ctx.astype(v.dtype), v], axis=1)
    qi = jnp.arange(L)[:, None]
    sj = jnp.arange(W + L)[None, :]
    dist = qi + W - sj
    valid = ((dist >= 0) & (dist <= WINDOW))[None]
    o = sink_attend(q.reshape(N, 1, L, N_KV_HEADS, GQ, HEAD_DIM), kk[:, None], vv[:, None],
                    dist, valid, sinks, rel_bias)
    return o.reshape(N, L, N_HEADS * HEAD_DIM), kk[:, -W:], vv[:, -W:]


def _trunk(x, conv_ctx, pool_ctx, k_ctx, v_ctx, p0, prompt, norm_g, w_ffn_in, w_ffn_out,
           w_mix_in, conv_w, pool_w, pool_scale, w_mix_out, w_qkv, w_o, attn_sinks, rel_bias):
    new_conv, new_pool, new_k, new_v = [], [], [], []
    for i in range(DEPTH):
        g = norm_g[i]
        x = x + 0.5 * rms_norm(swiglu(rms_norm(x, g[0]), w_ffn_in[i, 0], w_ffn_out[i, 0]), g[1])
        h = rms_norm(x, g[2])
        j = i // 2
        if i % 2 == 0:
            z = h @ w_mix_in[j]
            hc, gc, gb, u = jnp.split(z, [D_CONV, 2 * D_CONV, 3 * D_CONV], axis=-1)
            yc, cc = short_conv(gc * hc, conv_ctx[j], conv_w[j])
            yp, pc = pool_mix(u, pool_ctx[j], p0, pool_w[j], pool_scale[j])
            mix = jnp.concatenate([gb * yc, yp], axis=-1) @ w_mix_out[j]
            new_conv.append(cc)
            new_pool.append(pc)
        else:
            n, l = h.shape[:2]
            q, k, v = jnp.split(h @ w_qkv[j], [N_HEADS * HEAD_DIM, (N_HEADS + N_KV_HEADS) * HEAD_DIM], axis=-1)
            k = k.reshape(n, l, N_KV_HEADS, HEAD_DIM)
            v = v.reshape(n, l, N_KV_HEADS, HEAD_DIM)
            if prompt:
                att = attn_prompt(q, k, v, attn_sinks[j], rel_bias)
                kc, vc = k[:, -KV_BUF:], v[:, -KV_BUF:]
            else:
                att, kc, vc = attn_sample(q, k, v, k_ctx[j], v_ctx[j], attn_sinks[j], rel_bias)
            mix = att @ w_o[j]
            new_k.append(kc)
            new_v.append(vc)
        x = x + rms_norm(mix, g[3])
        x = x + 0.5 * rms_norm(swiglu(rms_norm(x, g[4]), w_ffn_in[i, 1], w_ffn_out[i, 1]), g[5])
    return x, jnp.stack(new_conv), jnp.stack(new_pool), jnp.stack(new_k), jnp.stack(new_v)


def setup_inputs(seed: int = 0) -> dict:
    key = jax.random.key(seed)
    ks = jax.random.split(key, 18)

    def nrm(k, shape, scale):
        return jax.random.normal(k, shape, jnp.float32) * scale

    return {
        "x_prompt": nrm(ks[0], (BATCH, SEQ, D_MODEL), 1.0),
        "x_sample": nrm(ks[1], (DEC_BATCH, DEC_SEQ, D_MODEL), 1.0),
        "state_conv": nrm(ks[2], (N_EVEN, DEC_BATCH, CONV_K - 1, D_CONV), 1.0),
        "state_pool": nrm(ks[3], (N_EVEN, DEC_BATCH, POOL_CTX, D_POOL), 1.0),
        "cache_k": nrm(ks[4], (N_ODD, DEC_BATCH, KV_BUF, N_KV_HEADS, HEAD_DIM), 1.0),
        "cache_v": nrm(ks[5], (N_ODD, DEC_BATCH, KV_BUF, N_KV_HEADS, HEAD_DIM), 1.0),
        "norm_g": 1.0 + nrm(ks[6], (DEPTH, 6, D_MODEL), 0.05),
        "w_ffn_in": nrm(ks[7], (DEPTH, 2, D_MODEL, 2 * D_FF), D_MODEL ** -0.5),
        "w_ffn_out": nrm(ks[8], (DEPTH, 2, D_FF, D_MODEL), D_FF ** -0.5),
        "w_mix_in": nrm(ks[9], (N_EVEN, D_MODEL, 3 * D_CONV + D_POOL), D_MODEL ** -0.5),
        "conv_w": nrm(ks[10], (N_EVEN, CONV_K, D_CONV), CONV_K ** -0.5),
        "pool_w": nrm(ks[11], (N_EVEN, N_POOL_GROUPS, POOL_GW, POOL_GW), POOL_GW ** -0.5),
        "pool_scale": 1.0 + nrm(ks[12], (N_EVEN, D_POOL), 0.1),
        "w_mix_out": nrm(ks[13], (N_EVEN, D_CONV + D_POOL, D_MODEL), (D_CONV + D_POOL) ** -0.5),
        "w_qkv": nrm(ks[14], (N_ODD, D_MODEL, (N_HEADS + 2 * N_KV_HEADS) * HEAD_DIM), D_MODEL ** -0.5),
        "w_o": nrm(ks[15], (N_ODD, N_HEADS * HEAD_DIM, D_MODEL), (N_HEADS * HEAD_DIM) ** -0.5),
        "attn_sinks": nrm(ks[16], (N_ODD, N_HEADS), 0.5),
        "rel_bias": nrm(ks[17], (N_BUCKETS, N_HEADS), 0.5),
    }


def reference(x_prompt, x_sample, state_conv, state_pool, cache_k, cache_v, norm_g, w_ffn_in,
              w_ffn_out, w_mix_in, conv_w, pool_w, pool_scale, w_mix_out, w_qkv, w_o,
              attn_sinks, rel_bias):
    nbp = x_prompt.shape[0]
    conv0 = jnp.zeros((N_EVEN, nbp, CONV_K - 1, D_CONV), x_prompt.dtype)
    pool0 = jnp.zeros((N_EVEN, nbp, POOL_CTX, D_POOL), x_prompt.dtype)
    y_prompt, conv_p, pool_p, k_p, v_p = _trunk(
        x_prompt, conv0, pool0, None, None, 0, True, norm_g, w_ffn_in, w_ffn_out,
        w_mix_in, conv_w, pool_w, pool_scale, w_mix_out, w_qkv, w_o, attn_sinks, rel_bias)
    y_sample, conv_s, pool_s, k_s, v_s = _trunk(
        x_sample, state_conv, state_pool, cache_k, cache_v, PAST_LEN, False, norm_g, w_ffn_in,
        w_ffn_out, w_mix_in, conv_w, pool_w, pool_scale, w_mix_out, w_qkv, w_o, attn_sinks, rel_bias)
    return (y_prompt, y_sample, conv_p, pool_p, k_p, v_p, conv_s, pool_s, k_s, v_s)
```

```python
import functools
import math

import jax
import jax.numpy as jnp
from jax import lax
from jax.experimental import pallas as pl
from jax.experimental.pallas import tpu as pltpu

F32 = jnp.float32
BF16 = jnp.bfloat16

EPS = 1e-6
NEG = -1e30
HEAD_DIM = 64
WINDOW = 128
BLOCK = WINDOW
CONV_K = 3
POOL_WINDOWS = (2, 4, 8, 16)
N_BUCKETS = 32
MAX_DISTANCE = 128
PAST_LEN = 16384

LANES = 128
SUBLANES = 8
CONV_HALO = SUBLANES
POOL_HALO = 2 * SUBLANES
FF_TILE = 2 * LANES
VMEM_LIMIT_BYTES = 58 * 1024 * 1024


def _divisor_tile(n, target, mult):
    best = None
    for t in range(mult, min(n, target) + 1, mult):
        if n % t == 0:
            best = t
    assert best is not None, (n, target, mult)
    return best


def _params(*sem):
    return pltpu.CompilerParams(dimension_semantics=sem, vmem_limit_bytes=VMEM_LIMIT_BYTES)


def _rms(xf, g):
    ms = jnp.mean(xf * xf, axis=-1, keepdims=True)
    return xf * lax.rsqrt(ms + EPS) * g


def _for_row_chunks(n_rows, body):
    rc = _divisor_tile(n_rows, 128, 16)

    def step(c, carry):
        body(pl.ds(pl.multiple_of(c * rc, rc), rc))
        return carry

    lax.fori_loop(0, n_rows // rc, step, 0)


def _ffn_kernel(x_ref, gpre_ref, gpost_ref, wg_ref, wul_ref, wuh_ref, wol_ref, woh_ref,
                o_ref, h_ref, *, d_ff, nj):
    j = pl.program_id(1)
    tm, d = x_ref.shape

    @pl.when(j == 0)
    def _():
        def init(rows):
            h_ref[rows, :] = _rms(x_ref[rows, :], gpre_ref[...]).astype(BF16)
            o_ref[rows, :] = jnp.zeros((rows.size, d), F32)
        _for_row_chunks(tm, init)

    w = jnp.concatenate([wg_ref[...].astype(BF16), wul_ref[...].astype(BF16),
                         wuh_ref[...].astype(BF16)], axis=1)
    r = jnp.dot(h_ref[...], w, preferred_element_type=F32)
    gate, up = r[:, :FF_TILE], r[:, FF_TILE:]
    a = (gate * jax.nn.sigmoid(gate)) * up
    if d_ff % FF_TILE:
        col = lax.broadcasted_iota(jnp.int32, a.shape, 1)
        a = jnp.where(col < d_ff - j * FF_TILE, a, 0.0)
    a = a.astype(BF16)
    wo = jnp.concatenate([wol_ref[...].astype(BF16), woh_ref[...].astype(BF16)], axis=0)
    nc = _divisor_tile(d, 512, LANES)
    for c in range(d // nc):
        o_ref[:, c * nc:(c + 1) * nc] += jnp.dot(a, wo[:, c * nc:(c + 1) * nc],
                                                 preferred_element_type=F32)

    @pl.when(j == nj - 1)
    def _():
        def fin(rows):
            o_ref[rows, :] = x_ref[rows, :] + 0.5 * _rms(o_ref[rows, :], gpost_ref[...])
        _for_row_chunks(tm, fin)


def _ffn(x, g_pre, g_post, w_in, w_out, tm):
    t, d = x.shape
    d_ff = w_out.shape[0]
    half = FF_TILE // 2
    assert d_ff % half == 0 and w_in.shape == (d, 2 * d_ff)
    n_half = d_ff // half
    nj = pl.cdiv(d_ff, FF_TILE)
    last_up = 2 * n_half - 1
    kern = functools.partial(_ffn_kernel, d_ff=d_ff, nj=nj)
    return pl.pallas_call(
        kern,
        grid=(t // tm, nj),
        in_specs=[
            pl.BlockSpec((tm, d), lambda i, j: (i, 0)),
            pl.BlockSpec((1, d), lambda i, j: (0, 0)),
            pl.BlockSpec((1, d), lambda i, j: (0, 0)),
            pl.BlockSpec((d, FF_TILE), lambda i, j: (0, j)),
            pl.BlockSpec((d, half), lambda i, j: (0, n_half + 2 * j)),
            pl.BlockSpec((d, half), lambda i, j: (0, jnp.minimum(n_half + 2 * j + 1, last_up))),
            pl.BlockSpec((half, d), lambda i, j: (2 * j, 0)),
            pl.BlockSpec((half, d), lambda i, j: (jnp.minimum(2 * j + 1, n_half - 1), 0)),
        ],
        out_specs=pl.BlockSpec((tm, d), lambda i, j: (i, 0)),
        out_shape=jax.ShapeDtypeStruct((t, d), F32),
        scratch_shapes=[pltpu.VMEM((tm, d), BF16)],
        compiler_params=_params("parallel", "arbitrary"),
        name="ffn",
    )(x, g_pre, g_post, w_in, w_in, w_in, w_out, w_out)


def _proj_kernel(x_ref, g_ref, w_ref, o_ref, h_ref):
    @pl.when(pl.program_id(1) == 0)
    def _():
        def init(rows):
            h_ref[rows, :] = _rms(x_ref[rows, :], g_ref[...]).astype(BF16)
        _for_row_chunks(x_ref.shape[0], init)

    o_ref[...] = jnp.dot(h_ref[...], w_ref[...].astype(BF16), preferred_element_type=F32)


def _proj(x, g, w, tm):
    t, d = x.shape
    n = w.shape[1]
    tn = _divisor_tile(n, 512, LANES)
    return pl.pallas_call(
        _proj_kernel,
        grid=(t // tm, n // tn),
        in_specs=[
            pl.BlockSpec((tm, d), lambda i, j: (i, 0)),
            pl.BlockSpec((1, d), lambda i, j: (0, 0)),
            pl.BlockSpec((d, tn), lambda i, j: (0, j)),
        ],
        out_specs=pl.BlockSpec((tm, tn), lambda i, j: (i, j)),
        out_shape=jax.ShapeDtypeStruct((t, n), F32),
        scratch_shapes=[pltpu.VMEM((tm, d), BF16)],
        compiler_params=_params("parallel", "arbitrary"),
        name="proj",
    )(x, g, w)


def _out_kernel(m_ref, w_ref, x_ref, g_ref, o_ref, *, nk):
    k = pl.program_id(1)
    tm, d = x_ref.shape

    @pl.when(k == 0)
    def _():
        o_ref[...] = jnp.zeros_like(o_ref)

    w = w_ref[...].astype(BF16)
    nc = _divisor_tile(d, 512, LANES)
    for c in range(d // nc):
        o_ref[:, c * nc:(c + 1) * nc] += jnp.dot(m_ref[...], w[:, c * nc:(c + 1) * nc],
                                                 preferred_element_type=F32)

    @pl.when(k == nk - 1)
    def _():
        def fin(rows):
            o_ref[rows, :] = x_ref[rows, :] + _rms(o_ref[rows, :], g_ref[...])
        _for_row_chunks(tm, fin)


def _out(m, w, x, g, tm):
    t, d = x.shape
    kdim = w.shape[0]
    tk = _divisor_tile(kdim, 512, LANES)
    nk = kdim // tk
    return pl.pallas_call(
        functools.partial(_out_kernel, nk=nk),
        grid=(t // tm, nk),
        in_specs=[
            pl.BlockSpec((tm, tk), lambda i, k: (i, k)),
            pl.BlockSpec((tk, d), lambda i, k: (k, 0)),
            pl.BlockSpec((tm, d), lambda i, k: (i, 0)),
            pl.BlockSpec((1, d), lambda i, k: (0, 0)),
        ],
        out_specs=pl.BlockSpec((tm, d), lambda i, k: (i, 0)),
        out_shape=jax.ShapeDtypeStruct((t, d), F32),
        compiler_params=_params("parallel", "arbitrary"),
        name="outproj",
    )(m, w, x, g)


def _conv3(ext, cw):
    return cw[0:1] * pltpu.roll(ext, 2, 0) + cw[1:2] * pltpu.roll(ext, 1, 0) + cw[2:3] * ext


def _window_sum(ext, w):
    s, k = ext, 1
    while k < w:
        s = s + pltpu.roll(s, k, 0)
        k *= 2
    return s


def _pool_group(win, cnt, ug, pw, scale):
    dlt = win / cnt - ug
    return jnp.dot(dlt.astype(BF16), pw.astype(BF16), preferred_element_type=F32) * scale


def _mix_prompt_kernel(hc_ref, gc_ref, gb_ref, u_ref, hch_ref, gch_ref, uh_ref, cw_ref, pw_ref,
                       ps_ref, m_ref, cst_ref, pst_ref, *, chunks_per_seq):
    rows, c = hc_ref.shape
    gw = c // len(POOL_WINDOWS)
    ci = pl.program_id(0) % chunks_per_seq
    first = ci == 0

    v = gc_ref[...] * hc_ref[...]
    v_halo = jnp.where(first, 0.0, gch_ref[...] * hch_ref[...])
    y = _conv3(jnp.concatenate([v_halo, v], axis=0), cw_ref[...])[CONV_HALO:]
    m_ref[:, :c] = (gb_ref[...] * y).astype(BF16)
    cst_ref[...] = v[rows - CONV_HALO:]

    u = u_ref[...]
    u_ext = jnp.concatenate([jnp.where(first, 0.0, uh_ref[...]), u], axis=0)
    pst_ref[...] = u_ext[rows:]
    pos1 = ci * rows + lax.broadcasted_iota(jnp.int32, (rows, 1), 0) + 1
    for gi, w in enumerate(POOL_WINDOWS):
        sl = slice(gi * gw, (gi + 1) * gw)
        win = _window_sum(u_ext[:, sl], w)[POOL_HALO:]
        cnt = jnp.minimum(w, pos1).astype(F32)
        yp = _pool_group(win, cnt, u[:, sl], pw_ref[gi], ps_ref[:, sl])
        m_ref[:, c + gi * gw:c + (gi + 1) * gw] = yp.astype(BF16)


def _mix_prompt(z, conv_w, pool_w, pool_scale, n_seq, seq, t_total):
    c = conv_w.shape[1]
    assert z.shape[1] == 4 * c
    rows = _divisor_tile(seq, 256, POOL_HALO)
    cps = seq // rows
    n_chunks = n_seq * cps
    rh_c, rh_p = rows // CONV_HALO, rows // POOL_HALO

    def halo(ratio, col):
        return lambda i: (jnp.maximum(i * ratio - 1, 0), col)

    kern = functools.partial(_mix_prompt_kernel, chunks_per_seq=cps)
    gw = c // len(POOL_WINDOWS)
    return pl.pallas_call(
        kern,
        grid=(n_chunks,),
        in_specs=[
            pl.BlockSpec((rows, c), lambda i: (i, 0)),
            pl.BlockSpec((rows, c), lambda i: (i, 1)),
            pl.BlockSpec((rows, c), lambda i: (i, 2)),
            pl.BlockSpec((rows, c), lambda i: (i, 3)),
            pl.BlockSpec((CONV_HALO, c), halo(rh_c, 0)),
            pl.BlockSpec((CONV_HALO, c), halo(rh_c, 1)),
            pl.BlockSpec((POOL_HALO, c), halo(rh_p, 3)),
            pl.BlockSpec((CONV_K, c), lambda i: (0, 0)),
            pl.BlockSpec((len(POOL_WINDOWS), gw, gw), lambda i: (0, 0, 0)),
            pl.BlockSpec((1, c), lambda i: (0, 0)),
        ],
        out_specs=[
            pl.BlockSpec((rows, 2 * c), lambda i: (i, 0)),
            pl.BlockSpec((None, CONV_HALO, c), lambda i: (i // cps, 0, 0)),
            pl.BlockSpec((None, POOL_HALO, c), lambda i: (i // cps, 0, 0)),
        ],
        out_shape=[
            jax.ShapeDtypeStruct((t_total, 2 * c), BF16),
            jax.ShapeDtypeStruct((n_seq, CONV_HALO, c), F32),
            jax.ShapeDtypeStruct((n_seq, POOL_HALO, c), F32),
        ],
        compiler_params=_params("arbitrary"),
        name="mix_prompt",
    )(z, z, z, z, z, z, z, conv_w, pool_w, pool_scale)


def _mix_sample_kernel(hc_ref, gc_ref, gb_ref, u_ref, cst_in_ref, pst_in_ref, cw_ref, pw_ref,
                       ps_ref, m_in_ref, m_ref, cst_ref, pst_ref, *, seq, past_len):
    del m_in_ref
    rows, c = hc_ref.shape
    n_seq = rows // seq
    gw = c // len(POOL_WINDOWS)

    def with_halo(halo3, x2):
        ext = jnp.concatenate([halo3, x2.reshape(n_seq, seq, c)], axis=1)
        return ext, ext.reshape(n_seq * ext.shape[1], c)

    def body_rows(flat, n_halo):
        return flat.reshape(n_seq, n_halo + seq, flat.shape[-1])[:, n_halo:].reshape(rows, flat.shape[-1])

    v = gc_ref[...] * hc_ref[...]
    v_ext3, v_ext = with_halo(cst_in_ref[...], v)
    y = body_rows(_conv3(v_ext, cw_ref[...]), CONV_HALO)
    m_ref[:, :c] = (gb_ref[...] * y).astype(BF16)
    cst_ref[...] = v_ext3[:, seq:]

    u = u_ref[...]
    u_ext3, u_ext = with_halo(pst_in_ref[...], u)
    pst_ref[...] = u_ext3[:, seq:]
    pos1 = past_len + jnp.bitwise_and(lax.broadcasted_iota(jnp.int32, (rows, 1), 0), seq - 1) + 1
    for gi, w in enumerate(POOL_WINDOWS):
        sl = slice(gi * gw, (gi + 1) * gw)
        win = body_rows(_window_sum(u_ext[:, sl], w), POOL_HALO)
        cnt = jnp.minimum(w, pos1).astype(F32)
        yp = _pool_group(win, cnt, u[:, sl], pw_ref[gi], ps_ref[:, sl])
        m_ref[:, c + gi * gw:c + (gi + 1) * gw] = yp.astype(BF16)


def _mix_sample(z, m, conv_state, pool_state, conv_w, pool_w, pool_scale, t_prompt, seq, past_len):
    c = conv_w.shape[1]
    n_seq = conv_state.shape[0]
    rows = n_seq * seq
    assert seq == SUBLANES and t_prompt % rows == 0
    rb = t_prompt // rows
    gw = c // len(POOL_WINDOWS)
    full3 = lambda i: (0, 0, 0)
    kern = functools.partial(_mix_sample_kernel, seq=seq, past_len=past_len)
    return pl.pallas_call(
        kern,
        grid=(1,),
        in_specs=[
            pl.BlockSpec((rows, c), lambda i: (rb, 0)),
            pl.BlockSpec((rows, c), lambda i: (rb, 1)),
            pl.BlockSpec((rows, c), lambda i: (rb, 2)),
            pl.BlockSpec((rows, c), lambda i: (rb, 3)),
            pl.BlockSpec((n_seq, CONV_HALO, c), full3),
            pl.BlockSpec((n_seq, POOL_HALO, c), full3),
            pl.BlockSpec((CONV_K, c), lambda i: (0, 0)),
            pl.BlockSpec((len(POOL_WINDOWS), gw, gw), full3),
            pl.BlockSpec((1, c), lambda i: (0, 0)),
            pl.BlockSpec(memory_space=pl.ANY),
        ],
        out_specs=[
            pl.BlockSpec((rows, 2 * c), lambda i: (rb, 0)),
            pl.BlockSpec((n_seq, CONV_HALO, c), full3),
            pl.BlockSpec((n_seq, POOL_HALO, c), full3),
        ],
        out_shape=[
            jax.ShapeDtypeStruct(m.shape, m.dtype),
            jax.ShapeDtypeStruct((n_seq, CONV_HALO, c), F32),
            jax.ShapeDtypeStruct((n_seq, POOL_HALO, c), F32),
        ],
        input_output_aliases={9: 0},
        compiler_params=_params("arbitrary"),
        name="mix_sample",
    )(z, z, z, z, conv_state, pool_state, conv_w, pool_w, pool_scale, m)


def _bias_kernel(rb_ref, o_ref):
    h = pl.program_id(0)
    q = lax.broadcasted_iota(jnp.int32, o_ref.shape, 0)
    s = lax.broadcasted_iota(jnp.int32, o_ref.shape, 1)
    n = jnp.maximum(BLOCK + q - s, 0)
    max_exact = N_BUCKETS // 2
    ratio = jnp.log(jnp.maximum(n, 1).astype(F32) / max_exact) / math.log(MAX_DISTANCE / max_exact)
    large = jnp.minimum(max_exact + (ratio * (N_BUCKETS - max_exact)).astype(jnp.int32), N_BUCKETS - 1)
    bucket = jnp.where(n < max_exact, n, large)
    acc = jnp.zeros(o_ref.shape, F32)
    for b in range(N_BUCKETS):
        acc = jnp.where(bucket == b, rb_ref[b, h], acc)
    o_ref[...] = acc


def _bias_table(rel_bias):
    n_heads = rel_bias.shape[1]
    return pl.pallas_call(
        _bias_kernel,
        grid=(n_heads,),
        in_specs=[pl.BlockSpec(memory_space=pltpu.SMEM)],
        out_specs=pl.BlockSpec((None, BLOCK, 2 * BLOCK), lambda h: (h, 0, 0)),
        out_shape=jax.ShapeDtypeStruct((n_heads, BLOCK, 2 * BLOCK), F32),
        compiler_params=_params("arbitrary"),
        name="rel_bias_table",
    )(rel_bias)


def _attend(q_ref, k_all, v_all, bias_ref, sinks_ref, o_ref, n_rows, mask_prev):
    n_keys = k_all.shape[0]
    n_heads = q_ref.shape[1] // HEAD_DIM
    n_kv = k_all.shape[1] // HEAD_DIM
    group = n_heads // n_kv
    row = lax.broadcasted_iota(jnp.int32, (n_rows, n_keys), 0)
    col = lax.broadcasted_iota(jnp.int32, (n_rows, n_keys), 1)
    valid = (col >= row) & (col <= row + WINDOW)
    if mask_prev is not None:
        valid = valid & ((col >= BLOCK) | jnp.logical_not(mask_prev))
    lane = lax.broadcasted_iota(jnp.int32, (n_keys, LANES), 1)
    low = lane < HEAD_DIM

    halves = {}
    for kv in range(n_kv):
        slab, hi = divmod(kv, 2)
        ks = k_all[:, slab * LANES:(slab + 1) * LANES]
        vs = v_all[:, slab * LANES:(slab + 1) * LANES]
        own = jnp.logical_not(low) if hi else low
        k_own, v_own = jnp.where(own, ks, 0.0), jnp.where(own, vs, 0.0)
        k_oth, v_oth = pltpu.roll(k_own, HEAD_DIM, 1), pltpu.roll(v_own, HEAD_DIM, 1)
        pair = ((k_oth, v_oth), (k_own, v_own)) if hi else ((k_own, v_own), (k_oth, v_oth))
        halves[kv] = [(kk.astype(BF16), vv.astype(BF16)) for kk, vv in pair]

    scale = HEAD_DIM ** -0.5
    for p in range(n_heads // 2):
        q_slab = q_ref[:, p * LANES:(p + 1) * LANES].astype(BF16)
        out = None
        for half in range(2):
            head = 2 * p + half
            kh, vh = halves[head // group][half]
            s = lax.dot_general(q_slab, kh, (((1,), (1,)), ((), ())), preferred_element_type=F32)
            s = s * scale + bias_ref[head, :n_rows, :]
            s = jnp.where(valid, s, NEG)
            sink = sinks_ref[head]
            m = jnp.maximum(jnp.max(s, axis=-1, keepdims=True), sink)
            e = jnp.exp(s - m)
            pr = e / (jnp.sum(e, axis=-1, keepdims=True) + jnp.exp(sink - m))
            o = jnp.dot(pr.astype(BF16), vh, preferred_element_type=F32)
            out = o if out is None else out + o
        o_ref[:, p * LANES:(p + 1) * LANES] = out.astype(o_ref.dtype)


def _attn_prompt_kernel(sinks_ref, q_ref, kc_ref, kp_ref, vc_ref, vp_ref, bias_ref, o_ref):
    first = pl.program_id(1) == 0
    k_all = jnp.concatenate([kp_ref[...], kc_ref[...]], axis=0)
    v_all = jnp.concatenate([vp_ref[...], vc_ref[...]], axis=0)
    _attend(q_ref, k_all, v_all, bias_ref, sinks_ref, o_ref, BLOCK, first)


def _attn_prompt(qkv, bias, sinks, n_seq, seq, n_heads, n_kv, t_total):
    d = n_heads * HEAD_DIM
    kvw = n_kv * HEAD_DIM
    assert kvw % LANES == 0 and d % kvw == 0 and seq % BLOCK == 0
    nb = seq // BLOCK
    kcol, vcol = d // kvw, d // kvw + 1
    cur = lambda col: (lambda n, b: (n * nb + b, col))
    prev = lambda col: (lambda n, b: (n * nb + jnp.maximum(b - 1, 0), col))
    return pl.pallas_call(
        _attn_prompt_kernel,
        grid=(n_seq, nb),
        in_specs=[
            pl.BlockSpec(memory_space=pltpu.SMEM),
            pl.BlockSpec((BLOCK, d), cur(0)),
            pl.BlockSpec((BLOCK, kvw), cur(kcol)),
            pl.BlockSpec((BLOCK, kvw), prev(kcol)),
            pl.BlockSpec((BLOCK, kvw), cur(vcol)),
            pl.BlockSpec((BLOCK, kvw), prev(vcol)),
            pl.BlockSpec((n_heads, BLOCK, 2 * BLOCK), lambda n, b: (0, 0, 0)),
        ],
        out_specs=pl.BlockSpec((BLOCK, d), cur(0)),
        out_shape=jax.ShapeDtypeStruct((t_total, d), BF16),
        compiler_params=_params("parallel", "arbitrary"),
        name="attn_prompt",
    )(sinks, qkv, qkv, qkv, qkv, qkv, bias)


def _attn_sample_kernel(sinks_ref, q_ref, kn_ref, vn_ref, kc_ref, vc_ref, bias_ref, att_in_ref, o_ref):
    del att_in_ref
    seq, kvw = kn_ref.shape
    pad = jnp.zeros((2 * BLOCK - kc_ref.shape[0] - seq, kvw), F32)
    k_all = jnp.concatenate([kc_ref[...], kn_ref[...], pad], axis=0)
    v_all = jnp.concatenate([vc_ref[...], vn_ref[...], pad], axis=0)
    _attend(q_ref, k_all, v_all, bias_ref, sinks_ref, o_ref, seq, None)


def _attn_sample(qkv, att, cache_k, cache_v, bias, sinks, t_prompt, seq, n_heads):
    n_seq, kv_buf, kvw = cache_k.shape
    d = n_heads * HEAD_DIM
    assert kv_buf == BLOCK and seq == SUBLANES and t_prompt % seq == 0
    rb = t_prompt // seq
    kcol, vcol = d // kvw, d // kvw + 1
    return pl.pallas_call(
        _attn_sample_kernel,
        grid=(n_seq,),
        in_specs=[
            pl.BlockSpec(memory_space=pltpu.SMEM),
            pl.BlockSpec((seq, d), lambda n: (rb + n, 0)),
            pl.BlockSpec((seq, kvw), lambda n: (rb + n, kcol)),
            pl.BlockSpec((seq, kvw), lambda n: (rb + n, vcol)),
            pl.BlockSpec((None, kv_buf, kvw), lambda n: (n, 0, 0)),
            pl.BlockSpec((None, kv_buf, kvw), lambda n: (n, 0, 0)),
            pl.BlockSpec((n_heads, BLOCK, 2 * BLOCK), lambda n: (0, 0, 0)),
            pl.BlockSpec(memory_space=pl.ANY),
        ],
        out_specs=pl.BlockSpec((seq, d), lambda n: (rb + n, 0)),
        out_shape=jax.ShapeDtypeStruct(att.shape, att.dtype),
        input_output_aliases={7: 0},
        compiler_params=_params("arbitrary"),
        name="attn_sample",
    )(sinks, qkv, qkv, qkv, cache_k, cache_v, bias, att)


def kernel(x_prompt, x_sample, state_conv, state_pool, cache_k, cache_v, norm_g, w_ffn_in,
           w_ffn_out, w_mix_in, conv_w, pool_w, pool_scale, w_mix_out, w_qkv, w_o,
           attn_sinks, rel_bias):
    n_p, seq_p, d = x_prompt.shape
    n_s, seq_s, _ = x_sample.shape
    depth = norm_g.shape[0]
    n_heads = d // HEAD_DIM
    n_kv = cache_k.shape[3]
    kv_buf = cache_k.shape[2]
    kvw = n_kv * HEAD_DIM
    conv_ctx = state_conv.shape[2]
    pool_ctx = state_pool.shape[2]
    t_p, t_s = n_p * seq_p, n_s * seq_s
    t = t_p + t_s
    tm_ffn = _divisor_tile(t, 768, 16)
    tm = _divisor_tile(t, 1056, 16)

    x = jnp.concatenate([x_prompt.reshape(t_p, d), x_sample.reshape(t_s, d)], axis=0)
    bias = _bias_table(rel_bias)

    conv_p, pool_p, k_p, v_p, conv_s, pool_s, k_s, v_s = ([] for _ in range(8))
    for i in range(depth):
        g = norm_g[i]
        grow = lambda r: g[r:r + 1]
        x = _ffn(x, grow(0), grow(1), w_ffn_in[i, 0], w_ffn_out[i, 0], tm_ffn)
        j = i // 2
        if i % 2 == 0:
            z = _proj(x, grow(2), w_mix_in[j], tm)
            m, cst, pst = _mix_prompt(z, conv_w[j], pool_w[j], pool_scale[j][None], n_p, seq_p, t)
            conv_p.append(cst[:, CONV_HALO - conv_ctx:])
            pool_p.append(pst[:, POOL_HALO - pool_ctx:])
            cst_in = jnp.pad(state_conv[j], ((0, 0), (CONV_HALO - conv_ctx, 0), (0, 0)))
            pst_in = jnp.pad(state_pool[j], ((0, 0), (POOL_HALO - pool_ctx, 0), (0, 0)))
            m, cst, pst = _mix_sample(z, m, cst_in, pst_in, conv_w[j], pool_w[j], pool_scale[j][None],
                                      t_p, seq_s, PAST_LEN)
            conv_s.append(cst[:, CONV_HALO - conv_ctx:])
            pool_s.append(pst[:, POOL_HALO - pool_ctx:])
            x = _out(m, w_mix_out[j], x, grow(3), tm)
        else:
            qkv = _proj(x, grow(2), w_qkv[j], tm)
            att = _attn_prompt(qkv, bias, attn_sinks[j], n_p, seq_p, n_heads, n_kv, t)
            ck = cache_k[j].reshape(n_s, kv_buf, kvw)
            cv = cache_v[j].reshape(n_s, kv_buf, kvw)
            att = _attn_sample(qkv, att, ck, cv, bias, attn_sinks[j], t_p, seq_s, n_heads)
            k_new, v_new = qkv[:, d:d + kvw], qkv[:, d + kvw:]
            kv_p = lambda a: a[:t_p].reshape(n_p, seq_p, n_kv, HEAD_DIM)[:, seq_p - kv_buf:]
            k_p.append(kv_p(k_new))
            v_p.append(kv_p(v_new))
            kv_s = lambda c, a: jnp.concatenate(
                [c, a[t_p:].reshape(n_s, seq_s, kvw)], axis=1)[:, seq_s:].reshape(n_s, kv_buf, n_kv, HEAD_DIM)
            k_s.append(kv_s(ck, k_new))
            v_s.append(kv_s(cv, v_new))
            x = _out(att, w_o[j], x, grow(3), tm)
        x = _ffn(x, grow(4), grow(5), w_ffn_in[i, 1], w_ffn_out[i, 1], tm_ffn)

    y_p = x[:t_p].reshape(n_p, seq_p, d)
    y_s = x[t_p:].reshape(n_s, seq_s, d)
    st = jnp.stack
    return (y_p, y_s, st(conv_p), st(pool_p), st(k_p), st(v_p), st(conv_s), st(pool_s), st(k_s), st(v_s))
```

```python
import functools
import math

import jax
import jax.numpy as jnp
from jax import lax
from jax.experimental import pallas as pl
from jax.experimental.pallas import tpu as pltpu

F32 = jnp.float32
BF16 = jnp.bfloat16

EPS = 1e-6
NEG = -1e30
HEAD_DIM = 64
WINDOW = 128
BLOCK = WINDOW
CONV_K = 3
POOL_WINDOWS = (2, 4, 8, 16)
N_BUCKETS = 32
MAX_DISTANCE = 128
PAST_LEN = 16384

LANES = 128
SUBLANES = 8
CONV_HALO = SUBLANES
POOL_HALO = 2 * SUBLANES
FF_TILE = 2 * LANES
VMEM_LIMIT_BYTES = 58 * 1024 * 1024


def _divisor_tile(n, target, mult):
    best = None
    for t in range(mult, min(n, target) + 1, mult):
        if n % t == 0:
            best = t
    assert best is not None, (n, target, mult)
    return best


def _params(*sem):
    return pltpu.CompilerParams(dimension_semantics=sem, vmem_limit_bytes=VMEM_LIMIT_BYTES)


def _rms(xf, g):
    ms = jnp.mean(xf * xf, axis=-1, keepdims=True)
    return xf * lax.rsqrt(ms + EPS) * g


def _for_row_chunks(n_rows, body):
    rc = _divisor_tile(n_rows, 272, 16)

    def step(c, carry):
        body(pl.ds(pl.multiple_of(c * rc, rc), rc))
        return carry

    lax.fori_loop(0, n_rows // rc, step, 0)


def _ffn_kernel(x_ref, g_ref, wg_ref, wul_ref, wuh_ref, wol_ref, woh_ref,
                o_ref, h_ref, *, d_ff, nj, pre, post):
    j = pl.program_id(1)
    tm, d = x_ref.shape

    @pl.when(j == 0)
    def _():
        def init(rows):
            h_ref[rows, :] = _rms(x_ref[rows, :], g_ref[pre:pre + 1, :]).astype(BF16)
            o_ref[rows, :] = jnp.zeros((rows.size, d), F32)
        _for_row_chunks(tm, init)

    w = jnp.concatenate([wg_ref[...].astype(BF16), wul_ref[...].astype(BF16),
                         wuh_ref[...].astype(BF16)], axis=1)
    r = jnp.dot(h_ref[...], w, preferred_element_type=F32)
    gate, up = r[:, :FF_TILE], r[:, FF_TILE:]
    a = (gate * jax.nn.sigmoid(gate)) * up
    if d_ff % FF_TILE:
        col = lax.broadcasted_iota(jnp.int32, a.shape, 1)
        a = jnp.where(col < d_ff - j * FF_TILE, a, 0.0)
    a = a.astype(BF16)
    wo = jnp.concatenate([wol_ref[...].astype(BF16), woh_ref[...].astype(BF16)], axis=0)
    nc = _divisor_tile(d, 512, LANES)
    for c in range(d // nc):
        o_ref[:, c * nc:(c + 1) * nc] += jnp.dot(a, wo[:, c * nc:(c + 1) * nc],
                                                 preferred_element_type=F32)

    @pl.when(j == nj - 1)
    def _():
        def fin(rows):
            o_ref[rows, :] = x_ref[rows, :] + 0.5 * _rms(o_ref[rows, :], g_ref[post:post + 1, :])
        _for_row_chunks(tm, fin)


def _ffn(x, norm_g, w_in, w_out, layer, which, tm):
    t, d = x.shape
    d_ff = w_out.shape[2]
    half = FF_TILE // 2
    assert d_ff % half == 0 and w_in.shape[2:] == (d, 2 * d_ff)
    n_half = d_ff // half
    nj = pl.cdiv(d_ff, FF_TILE)
    last_up = 2 * n_half - 1
    kern = functools.partial(_ffn_kernel, d_ff=d_ff, nj=nj, pre=4 * which, post=4 * which + 1)
    w_in_spec = lambda width, col: pl.BlockSpec((None, None, d, width), lambda i, j: (layer, which, 0, col(j)))
    w_out_spec = lambda row: pl.BlockSpec((None, None, half, d), lambda i, j: (layer, which, row(j), 0))
    return pl.pallas_call(
        kern,
        grid=(t // tm, nj),
        in_specs=[
            pl.BlockSpec((tm, d), lambda i, j: (i, 0), pipeline_mode=pl.Buffered(1)),
            pl.BlockSpec((None,) + norm_g.shape[1:], lambda i, j: (layer, 0, 0)),
            w_in_spec(FF_TILE, lambda j: j),
            w_in_spec(half, lambda j: n_half + 2 * j),
            w_in_spec(half, lambda j: jnp.minimum(n_half + 2 * j + 1, last_up)),
            w_out_spec(lambda j: 2 * j),
            w_out_spec(lambda j: jnp.minimum(2 * j + 1, n_half - 1)),
        ],
        out_specs=pl.BlockSpec((tm, d), lambda i, j: (i, 0)),
        out_shape=jax.ShapeDtypeStruct((t, d), F32),
        scratch_shapes=[pltpu.VMEM((tm, d), BF16)],
        compiler_params=_params("parallel", "arbitrary"),
        name="ffn",
    )(x, norm_g, w_in, w_in, w_in, w_out, w_out)


MIX_PRE, MIX_POST = 2, 3


def _proj_kernel(x_ref, g_ref, w_ref, o_ref, h_ref):
    @pl.when(pl.program_id(1) == 0)
    def _():
        def init(rows):
            h_ref[rows, :] = _rms(x_ref[rows, :], g_ref[MIX_PRE:MIX_PRE + 1, :]).astype(BF16)
        _for_row_chunks(x_ref.shape[0], init)

    o_ref[...] = jnp.dot(h_ref[...], w_ref[...].astype(BF16), preferred_element_type=F32)


def _proj(x, norm_g, layer, w, widx, tm):
    t, d = x.shape
    n = w.shape[2]
    tn = _divisor_tile(n, 512, LANES)
    return pl.pallas_call(
        _proj_kernel,
        grid=(t // tm, n // tn),
        in_specs=[
            pl.BlockSpec((tm, d), lambda i, j: (i, 0)),
            pl.BlockSpec((None,) + norm_g.shape[1:], lambda i, j: (layer, 0, 0)),
            pl.BlockSpec((None, d, tn), lambda i, j: (widx, 0, j)),
        ],
        out_specs=pl.BlockSpec((tm, tn), lambda i, j: (i, j)),
        out_shape=jax.ShapeDtypeStruct((t, n), F32),
        scratch_shapes=[pltpu.VMEM((tm, d), BF16)],
        compiler_params=_params("parallel", "arbitrary"),
        name="proj",
    )(x, norm_g, w)


def _out_kernel(m_ref, w_ref, x_ref, g_ref, o_ref, *, nk):
    k = pl.program_id(1)
    tm, d = x_ref.shape

    @pl.when(k == 0)
    def _():
        o_ref[...] = jnp.zeros_like(o_ref)

    w = w_ref[...].astype(BF16)
    nc = _divisor_tile(d, 512, LANES)
    for c in range(d // nc):
        o_ref[:, c * nc:(c + 1) * nc] += jnp.dot(m_ref[...], w[:, c * nc:(c + 1) * nc],
                                                 preferred_element_type=F32)

    @pl.when(k == nk - 1)
    def _():
        def fin(rows):
            o_ref[rows, :] = x_ref[rows, :] + _rms(o_ref[rows, :], g_ref[MIX_POST:MIX_POST + 1, :])
        _for_row_chunks(tm, fin)


def _out(m, w, widx, x, norm_g, layer, tm):
    t, d = x.shape
    kdim = w.shape[1]
    tk = _divisor_tile(kdim, 512, LANES)
    nk = kdim // tk
    return pl.pallas_call(
        functools.partial(_out_kernel, nk=nk),
        grid=(t // tm, nk),
        in_specs=[
            pl.BlockSpec((tm, tk), lambda i, k: (i, k)),
            pl.BlockSpec((None, tk, d), lambda i, k: (widx, k, 0)),
            pl.BlockSpec((tm, d), lambda i, k: (i, 0)),
            pl.BlockSpec((None,) + norm_g.shape[1:], lambda i, k: (layer, 0, 0)),
        ],
        out_specs=pl.BlockSpec((tm, d), lambda i, k: (i, 0)),
        out_shape=jax.ShapeDtypeStruct((t, d), F32),
        compiler_params=_params("parallel", "arbitrary"),
        name="outproj",
    )(m, w, x, norm_g)


def _conv3(ext, cw):
    return cw[0:1] * pltpu.roll(ext, 2, 0) + cw[1:2] * pltpu.roll(ext, 1, 0) + cw[2:3] * ext


def _window_sum(ext, w):
    s, k = ext, 1
    while k < w:
        s = s + pltpu.roll(s, k, 0)
        k *= 2
    return s


def _pool_group(win, cnt, ug, pw, scale):
    dlt = win / cnt - ug
    return jnp.dot(dlt.astype(BF16), pw.astype(BF16), preferred_element_type=F32) * scale


def _mix_prompt_kernel(hc_ref, gc_ref, gb_ref, u_ref, hch_ref, gch_ref, uh_ref, cw_ref, pw_ref,
                       ps_ref, m_ref, cst_ref, pst_ref, *, chunks_per_seq):
    rows, c = hc_ref.shape
    gw = c // len(POOL_WINDOWS)
    ci = pl.program_id(0) % chunks_per_seq
    first = ci == 0

    v = gc_ref[...] * hc_ref[...]
    v_halo = jnp.where(first, 0.0, gch_ref[...] * hch_ref[...])
    y = _conv3(jnp.concatenate([v_halo, v], axis=0), cw_ref[...])[CONV_HALO:]
    m_ref[:, :c] = (gb_ref[...] * y).astype(BF16)
    cst_ref[...] = v[rows - CONV_HALO:]

    u = u_ref[...]
    u_ext = jnp.concatenate([jnp.where(first, 0.0, uh_ref[...]), u], axis=0)
    pst_ref[...] = u_ext[rows:]
    pos1 = ci * rows + lax.broadcasted_iota(jnp.int32, (rows, 1), 0) + 1
    for gi, w in enumerate(POOL_WINDOWS):
        sl = slice(gi * gw, (gi + 1) * gw)
        win = _window_sum(u_ext[:, sl], w)[POOL_HALO:]
        cnt = jnp.minimum(w, pos1).astype(F32)
        yp = _pool_group(win, cnt, u[:, sl], pw_ref[gi], ps_ref[:, sl])
        m_ref[:, c + gi * gw:c + (gi + 1) * gw] = yp.astype(BF16)


def _mix_prompt(z, conv_w, pool_w, pool_scale, n_seq, seq, t_total):
    c = conv_w.shape[1]
    assert z.shape[1] == 4 * c
    rows = _divisor_tile(seq, 256, POOL_HALO)
    cps = seq // rows
    n_chunks = n_seq * cps
    rh_c, rh_p = rows // CONV_HALO, rows // POOL_HALO

    def halo(ratio, col):
        return lambda i: (jnp.maximum(i * ratio - 1, 0), col)

    kern = functools.partial(_mix_prompt_kernel, chunks_per_seq=cps)
    gw = c // len(POOL_WINDOWS)
    return pl.pallas_call(
        kern,
        grid=(n_chunks,),
        in_specs=[
            pl.BlockSpec((rows, c), lambda i: (i, 0)),
            pl.BlockSpec((rows, c), lambda i: (i, 1)),
            pl.BlockSpec((rows, c), lambda i: (i, 2)),
            pl.BlockSpec((rows, c), lambda i: (i, 3)),
            pl.BlockSpec((CONV_HALO, c), halo(rh_c, 0)),
            pl.BlockSpec((CONV_HALO, c), halo(rh_c, 1)),
            pl.BlockSpec((POOL_HALO, c), halo(rh_p, 3)),
            pl.BlockSpec((CONV_K, c), lambda i: (0, 0)),
            pl.BlockSpec((len(POOL_WINDOWS), gw, gw), lambda i: (0, 0, 0)),
            pl.BlockSpec((1, c), lambda i: (0, 0)),
        ],
        out_specs=[
            pl.BlockSpec((rows, 2 * c), lambda i: (i, 0)),
            pl.BlockSpec((None, CONV_HALO, c), lambda i: (i // cps, 0, 0)),
            pl.BlockSpec((None, POOL_HALO, c), lambda i: (i // cps, 0, 0)),
        ],
        out_shape=[
            jax.ShapeDtypeStruct((t_total, 2 * c), BF16),
            jax.ShapeDtypeStruct((n_seq, CONV_HALO, c), F32),
            jax.ShapeDtypeStruct((n_seq, POOL_HALO, c), F32),
        ],
        compiler_params=_params("arbitrary"),
        name="mix_prompt",
    )(z, z, z, z, z, z, z, conv_w, pool_w, pool_scale)


def _mix_sample_kernel(hc_ref, gc_ref, gb_ref, u_ref, cst_in_ref, pst_in_ref, cw_ref, pw_ref,
                       ps_ref, m_in_ref, m_ref, cst_ref, pst_ref, *, seq, past_len):
    del m_in_ref
    rows, c = hc_ref.shape
    n_seq = rows // seq
    gw = c // len(POOL_WINDOWS)

    def with_halo(halo3, x2):
        ext = jnp.concatenate([halo3, x2.reshape(n_seq, seq, c)], axis=1)
        return ext, ext.reshape(n_seq * ext.shape[1], c)

    def body_rows(flat, n_halo):
        return flat.reshape(n_seq, n_halo + seq, flat.shape[-1])[:, n_halo:].reshape(rows, flat.shape[-1])

    v = gc_ref[...] * hc_ref[...]
    v_ext3, v_ext = with_halo(cst_in_ref[...], v)
    y = body_rows(_conv3(v_ext, cw_ref[...]), CONV_HALO)
    m_ref[:, :c] = (gb_ref[...] * y).astype(BF16)
    cst_ref[...] = v_ext3[:, seq:]

    u = u_ref[...]
    u_ext3, u_ext = with_halo(pst_in_ref[...], u)
    pst_ref[...] = u_ext3[:, seq:]
    pos1 = past_len + jnp.bitwise_and(lax.broadcasted_iota(jnp.int32, (rows, 1), 0), seq - 1) + 1
    for gi, w in enumerate(POOL_WINDOWS):
        sl = slice(gi * gw, (gi + 1) * gw)
        win = body_rows(_window_sum(u_ext[:, sl], w), POOL_HALO)
        cnt = jnp.minimum(w, pos1).astype(F32)
        yp = _pool_group(win, cnt, u[:, sl], pw_ref[gi], ps_ref[:, sl])
        m_ref[:, c + gi * gw:c + (gi + 1) * gw] = yp.astype(BF16)


def _mix_sample(z, m, conv_state, pool_state, conv_w, pool_w, pool_scale, t_prompt, seq, past_len):
    c = conv_w.shape[1]
    n_seq = conv_state.shape[0]
    rows = n_seq * seq
    assert seq == SUBLANES and t_prompt % rows == 0
    rb = t_prompt // rows
    gw = c // len(POOL_WINDOWS)
    full3 = lambda i: (0, 0, 0)
    kern = functools.partial(_mix_sample_kernel, seq=seq, past_len=past_len)
    return pl.pallas_call(
        kern,
        grid=(1,),
        in_specs=[
            pl.BlockSpec((rows, c), lambda i: (rb, 0)),
            pl.BlockSpec((rows, c), lambda i: (rb, 1)),
            pl.BlockSpec((rows, c), lambda i: (rb, 2)),
            pl.BlockSpec((rows, c), lambda i: (rb, 3)),
            pl.BlockSpec((n_seq, CONV_HALO, c), full3),
            pl.BlockSpec((n_seq, POOL_HALO, c), full3),
            pl.BlockSpec((CONV_K, c), lambda i: (0, 0)),
            pl.BlockSpec((len(POOL_WINDOWS), gw, gw), full3),
            pl.BlockSpec((1, c), lambda i: (0, 0)),
            pl.BlockSpec(memory_space=pl.ANY),
        ],
        out_specs=[
            pl.BlockSpec((rows, 2 * c), lambda i: (rb, 0)),
            pl.BlockSpec((n_seq, CONV_HALO, c), full3),
            pl.BlockSpec((n_seq, POOL_HALO, c), full3),
        ],
        out_shape=[
            jax.ShapeDtypeStruct(m.shape, m.dtype),
            jax.ShapeDtypeStruct((n_seq, CONV_HALO, c), F32),
            jax.ShapeDtypeStruct((n_seq, POOL_HALO, c), F32),
        ],
        input_output_aliases={9: 0},
        compiler_params=_params("arbitrary"),
        name="mix_sample",
    )(z, z, z, z, conv_state, pool_state, conv_w, pool_w, pool_scale, m)


N_KEYS = 2 * BLOCK


def _t5_bucket(dist):
    max_exact = N_BUCKETS // 2
    n = jnp.maximum(dist, 0)
    ratio = jnp.log(jnp.maximum(n, 1).astype(F32) / max_exact) / math.log(MAX_DISTANCE / max_exact)
    large = jnp.minimum(max_exact + (ratio * (N_BUCKETS - max_exact)).astype(jnp.int32), N_BUCKETS - 1)
    return jnp.where(n < max_exact, n, large)


def _bias_prompt_kernel(rb_ref, o_ref):
    h = pl.program_id(0)
    s = lax.broadcasted_iota(jnp.int32, o_ref.shape, 0)
    q = lax.broadcasted_iota(jnp.int32, o_ref.shape, 1)
    bucket = _t5_bucket(BLOCK + q - s)
    acc = jnp.zeros(o_ref.shape, F32)
    for b in range(N_BUCKETS):
        acc = jnp.where(bucket == b, rb_ref[b, h], acc)
    o_ref[...] = acc


def _bias_sample_kernel(rb_ref, o_ref, *, seq, n_heads):
    half = pl.program_id(0)
    s = lax.broadcasted_iota(jnp.int32, o_ref.shape, 0)
    lane = lax.broadcasted_iota(jnp.int32, o_ref.shape, 1)
    bucket = _t5_bucket(BLOCK + jnp.bitwise_and(lane, seq - 1) - s)
    lane1 = lax.broadcasted_iota(jnp.int32, (1, LANES), 1)
    acc = jnp.zeros(o_ref.shape, F32)
    for b in range(N_BUCKETS):
        vec = jnp.zeros((1, LANES), F32)
        for slab in range(n_heads // 2):
            vec = jnp.where((lane1 >= slab * seq) & (lane1 < (slab + 1) * seq), rb_ref[b, 2 * slab + half], vec)
        acc = jnp.where(bucket == b, vec, acc)
    o_ref[...] = acc


def _bias_tables(rel_bias, seq_s):
    n_heads = rel_bias.shape[1]
    assert seq_s == SUBLANES and (n_heads // 2) * seq_s <= LANES
    prompt = pl.pallas_call(
        _bias_prompt_kernel,
        grid=(n_heads,),
        in_specs=[pl.BlockSpec(memory_space=pltpu.SMEM)],
        out_specs=pl.BlockSpec((None, N_KEYS, BLOCK), lambda h: (h, 0, 0)),
        out_shape=jax.ShapeDtypeStruct((n_heads, N_KEYS, BLOCK), F32),
        compiler_params=_params("arbitrary"),
        name="rel_bias_prompt",
    )(rel_bias)
    sample = pl.pallas_call(
        functools.partial(_bias_sample_kernel, seq=seq_s, n_heads=n_heads),
        grid=(2,),
        in_specs=[pl.BlockSpec(memory_space=pltpu.SMEM)],
        out_specs=pl.BlockSpec((None, N_KEYS, LANES), lambda h: (h, 0, 0)),
        out_shape=jax.ShapeDtypeStruct((2, N_KEYS, LANES), F32),
        compiler_params=_params("arbitrary"),
        name="rel_bias_sample",
    )(rel_bias)
    return prompt, sample


def _stage_kv(k_all, v_all, kpad_ref, vtpad_ref):
    n_keys, kvw = k_all.shape
    lane = lax.broadcasted_iota(jnp.int32, (n_keys, LANES), 1)
    row = lax.broadcasted_iota(jnp.int32, (LANES, n_keys), 0)
    for slab in range(kvw // LANES):
        ks = k_all[:, slab * LANES:(slab + 1) * LANES]
        vt = v_all[:, slab * LANES:(slab + 1) * LANES].T
        for own in range(2):
            kv = 2 * slab + own
            k_own = jnp.where((lane >= own * HEAD_DIM) & (lane < (own + 1) * HEAD_DIM), ks, 0.0)
            v_own = jnp.where((row >= own * HEAD_DIM) & (row < (own + 1) * HEAD_DIM), vt, 0.0)
            kpad_ref[kv, own] = k_own.astype(BF16)
            kpad_ref[kv, 1 - own] = pltpu.roll(k_own, HEAD_DIM, 1).astype(BF16)
            vtpad_ref[kv, own] = v_own.astype(BF16)
            vtpad_ref[kv, 1 - own] = pltpu.roll(v_own, HEAD_DIM, 0).astype(BF16)


_TRANS_B = (((1,), (1,)), ((), ()))


def _attn_prompt_kernel(sinks_ref, q_ref, kc_ref, kp_ref, vc_ref, vp_ref, bias_ref, o_ref,
                        kpad_ref, vtpad_ref, s_ref, e_ref, m_ref, den_ref):
    first = pl.program_id(1) == 0
    n_heads = bias_ref.shape[0]
    group = n_heads // kpad_ref.shape[0]
    _stage_kv(jnp.concatenate([kp_ref[...], kc_ref[...]], axis=0),
              jnp.concatenate([vp_ref[...], vc_ref[...]], axis=0), kpad_ref, vtpad_ref)
    key = lax.broadcasted_iota(jnp.int32, (N_KEYS, BLOCK), 0)
    qry = lax.broadcasted_iota(jnp.int32, (N_KEYS, BLOCK), 1)
    valid = (key >= qry) & (key <= qry + WINDOW) & ((key >= BLOCK) | jnp.logical_not(first))

    for p in range(n_heads // 2):
        q_slab = (q_ref[:, p * LANES:(p + 1) * LANES] * HEAD_DIM ** -0.5).astype(BF16)
        for half in range(2):
            head = 2 * p + half
            s = lax.dot_general(kpad_ref[head // group, half], q_slab, _TRANS_B, preferred_element_type=F32)
            s = jnp.where(valid, s + bias_ref[head], NEG)
            s_ref[head] = s
            m_ref[head:head + 1, :] = jnp.maximum(jnp.max(s, axis=0, keepdims=True), sinks_ref[head])
    for head in range(n_heads):
        m = m_ref[head:head + 1, :]
        e = jnp.exp(s_ref[head] - m)
        den_ref[head:head + 1, :] = jnp.sum(e, axis=0, keepdims=True) + jnp.exp(sinks_ref[head] - m)
        e_ref[head] = e.astype(BF16)
    first_head = lax.broadcasted_iota(jnp.int32, (LANES, BLOCK), 0) < HEAD_DIM
    for p in range(n_heads // 2):
        kv = 2 * p // group
        acc = (jnp.dot(vtpad_ref[kv, 0], e_ref[2 * p], preferred_element_type=F32)
               + jnp.dot(vtpad_ref[kv, 1], e_ref[2 * p + 1], preferred_element_type=F32))
        inv = jnp.where(first_head, 1.0 / den_ref[2 * p:2 * p + 1, :], 1.0 / den_ref[2 * p + 1:2 * p + 2, :])
        o_ref[:, p * LANES:(p + 1) * LANES] = (acc * inv).T.astype(o_ref.dtype)


def _attn_prompt(qkv, bias, sinks, n_seq, seq, n_heads, n_kv, t_total):
    d = n_heads * HEAD_DIM
    kvw = n_kv * HEAD_DIM
    assert kvw % LANES == 0 and d % kvw == 0 and seq % BLOCK == 0
    nb = seq // BLOCK
    kcol, vcol = d // kvw, d // kvw + 1
    cur = lambda col: (lambda n, b: (n * nb + b, col))
    prev = lambda col: (lambda n, b: (n * nb + jnp.maximum(b - 1, 0), col))
    return pl.pallas_call(
        _attn_prompt_kernel,
        grid=(n_seq, nb),
        in_specs=[
            pl.BlockSpec(memory_space=pltpu.SMEM),
            pl.BlockSpec((BLOCK, d), cur(0)),
            pl.BlockSpec((BLOCK, kvw), cur(kcol)),
            pl.BlockSpec((BLOCK, kvw), prev(kcol)),
            pl.BlockSpec((BLOCK, kvw), cur(vcol)),
            pl.BlockSpec((BLOCK, kvw), prev(vcol)),
            pl.BlockSpec((n_heads, N_KEYS, BLOCK), lambda n, b: (0, 0, 0), pipeline_mode=pl.Buffered(1)),
        ],
        out_specs=pl.BlockSpec((BLOCK, d), cur(0)),
        out_shape=jax.ShapeDtypeStruct((t_total, d), BF16),
        scratch_shapes=[
            pltpu.VMEM((n_kv, 2, N_KEYS, LANES), BF16),
            pltpu.VMEM((n_kv, 2, LANES, N_KEYS), BF16),
            pltpu.VMEM((n_heads, N_KEYS, BLOCK), F32),
            pltpu.VMEM((n_heads, N_KEYS, BLOCK), BF16),
            pltpu.VMEM((n_heads, BLOCK), F32),
            pltpu.VMEM((n_heads, BLOCK), F32),
        ],
        compiler_params=_params("parallel", "arbitrary"),
        name="attn_prompt",
    )(sinks, qkv, qkv, qkv, qkv, qkv, bias)


def _attn_sample_kernel(sinks_ref, q_ref, kn_ref, vn_ref, kc_ref, vc_ref, bias_ref, att_in_ref, o_ref,
                        kpad_ref, vtpad_ref):
    del att_in_ref
    seq, kvw = kn_ref.shape
    n_pairs = q_ref.shape[1] // LANES
    n_kv = kvw // HEAD_DIM
    kv_lanes = n_pairs // n_kv * seq
    pad = jnp.zeros((N_KEYS - kc_ref.shape[0] - seq, kvw), F32)
    _stage_kv(jnp.concatenate([kc_ref[...], kn_ref[...], pad], axis=0),
              jnp.concatenate([vc_ref[...], vn_ref[...], pad], axis=0), kpad_ref, vtpad_ref)

    rows = [q_ref[:, p * LANES:(p + 1) * LANES] for p in range(n_pairs)]
    if n_pairs * seq < LANES:
        rows.append(jnp.zeros((LANES - n_pairs * seq, LANES), F32))
    qs = (jnp.concatenate(rows, axis=0) * HEAD_DIM ** -0.5).astype(BF16)

    key = lax.broadcasted_iota(jnp.int32, (N_KEYS, LANES), 0)
    lane = lax.broadcasted_iota(jnp.int32, (N_KEYS, LANES), 1)
    lane1 = lax.broadcasted_iota(jnp.int32, (1, LANES), 1)
    qry = jnp.bitwise_and(lane, seq - 1)
    valid = (key >= qry) & (key <= qry + WINDOW)

    probs, dens = [], []
    for half in range(2):
        s = jnp.zeros((N_KEYS, LANES), F32)
        for kv in range(n_kv):
            s_kv = lax.dot_general(kpad_ref[kv, half], qs, _TRANS_B, preferred_element_type=F32)
            s = jnp.where((lane >= kv * kv_lanes) & (lane < (kv + 1) * kv_lanes), s_kv, s)
        sink = jnp.zeros((1, LANES), F32)
        for p in range(n_pairs):
            sink = jnp.where((lane1 >= p * seq) & (lane1 < (p + 1) * seq), sinks_ref[2 * p + half], sink)
        s = jnp.where(valid, s + bias_ref[half], NEG)
        m = jnp.maximum(jnp.max(s, axis=0, keepdims=True), sink)
        e = jnp.exp(s - m)
        dens.append(jnp.sum(e, axis=0, keepdims=True) + jnp.exp(sink - m))
        probs.append(e.astype(BF16))

    row_t = lax.broadcasted_iota(jnp.int32, (LANES, LANES), 0)
    lane_t = lax.broadcasted_iota(jnp.int32, (LANES, LANES), 1)
    out_t = jnp.zeros((LANES, LANES), F32)
    for kv in range(n_kv):
        acc = (jnp.dot(vtpad_ref[kv, 0], probs[0], preferred_element_type=F32)
               + jnp.dot(vtpad_ref[kv, 1], probs[1], preferred_element_type=F32))
        out_t = jnp.where((lane_t >= kv * kv_lanes) & (lane_t < (kv + 1) * kv_lanes), acc, out_t)
    inv = jnp.where(row_t < HEAD_DIM, 1.0 / dens[0], 1.0 / dens[1])
    out = (out_t * inv).T
    for p in range(n_pairs):
        o_ref[:, p * LANES:(p + 1) * LANES] = out[p * seq:(p + 1) * seq].astype(o_ref.dtype)


def _attn_sample(qkv, att, cache_k, cache_v, bias, sinks, t_prompt, seq, n_heads):
    n_seq, kv_buf, kvw = cache_k.shape
    d = n_heads * HEAD_DIM
    assert kv_buf == BLOCK and seq == SUBLANES and t_prompt % seq == 0
    rb = t_prompt // seq
    kcol, vcol = d // kvw, d // kvw + 1
    return pl.pallas_call(
        _attn_sample_kernel,
        grid=(n_seq,),
        in_specs=[
            pl.BlockSpec(memory_space=pltpu.SMEM),
            pl.BlockSpec((seq, d), lambda n: (rb + n, 0)),
            pl.BlockSpec((seq, kvw), lambda n: (rb + n, kcol)),
            pl.BlockSpec((seq, kvw), lambda n: (rb + n, vcol)),
            pl.BlockSpec((None, kv_buf, kvw), lambda n: (n, 0, 0)),
            pl.BlockSpec((None, kv_buf, kvw), lambda n: (n, 0, 0)),
            pl.BlockSpec((2, N_KEYS, LANES), lambda n: (0, 0, 0)),
            pl.BlockSpec(memory_space=pl.ANY),
        ],
        out_specs=pl.BlockSpec((seq, d), lambda n: (rb + n, 0)),
        out_shape=jax.ShapeDtypeStruct(att.shape, att.dtype),
        scratch_shapes=[
            pltpu.VMEM((kvw // HEAD_DIM, 2, N_KEYS, LANES), BF16),
            pltpu.VMEM((kvw // HEAD_DIM, 2, LANES, N_KEYS), BF16),
        ],
        input_output_aliases={7: 0},
        compiler_params=_params("arbitrary"),
        name="attn_sample",
    )(sinks, qkv, qkv, qkv, cache_k, cache_v, bias, att)


def kernel(x_prompt, x_sample, state_conv, state_pool, cache_k, cache_v, norm_g, w_ffn_in,
           w_ffn_out, w_mix_in, conv_w, pool_w, pool_scale, w_mix_out, w_qkv, w_o,
           attn_sinks, rel_bias):
    n_p, seq_p, d = x_prompt.shape
    n_s, seq_s, _ = x_sample.shape
    depth = norm_g.shape[0]
    n_heads = d // HEAD_DIM
    n_kv = cache_k.shape[3]
    kv_buf = cache_k.shape[2]
    kvw = n_kv * HEAD_DIM
    conv_ctx = state_conv.shape[2]
    pool_ctx = state_pool.shape[2]
    t_p, t_s = n_p * seq_p, n_s * seq_s
    t = t_p + t_s
    tm_ffn = _divisor_tile(t, 1408, 16)
    tm = _divisor_tile(t, 1056, 16)

    x = jnp.concatenate([x_prompt.reshape(t_p, d), x_sample.reshape(t_s, d)], axis=0)
    bias_p, bias_s = _bias_tables(rel_bias, seq_s)

    conv_p, pool_p, k_p, v_p, conv_s, pool_s, k_s, v_s = ([] for _ in range(8))
    for i in range(depth):
        x = _ffn(x, norm_g, w_ffn_in, w_ffn_out, i, 0, tm_ffn)
        j = i // 2
        if i % 2 == 0:
            z = _proj(x, norm_g, i, w_mix_in, j, tm)
            m, cst, pst = _mix_prompt(z, conv_w[j], pool_w[j], pool_scale[j][None], n_p, seq_p, t)
            conv_p.append(cst[:, CONV_HALO - conv_ctx:])
            pool_p.append(pst[:, POOL_HALO - pool_ctx:])
            cst_in = jnp.pad(state_conv[j], ((0, 0), (CONV_HALO - conv_ctx, 0), (0, 0)))
            pst_in = jnp.pad(state_pool[j], ((0, 0), (POOL_HALO - pool_ctx, 0), (0, 0)))
            m, cst, pst = _mix_sample(z, m, cst_in, pst_in, conv_w[j], pool_w[j], pool_scale[j][None],
                                      t_p, seq_s, PAST_LEN)
            conv_s.append(cst[:, CONV_HALO - conv_ctx:])
            pool_s.append(pst[:, POOL_HALO - pool_ctx:])
            x = _out(m, w_mix_out, j, x, norm_g, i, tm)
        else:
            qkv = _proj(x, norm_g, i, w_qkv, j, tm)
            att = _attn_prompt(qkv, bias_p, attn_sinks[j], n_p, seq_p, n_heads, n_kv, t)
            ck = cache_k[j].reshape(n_s, kv_buf, kvw)
            cv = cache_v[j].reshape(n_s, kv_buf, kvw)
            att = _attn_sample(qkv, att, ck, cv, bias_s, attn_sinks[j], t_p, seq_s, n_heads)
            k_new, v_new = qkv[:, d:d + kvw], qkv[:, d + kvw:]
            kv_p = lambda a: a[:t_p].reshape(n_p, seq_p, n_kv, HEAD_DIM)[:, seq_p - kv_buf:]
            k_p.append(kv_p(k_new))
            v_p.append(kv_p(v_new))
            kv_s = lambda c, a: jnp.concatenate(
                [c, a[t_p:].reshape(n_s, seq_s, kvw)], axis=1)[:, seq_s:].reshape(n_s, kv_buf, n_kv, HEAD_DIM)
            k_s.append(kv_s(ck, k_new))
            v_s.append(kv_s(cv, v_new))
            x = _out(att, w_o, j, x, norm_g, i, tm)
        x = _ffn(x, norm_g, w_ffn_in, w_ffn_out, i, 1, tm_ffn)

    y_p = x[:t_p].reshape(n_p, seq_p, d)
    y_s = x[t_p:].reshape(n_s, seq_s, d)
    st = jnp.stack
    return (y_p, y_s, st(conv_p), st(pool_p), st(k_p), st(v_p), st(conv_s), st(pool_s), st(k_s), st(v_s))
```

```python
import functools
import math

import jax
import jax.numpy as jnp
from jax import lax
from jax.experimental import pallas as pl
from jax.experimental.pallas import tpu as pltpu

F32 = jnp.float32
BF16 = jnp.bfloat16

EPS = 1e-6
NEG = -1e30
HEAD_DIM = 64
WINDOW = 128
BLOCK = WINDOW
CONV_K = 3
POOL_WINDOWS = (2, 4, 8, 16)
N_BUCKETS = 32
MAX_DISTANCE = 128
PAST_LEN = 16384

LANES = 128
SUBLANES = 8
CONV_HALO = SUBLANES
POOL_HALO = 2 * SUBLANES
FF_TILE = 2 * LANES
VMEM_LIMIT_BYTES = 58 * 1024 * 1024
ROW_TILE = 1056
FFN_ROW_TILE = 1408
PROJ_ROW_TILE = 2112


def _divisor_tile(n, target, mult):
    best = None
    for t in range(mult, min(n, target) + 1, mult):
        if n % t == 0:
            best = t
    assert best is not None, (n, target, mult)
    return best


def _params(*sem):
    return pltpu.CompilerParams(dimension_semantics=sem, vmem_limit_bytes=VMEM_LIMIT_BYTES)


def _rms(xf, g):
    ms = jnp.mean(xf * xf, axis=-1, keepdims=True)
    return xf * lax.rsqrt(ms + EPS) * g


def _for_row_chunks(n_rows, body):
    rc = _divisor_tile(n_rows, 272, 16)

    def step(c, carry):
        body(pl.ds(pl.multiple_of(c * rc, rc), rc))
        return carry

    lax.fori_loop(0, n_rows // rc, step, 0)


def _merge_kernel(xp_ref, xs_ref, g_ref, x_ref, h_ref, *, n_prompt_blocks, row):
    x = jnp.where(pl.program_id(0) < n_prompt_blocks, xp_ref[...], xs_ref[...])
    x_ref[...] = x
    h_ref[...] = _rms(x, g_ref[row:row + 1, :]).astype(BF16)


def _merge(x_prompt, x_sample, norm_g, layer, row):
    (t_p, d), t_s = x_prompt.shape, x_sample.shape[0]
    assert t_p % t_s == 0 and t_s % 16 == 0
    nbp = t_p // t_s
    row_block = pl.BlockSpec((t_s, d), lambda i: (i, 0))
    return pl.pallas_call(
        functools.partial(_merge_kernel, n_prompt_blocks=nbp, row=row),
        grid=(nbp + 1,),
        in_specs=[
            pl.BlockSpec((t_s, d), lambda i: (jnp.minimum(i, nbp - 1), 0)),
            pl.BlockSpec((t_s, d), lambda i: (0, 0)),
            pl.BlockSpec((None,) + norm_g.shape[1:], lambda i: (layer, 0, 0)),
        ],
        out_specs=[row_block, row_block],
        out_shape=[jax.ShapeDtypeStruct((t_p + t_s, d), F32), jax.ShapeDtypeStruct((t_p + t_s, d), BF16)],
        compiler_params=_params("parallel"),
        name="merge_rows",
    )(x_prompt, x_sample, norm_g)


def _ffn_kernel(h_ref, xr_ref, g_ref, gn_ref, wg_ref, wul_ref, wuh_ref, wol_ref, woh_ref,
                xo_ref, *rest, d_ff, nj, nb, nc, post, nxt):
    ho_ref, acc_ref = rest if nxt is not None else (None,) + rest
    i, j = pl.program_id(0), pl.program_id(1)
    tm, d = h_ref.shape
    rc = xr_ref.shape[0]
    cur = lax.rem(i, 2)
    prv = 1 - cur

    @pl.when((i == 0) & (j == 0))
    def _():
        def zero(rows):
            acc_ref[0, rows, :] = jnp.zeros((rows.size, d), F32)
            acc_ref[1, rows, :] = jnp.zeros((rows.size, d), F32)
        _for_row_chunks(tm, zero)

    def tile():
        w = jnp.concatenate([wg_ref[...].astype(BF16), wul_ref[...].astype(BF16),
                             wuh_ref[...].astype(BF16)], axis=1)
        r = jnp.dot(h_ref[...], w, preferred_element_type=F32)
        gate, up = r[:, :FF_TILE], r[:, FF_TILE:]
        a = (gate * jax.nn.sigmoid(gate)) * up
        if d_ff % FF_TILE:
            col = lax.broadcasted_iota(jnp.int32, a.shape, 1)
            a = jnp.where(col < d_ff - j * FF_TILE, a, 0.0)
        a = a.astype(BF16)
        wo = jnp.concatenate([wol_ref[...].astype(BF16), woh_ref[...].astype(BF16)], axis=0)
        wc = _divisor_tile(d, 512, LANES)
        for c in range(d // wc):
            acc_ref[cur, :, c * wc:(c + 1) * wc] += jnp.dot(a, wo[:, c * wc:(c + 1) * wc],
                                                            preferred_element_type=F32)

    def finish_chunk():
        rows = pl.ds(pl.multiple_of(j * rc, rc), rc)
        y = xr_ref[...] + 0.5 * _rms(acc_ref[prv, rows, :], g_ref[post:post + 1, :])
        xo_ref[...] = y
        if ho_ref is not None:
            ho_ref[...] = _rms(y, gn_ref[nxt:nxt + 1, :]).astype(BF16)
        acc_ref[prv, rows, :] = jnp.zeros((rc, d), F32)

    real = i < nb

    @pl.when(real & (j < nc))
    def _():
        finish_chunk()
        tile()

    @pl.when(real & (j >= nc))
    def _():
        tile()

    @pl.when(jnp.logical_not(real) & (j < nc))
    def _():
        finish_chunk()


def _ffn(h, x, norm_g, w_in, w_out, layer, which, nxt, tm):
    t, d = x.shape
    d_ff = w_out.shape[2]
    half = FF_TILE // 2
    assert d_ff % half == 0 and w_in.shape[2:] == (d, 2 * d_ff) and t % tm == 0
    n_half = d_ff // half
    nj = pl.cdiv(d_ff, FF_TILE)
    nb = t // tm
    last_up = 2 * n_half - 1
    rc = min(c for c in range(16, tm + 1, 16) if tm % c == 0 and tm // c <= nj)
    nc = tm // rc
    nxt_layer, nxt_row = nxt if nxt is not None else (layer, None)
    kern = functools.partial(_ffn_kernel, d_ff=d_ff, nj=nj, nb=nb, nc=nc, post=4 * which + 1, nxt=nxt_row)
    jw = lambda i, j: jnp.where(i < nb, j, nj - 1)
    w_in_spec = lambda width, col: pl.BlockSpec(
        (None, None, d, width), lambda i, j: (layer, which, 0, col(jw(i, j))))
    w_out_spec = lambda row: pl.BlockSpec(
        (None, None, half, d), lambda i, j: (layer, which, row(jw(i, j)), 0))
    chunk = lambda i, j: (jnp.maximum((i - 1) * nc + jnp.minimum(j, nc - 1), 0), 0)
    g_spec = lambda l: pl.BlockSpec((None,) + norm_g.shape[1:], lambda i, j: (l, 0, 0))
    out_specs = [pl.BlockSpec((rc, d), chunk)]
    out_shape = [jax.ShapeDtypeStruct((t, d), F32)]
    if nxt is not None:
        out_specs.append(pl.BlockSpec((rc, d), chunk))
        out_shape.append(jax.ShapeDtypeStruct((t, d), BF16))
    res = pl.pallas_call(
        kern,
        grid=(nb + 1, nj),
        in_specs=[
            pl.BlockSpec((tm, d), lambda i, j: (jnp.minimum(i, nb - 1), 0)),
            pl.BlockSpec((rc, d), chunk),
            g_spec(layer),
            g_spec(nxt_layer),
            w_in_spec(FF_TILE, lambda j: j),
            w_in_spec(half, lambda j: n_half + 2 * j),
            w_in_spec(half, lambda j: jnp.minimum(n_half + 2 * j + 1, last_up)),
            w_out_spec(lambda j: 2 * j),
            w_out_spec(lambda j: jnp.minimum(2 * j + 1, n_half - 1)),
        ],
        out_specs=out_specs,
        out_shape=out_shape,
        scratch_shapes=[pltpu.VMEM((2, tm, d), F32)],
        compiler_params=_params("arbitrary", "arbitrary"),
        name="ffn",
    )(h, x, norm_g, norm_g, w_in, w_in, w_in, w_out, w_out)
    return (res[0], res[1]) if nxt is not None else (res[0], None)


FFN1_PRE, MIX_PRE, MIX_POST, FFN2_PRE = 0, 2, 3, 4


def _proj_kernel(h_ref, w_ref, o_ref):
    o_ref[...] = jnp.dot(h_ref[...], w_ref[...].astype(BF16), preferred_element_type=F32)


def _proj(h, w, widx, tm):
    t, d = h.shape
    n = w.shape[2]
    tn = _divisor_tile(n, 512, LANES)
    return pl.pallas_call(
        _proj_kernel,
        grid=(t // tm, n // tn),
        in_specs=[
            pl.BlockSpec((tm, d), lambda i, j: (i, 0)),
            pl.BlockSpec((None, d, tn), lambda i, j: (widx, 0, j)),
        ],
        out_specs=pl.BlockSpec((tm, tn), lambda i, j: (i, j)),
        out_shape=jax.ShapeDtypeStruct((t, n), F32),
        compiler_params=_params("parallel", "arbitrary"),
        name="proj",
    )(h, w)


def _out_kernel(m_ref, w_ref, x_ref, g_ref, o_ref, ho_ref, *, nk):
    k = pl.program_id(1)
    tm, d = x_ref.shape

    @pl.when(k == 0)
    def _():
        o_ref[...] = jnp.zeros_like(o_ref)

    w = w_ref[...].astype(BF16)
    nc = _divisor_tile(d, 512, LANES)
    for c in range(d // nc):
        o_ref[:, c * nc:(c + 1) * nc] += jnp.dot(m_ref[...], w[:, c * nc:(c + 1) * nc],
                                                 preferred_element_type=F32)

    @pl.when(k == nk - 1)
    def _():
        def fin(rows):
            y = x_ref[rows, :] + _rms(o_ref[rows, :], g_ref[MIX_POST:MIX_POST + 1, :])
            o_ref[rows, :] = y
            ho_ref[rows, :] = _rms(y, g_ref[FFN2_PRE:FFN2_PRE + 1, :]).astype(BF16)
        _for_row_chunks(tm, fin)


def _out(m, w, widx, x, norm_g, layer, tm):
    t, d = x.shape
    kdim = w.shape[1]
    tk = _divisor_tile(kdim, 512, LANES)
    nk = kdim // tk
    row_block = pl.BlockSpec((tm, d), lambda i, k: (i, 0))
    return pl.pallas_call(
        functools.partial(_out_kernel, nk=nk),
        grid=(t // tm, nk),
        in_specs=[
            pl.BlockSpec((tm, tk), lambda i, k: (i, k)),
            pl.BlockSpec((None, tk, d), lambda i, k: (widx, k, 0)),
            row_block,
            pl.BlockSpec((None,) + norm_g.shape[1:], lambda i, k: (layer, 0, 0)),
        ],
        out_specs=[row_block, row_block],
        out_shape=[jax.ShapeDtypeStruct((t, d), F32), jax.ShapeDtypeStruct((t, d), BF16)],
        compiler_params=_params("parallel", "arbitrary"),
        name="outproj",
    )(m, w, x, norm_g)


def _conv3(ext, cw):
    return cw[0:1] * pltpu.roll(ext, 2, 0) + cw[1:2] * pltpu.roll(ext, 1, 0) + cw[2:3] * ext


def _window_sum(ext, w):
    s, k = ext, 1
    while k < w:
        s = s + pltpu.roll(s, k, 0)
        k *= 2
    return s


def _pool_group(win, cnt, ug, pw, scale):
    dlt = win / cnt - ug
    return jnp.dot(dlt.astype(BF16), pw.astype(BF16), preferred_element_type=F32) * scale


def _mix_prompt_kernel(hc_ref, gc_ref, gb_ref, u_ref, hch_ref, gch_ref, uh_ref, cw_ref, pw_ref,
                       ps_ref, m_ref, cst_ref, pst_ref, *, chunks_per_seq):
    rows, c = hc_ref.shape
    gw = c // len(POOL_WINDOWS)
    ci = pl.program_id(0) % chunks_per_seq
    first = ci == 0

    v = gc_ref[...] * hc_ref[...]
    v_halo = jnp.where(first, 0.0, gch_ref[...] * hch_ref[...])
    y = _conv3(jnp.concatenate([v_halo, v], axis=0), cw_ref[...])[CONV_HALO:]
    m_ref[:, :c] = (gb_ref[...] * y).astype(BF16)
    cst_ref[...] = v[rows - CONV_HALO:]

    u = u_ref[...]
    u_ext = jnp.concatenate([jnp.where(first, 0.0, uh_ref[...]), u], axis=0)
    pst_ref[...] = u_ext[rows:]
    pos1 = ci * rows + lax.broadcasted_iota(jnp.int32, (rows, 1), 0) + 1
    for gi, w in enumerate(POOL_WINDOWS):
        sl = slice(gi * gw, (gi + 1) * gw)
        win = _window_sum(u_ext[:, sl], w)[POOL_HALO:]
        cnt = jnp.minimum(w, pos1).astype(F32)
        yp = _pool_group(win, cnt, u[:, sl], pw_ref[gi], ps_ref[:, sl])
        m_ref[:, c + gi * gw:c + (gi + 1) * gw] = yp.astype(BF16)


def _mix_prompt(z, conv_w, pool_w, pool_scale, n_seq, seq, t_total):
    c = conv_w.shape[1]
    assert z.shape[1] == 4 * c
    rows = _divisor_tile(seq, 256, POOL_HALO)
    cps = seq // rows
    n_chunks = n_seq * cps
    rh_c, rh_p = rows // CONV_HALO, rows // POOL_HALO

    def halo(ratio, col):
        return lambda i: (jnp.maximum(i * ratio - 1, 0), col)

    kern = functools.partial(_mix_prompt_kernel, chunks_per_seq=cps)
    gw = c // len(POOL_WINDOWS)
    return pl.pallas_call(
        kern,
        grid=(n_chunks,),
        in_specs=[
            pl.BlockSpec((rows, c), lambda i: (i, 0)),
            pl.BlockSpec((rows, c), lambda i: (i, 1)),
            pl.BlockSpec((rows, c), lambda i: (i, 2)),
            pl.BlockSpec((rows, c), lambda i: (i, 3)),
            pl.BlockSpec((CONV_HALO, c), halo(rh_c, 0)),
            pl.BlockSpec((CONV_HALO, c), halo(rh_c, 1)),
            pl.BlockSpec((POOL_HALO, c), halo(rh_p, 3)),
            pl.BlockSpec((CONV_K, c), lambda i: (0, 0)),
            pl.BlockSpec((len(POOL_WINDOWS), gw, gw), lambda i: (0, 0, 0)),
            pl.BlockSpec((1, c), lambda i: (0, 0)),
        ],
        out_specs=[
            pl.BlockSpec((rows, 2 * c), lambda i: (i, 0)),
            pl.BlockSpec((None, CONV_HALO, c), lambda i: (i // cps, 0, 0)),
            pl.BlockSpec((None, POOL_HALO, c), lambda i: (i // cps, 0, 0)),
        ],
        out_shape=[
            jax.ShapeDtypeStruct((t_total, 2 * c), BF16),
            jax.ShapeDtypeStruct((n_seq, CONV_HALO, c), F32),
            jax.ShapeDtypeStruct((n_seq, POOL_HALO, c), F32),
        ],
        compiler_params=_params("arbitrary"),
        name="mix_prompt",
    )(z, z, z, z, z, z, z, conv_w, pool_w, pool_scale)


def _mix_sample_kernel(hc_ref, gc_ref, gb_ref, u_ref, cst_in_ref, pst_in_ref, cw_ref, pw_ref,
                       ps_ref, m_in_ref, m_ref, cst_ref, pst_ref, *, seq, past_len):
    del m_in_ref
    rows, c = hc_ref.shape
    n_seq = rows // seq
    gw = c // len(POOL_WINDOWS)

    def with_halo(halo3, x2):
        ext = jnp.concatenate([halo3, x2.reshape(n_seq, seq, c)], axis=1)
        return ext, ext.reshape(n_seq * ext.shape[1], c)

    def body_rows(flat, n_halo):
        return flat.reshape(n_seq, n_halo + seq, flat.shape[-1])[:, n_halo:].reshape(rows, flat.shape[-1])

    v = gc_ref[...] * hc_ref[...]
    v_ext3, v_ext = with_halo(cst_in_ref[...], v)
    y = body_rows(_conv3(v_ext, cw_ref[...]), CONV_HALO)
    m_ref[:, :c] = (gb_ref[...] * y).astype(BF16)
    cst_ref[...] = v_ext3[:, seq:]

    u = u_ref[...]
    u_ext3, u_ext = with_halo(pst_in_ref[...], u)
    pst_ref[...] = u_ext3[:, seq:]
    pos1 = past_len + jnp.bitwise_and(lax.broadcasted_iota(jnp.int32, (rows, 1), 0), seq - 1) + 1
    for gi, w in enumerate(POOL_WINDOWS):
        sl = slice(gi * gw, (gi + 1) * gw)
        win = body_rows(_window_sum(u_ext[:, sl], w), POOL_HALO)
        cnt = jnp.minimum(w, pos1).astype(F32)
        yp = _pool_group(win, cnt, u[:, sl], pw_ref[gi], ps_ref[:, sl])
        m_ref[:, c + gi * gw:c + (gi + 1) * gw] = yp.astype(BF16)


def _mix_sample(z, m, conv_state, pool_state, conv_w, pool_w, pool_scale, t_prompt, seq, past_len):
    c = conv_w.shape[1]
    n_seq = conv_state.shape[0]
    rows = n_seq * seq
    assert seq == SUBLANES and t_prompt % rows == 0
    rb = t_prompt // rows
    gw = c // len(POOL_WINDOWS)
    full3 = lambda i: (0, 0, 0)
    kern = functools.partial(_mix_sample_kernel, seq=seq, past_len=past_len)
    return pl.pallas_call(
        kern,
        grid=(1,),
        in_specs=[
            pl.BlockSpec((rows, c), lambda i: (rb, 0)),
            pl.BlockSpec((rows, c), lambda i: (rb, 1)),
            pl.BlockSpec((rows, c), lambda i: (rb, 2)),
            pl.BlockSpec((rows, c), lambda i: (rb, 3)),
            pl.BlockSpec((n_seq, CONV_HALO, c), full3),
            pl.BlockSpec((n_seq, POOL_HALO, c), full3),
            pl.BlockSpec((CONV_K, c), lambda i: (0, 0)),
            pl.BlockSpec((len(POOL_WINDOWS), gw, gw), full3),
            pl.BlockSpec((1, c), lambda i: (0, 0)),
            pl.BlockSpec(memory_space=pl.ANY),
        ],
        out_specs=[
            pl.BlockSpec((rows, 2 * c), lambda i: (rb, 0)),
            pl.BlockSpec((n_seq, CONV_HALO, c), full3),
            pl.BlockSpec((n_seq, POOL_HALO, c), full3),
        ],
        out_shape=[
            jax.ShapeDtypeStruct(m.shape, m.dtype),
            jax.ShapeDtypeStruct((n_seq, CONV_HALO, c), F32),
            jax.ShapeDtypeStruct((n_seq, POOL_HALO, c), F32),
        ],
        input_output_aliases={9: 0},
        compiler_params=_params("arbitrary"),
        name="mix_sample",
    )(z, z, z, z, conv_state, pool_state, conv_w, pool_w, pool_scale, m)


N_KEYS = 2 * BLOCK


def _t5_bucket(dist):
    max_exact = N_BUCKETS // 2
    n = jnp.maximum(dist, 0)
    ratio = jnp.log(jnp.maximum(n, 1).astype(F32) / max_exact) / math.log(MAX_DISTANCE / max_exact)
    large = jnp.minimum(max_exact + (ratio * (N_BUCKETS - max_exact)).astype(jnp.int32), N_BUCKETS - 1)
    return jnp.where(n < max_exact, n, large)


def _bias_prompt_kernel(rb_ref, o_ref):
    h = pl.program_id(0)
    s = lax.broadcasted_iota(jnp.int32, o_ref.shape, 0)
    q = lax.broadcasted_iota(jnp.int32, o_ref.shape, 1)
    bucket = _t5_bucket(BLOCK + q - s)
    acc = jnp.zeros(o_ref.shape, F32)
    for b in range(N_BUCKETS):
        acc = jnp.where(bucket == b, rb_ref[b, h], acc)
    o_ref[...] = acc


def _bias_sample_kernel(rb_ref, o_ref, *, seq, n_heads):
    half = pl.program_id(0)
    s = lax.broadcasted_iota(jnp.int32, o_ref.shape, 0)
    lane = lax.broadcasted_iota(jnp.int32, o_ref.shape, 1)
    bucket = _t5_bucket(BLOCK + jnp.bitwise_and(lane, seq - 1) - s)
    lane1 = lax.broadcasted_iota(jnp.int32, (1, LANES), 1)
    acc = jnp.zeros(o_ref.shape, F32)
    for b in range(N_BUCKETS):
        vec = jnp.zeros((1, LANES), F32)
        for slab in range(n_heads // 2):
            vec = jnp.where((lane1 >= slab * seq) & (lane1 < (slab + 1) * seq), rb_ref[b, 2 * slab + half], vec)
        acc = jnp.where(bucket == b, vec, acc)
    o_ref[...] = acc


def _bias_tables(rel_bias, seq_s):
    n_heads = rel_bias.shape[1]
    assert seq_s == SUBLANES and (n_heads // 2) * seq_s <= LANES
    prompt = pl.pallas_call(
        _bias_prompt_kernel,
        grid=(n_heads,),
        in_specs=[pl.BlockSpec(memory_space=pltpu.SMEM)],
        out_specs=pl.BlockSpec((None, N_KEYS, BLOCK), lambda h: (h, 0, 0)),
        out_shape=jax.ShapeDtypeStruct((n_heads, N_KEYS, BLOCK), F32),
        compiler_params=_params("arbitrary"),
        name="rel_bias_prompt",
    )(rel_bias)
    sample = pl.pallas_call(
        functools.partial(_bias_sample_kernel, seq=seq_s, n_heads=n_heads),
        grid=(2,),
        in_specs=[pl.BlockSpec(memory_space=pltpu.SMEM)],
        out_specs=pl.BlockSpec((None, N_KEYS, LANES), lambda h: (h, 0, 0)),
        out_shape=jax.ShapeDtypeStruct((2, N_KEYS, LANES), F32),
        compiler_params=_params("arbitrary"),
        name="rel_bias_sample",
    )(rel_bias)
    return prompt, sample


def _stage_kv(k_all, v_all, kpad_ref, vtpad_ref):
    n_keys, kvw = k_all.shape
    lane = lax.broadcasted_iota(jnp.int32, (n_keys, LANES), 1)
    row = lax.broadcasted_iota(jnp.int32, (LANES, n_keys), 0)
    for slab in range(kvw // LANES):
        ks = k_all[:, slab * LANES:(slab + 1) * LANES]
        vt = v_all[:, slab * LANES:(slab + 1) * LANES].T
        for own in range(2):
            kv = 2 * slab + own
            k_own = jnp.where((lane >= own * HEAD_DIM) & (lane < (own + 1) * HEAD_DIM), ks, 0.0)
            v_own = jnp.where((row >= own * HEAD_DIM) & (row < (own + 1) * HEAD_DIM), vt, 0.0)
            kpad_ref[kv, own] = k_own.astype(BF16)
            kpad_ref[kv, 1 - own] = pltpu.roll(k_own, HEAD_DIM, 1).astype(BF16)
            vtpad_ref[kv, own] = v_own.astype(BF16)
            vtpad_ref[kv, 1 - own] = pltpu.roll(v_own, HEAD_DIM, 0).astype(BF16)


_TRANS_B = (((1,), (1,)), ((), ()))


def _attn_prompt_kernel(sinks_ref, q_ref, kc_ref, kp_ref, vc_ref, vp_ref, bias_ref, o_ref,
                        kpad_ref, vtpad_ref, s_ref, e_ref, m_ref, den_ref):
    first = pl.program_id(1) == 0
    n_heads = bias_ref.shape[0]
    group = n_heads // kpad_ref.shape[0]
    _stage_kv(jnp.concatenate([kp_ref[...], kc_ref[...]], axis=0),
              jnp.concatenate([vp_ref[...], vc_ref[...]], axis=0), kpad_ref, vtpad_ref)
    key = lax.broadcasted_iota(jnp.int32, (N_KEYS, BLOCK), 0)
    qry = lax.broadcasted_iota(jnp.int32, (N_KEYS, BLOCK), 1)
    valid = (key >= qry) & (key <= qry + WINDOW) & ((key >= BLOCK) | jnp.logical_not(first))

    for p in range(n_heads // 2):
        q_slab = (q_ref[:, p * LANES:(p + 1) * LANES] * HEAD_DIM ** -0.5).astype(BF16)
        for half in range(2):
            head = 2 * p + half
            s = lax.dot_general(kpad_ref[head // group, half], q_slab, _TRANS_B, preferred_element_type=F32)
            s = jnp.where(valid, s + bias_ref[head], NEG)
            s_ref[head] = s
            m_ref[head:head + 1, :] = jnp.maximum(jnp.max(s, axis=0, keepdims=True), sinks_ref[head])
    for head in range(n_heads):
        m = m_ref[head:head + 1, :]
        e = jnp.exp(s_ref[head] - m)
        den_ref[head:head + 1, :] = jnp.sum(e, axis=0, keepdims=True) + jnp.exp(sinks_ref[head] - m)
        e_ref[head] = e.astype(BF16)
    first_head = lax.broadcasted_iota(jnp.int32, (LANES, BLOCK), 0) < HEAD_DIM
    for p in range(n_heads // 2):
        kv = 2 * p // group
        acc = (jnp.dot(vtpad_ref[kv, 0], e_ref[2 * p], preferred_element_type=F32)
               + jnp.dot(vtpad_ref[kv, 1], e_ref[2 * p + 1], preferred_element_type=F32))
        inv = jnp.where(first_head, 1.0 / den_ref[2 * p:2 * p + 1, :], 1.0 / den_ref[2 * p + 1:2 * p + 2, :])
        o_ref[:, p * LANES:(p + 1) * LANES] = (acc * inv).T.astype(o_ref.dtype)


def _attn_prompt(qkv, bias, sinks, n_seq, seq, n_heads, n_kv, t_total):
    d = n_heads * HEAD_DIM
    kvw = n_kv * HEAD_DIM
    assert kvw % LANES == 0 and d % kvw == 0 and seq % BLOCK == 0
    nb = seq // BLOCK
    kcol, vcol = d // kvw, d // kvw + 1
    cur = lambda col: (lambda n, b: (n * nb + b, col))
    prev = lambda col: (lambda n, b: (n * nb + jnp.maximum(b - 1, 0), col))
    return pl.pallas_call(
        _attn_prompt_kernel,
        grid=(n_seq, nb),
        in_specs=[
            pl.BlockSpec(memory_space=pltpu.SMEM),
            pl.BlockSpec((BLOCK, d), cur(0)),
            pl.BlockSpec((BLOCK, kvw), cur(kcol)),
            pl.BlockSpec((BLOCK, kvw), prev(kcol)),
            pl.BlockSpec((BLOCK, kvw), cur(vcol)),
            pl.BlockSpec((BLOCK, kvw), prev(vcol)),
            pl.BlockSpec((n_heads, N_KEYS, BLOCK), lambda n, b: (0, 0, 0), pipeline_mode=pl.Buffered(1)),
        ],
        out_specs=pl.BlockSpec((BLOCK, d), cur(0)),
        out_shape=jax.ShapeDtypeStruct((t_total, d), BF16),
        scratch_shapes=[
            pltpu.VMEM((n_kv, 2, N_KEYS, LANES), BF16),
            pltpu.VMEM((n_kv, 2, LANES, N_KEYS), BF16),
            pltpu.VMEM((n_heads, N_KEYS, BLOCK), F32),
            pltpu.VMEM((n_heads, N_KEYS, BLOCK), BF16),
            pltpu.VMEM((n_heads, BLOCK), F32),
            pltpu.VMEM((n_heads, BLOCK), F32),
        ],
        compiler_params=_params("parallel", "arbitrary"),
        name="attn_prompt",
    )(sinks, qkv, qkv, qkv, qkv, qkv, bias)


def _attn_sample_kernel(sinks_ref, q_ref, kn_ref, vn_ref, kc_ref, vc_ref, bias_ref, att_in_ref, o_ref,
                        kpad_ref, vtpad_ref):
    del att_in_ref
    seq, kvw = kn_ref.shape
    n_pairs = q_ref.shape[1] // LANES
    n_kv = kvw // HEAD_DIM
    kv_lanes = n_pairs // n_kv * seq
    pad = jnp.zeros((N_KEYS - kc_ref.shape[0] - seq, kvw), F32)
    _stage_kv(jnp.concatenate([kc_ref[...], kn_ref[...], pad], axis=0),
              jnp.concatenate([vc_ref[...], vn_ref[...], pad], axis=0), kpad_ref, vtpad_ref)

    rows = [q_ref[:, p * LANES:(p + 1) * LANES] for p in range(n_pairs)]
    if n_pairs * seq < LANES:
        rows.append(jnp.zeros((LANES - n_pairs * seq, LANES), F32))
    qs = (jnp.concatenate(rows, axis=0) * HEAD_DIM ** -0.5).astype(BF16)

    key = lax.broadcasted_iota(jnp.int32, (N_KEYS, LANES), 0)
    lane = lax.broadcasted_iota(jnp.int32, (N_KEYS, LANES), 1)
    lane1 = lax.broadcasted_iota(jnp.int32, (1, LANES), 1)
    qry = jnp.bitwise_and(lane, seq - 1)
    valid = (key >= qry) & (key <= qry + WINDOW)

    probs, dens = [], []
    for half in range(2):
        s = jnp.zeros((N_KEYS, LANES), F32)
        for kv in range(n_kv):
            s_kv = lax.dot_general(kpad_ref[kv, half], qs, _TRANS_B, preferred_element_type=F32)
            s = jnp.where((lane >= kv * kv_lanes) & (lane < (kv + 1) * kv_lanes), s_kv, s)
        sink = jnp.zeros((1, LANES), F32)
        for p in range(n_pairs):
            sink = jnp.where((lane1 >= p * seq) & (lane1 < (p + 1) * seq), sinks_ref[2 * p + half], sink)
        s = jnp.where(valid, s + bias_ref[half], NEG)
        m = jnp.maximum(jnp.max(s, axis=0, keepdims=True), sink)
        e = jnp.exp(s - m)
        dens.append(jnp.sum(e, axis=0, keepdims=True) + jnp.exp(sink - m))
        probs.append(e.astype(BF16))

    row_t = lax.broadcasted_iota(jnp.int32, (LANES, LANES), 0)
    lane_t = lax.broadcasted_iota(jnp.int32, (LANES, LANES), 1)
    out_t = jnp.zeros((LANES, LANES), F32)
    for kv in range(n_kv):
        acc = (jnp.dot(vtpad_ref[kv, 0], probs[0], preferred_element_type=F32)
               + jnp.dot(vtpad_ref[kv, 1], probs[1], preferred_element_type=F32))
        out_t = jnp.where((lane_t >= kv * kv_lanes) & (lane_t < (kv + 1) * kv_lanes), acc, out_t)
    inv = jnp.where(row_t < HEAD_DIM, 1.0 / dens[0], 1.0 / dens[1])
    out = (out_t * inv).T
    for p in range(n_pairs):
        o_ref[:, p * LANES:(p + 1) * LANES] = out[p * seq:(p + 1) * seq].astype(o_ref.dtype)


def _attn_sample(qkv, att, cache_k, cache_v, bias, sinks, t_prompt, seq, n_heads):
    n_seq, kv_buf, kvw = cache_k.shape
    d = n_heads * HEAD_DIM
    assert kv_buf == BLOCK and seq == SUBLANES and t_prompt % seq == 0
    rb = t_prompt // seq
    kcol, vcol = d // kvw, d // kvw + 1
    return pl.pallas_call(
        _attn_sample_kernel,
        grid=(n_seq,),
        in_specs=[
            pl.BlockSpec(memory_space=pltpu.SMEM),
            pl.BlockSpec((seq, d), lambda n: (rb + n, 0)),
            pl.BlockSpec((seq, kvw), lambda n: (rb + n, kcol)),
            pl.BlockSpec((seq, kvw), lambda n: (rb + n, vcol)),
            pl.BlockSpec((None, kv_buf, kvw), lambda n: (n, 0, 0)),
            pl.BlockSpec((None, kv_buf, kvw), lambda n: (n, 0, 0)),
            pl.BlockSpec((2, N_KEYS, LANES), lambda n: (0, 0, 0)),
            pl.BlockSpec(memory_space=pl.ANY),
        ],
        out_specs=pl.BlockSpec((seq, d), lambda n: (rb + n, 0)),
        out_shape=jax.ShapeDtypeStruct(att.shape, att.dtype),
        scratch_shapes=[
            pltpu.VMEM((kvw // HEAD_DIM, 2, N_KEYS, LANES), BF16),
            pltpu.VMEM((kvw // HEAD_DIM, 2, LANES, N_KEYS), BF16),
        ],
        input_output_aliases={7: 0},
        compiler_params=_params("arbitrary"),
        name="attn_sample",
    )(sinks, qkv, qkv, qkv, cache_k, cache_v, bias, att)


def kernel(x_prompt, x_sample, state_conv, state_pool, cache_k, cache_v, norm_g, w_ffn_in,
           w_ffn_out, w_mix_in, conv_w, pool_w, pool_scale, w_mix_out, w_qkv, w_o,
           attn_sinks, rel_bias):
    n_p, seq_p, d = x_prompt.shape
    n_s, seq_s, _ = x_sample.shape
    depth = norm_g.shape[0]
    n_heads = d // HEAD_DIM
    n_kv = cache_k.shape[3]
    kv_buf = cache_k.shape[2]
    kvw = n_kv * HEAD_DIM
    conv_ctx = state_conv.shape[2]
    pool_ctx = state_pool.shape[2]
    t_p, t_s = n_p * seq_p, n_s * seq_s
    t = t_p + t_s
    tm = _divisor_tile(t, ROW_TILE, 16)
    tm_proj = _divisor_tile(t, PROJ_ROW_TILE, 16)

    tm_ffn = _divisor_tile(t, FFN_ROW_TILE, 16)
    x, h = _merge(x_prompt.reshape(t_p, d), x_sample.reshape(t_s, d), norm_g, 0, FFN1_PRE)
    bias_p, bias_s = _bias_tables(rel_bias, seq_s)

    conv_p, pool_p, k_p, v_p, conv_s, pool_s, k_s, v_s = ([] for _ in range(8))
    for i in range(depth):
        x, h = _ffn(h, x, norm_g, w_ffn_in, w_ffn_out, i, 0, (i, MIX_PRE), tm_ffn)
        j = i // 2
        if i % 2 == 0:
            z = _proj(h, w_mix_in, j, tm_proj)
            m, cst, pst = _mix_prompt(z, conv_w[j], pool_w[j], pool_scale[j][None], n_p, seq_p, t)
            conv_p.append(cst[:, CONV_HALO - conv_ctx:])
            pool_p.append(pst[:, POOL_HALO - pool_ctx:])
            cst_in = jnp.pad(state_conv[j], ((0, 0), (CONV_HALO - conv_ctx, 0), (0, 0)))
            pst_in = jnp.pad(state_pool[j], ((0, 0), (POOL_HALO - pool_ctx, 0), (0, 0)))
            m, cst, pst = _mix_sample(z, m, cst_in, pst_in, conv_w[j], pool_w[j], pool_scale[j][None],
                                      t_p, seq_s, PAST_LEN)
            conv_s.append(cst[:, CONV_HALO - conv_ctx:])
            pool_s.append(pst[:, POOL_HALO - pool_ctx:])
            x, h = _out(m, w_mix_out, j, x, norm_g, i, tm)
        else:
            qkv = _proj(h, w_qkv, j, tm_proj)
            att = _attn_prompt(qkv, bias_p, attn_sinks[j], n_p, seq_p, n_heads, n_kv, t)
            ck = cache_k[j].reshape(n_s, kv_buf, kvw)
            cv = cache_v[j].reshape(n_s, kv_buf, kvw)
            att = _attn_sample(qkv, att, ck, cv, bias_s, attn_sinks[j], t_p, seq_s, n_heads)
            k_new, v_new = qkv[:, d:d + kvw], qkv[:, d + kvw:]
            kv_p = lambda a: a[:t_p].reshape(n_p, seq_p, n_kv, HEAD_DIM)[:, seq_p - kv_buf:]
            k_p.append(kv_p(k_new))
            v_p.append(kv_p(v_new))
            kv_s = lambda c, a: jnp.concatenate(
                [c, a[t_p:].reshape(n_s, seq_s, kvw)], axis=1)[:, seq_s:].reshape(n_s, kv_buf, n_kv, HEAD_DIM)
            k_s.append(kv_s(ck, k_new))
            v_s.append(kv_s(cv, v_new))
            x, h = _out(att, w_o, j, x, norm_g, i, tm)
        x, h = _ffn(h, x, norm_g, w_ffn_in, w_ffn_out, i, 1, (i + 1, FFN1_PRE) if i + 1 < depth else None, tm_ffn)

    y_p = x[:t_p].reshape(n_p, seq_p, d)
    y_s = x[t_p:].reshape(n_s, seq_s, d)
    st = jnp.stack
    return (y_p, y_s, st(conv_p), st(pool_p), st(k_p), st(v_p), st(conv_s), st(pool_s), st(k_s), st(v_s))
```

```python
import functools
import math

import jax
import jax.numpy as jnp
from jax import lax
from jax.experimental import pallas as pl
from jax.experimental.pallas import tpu as pltpu

F32 = jnp.float32
BF16 = jnp.bfloat16

EPS = 1e-6
NEG = -1e30
HEAD_DIM = 64
WINDOW = 128
BLOCK = WINDOW
CONV_K = 3
POOL_WINDOWS = (2, 4, 8, 16)
N_BUCKETS = 32
MAX_DISTANCE = 128
PAST_LEN = 16384

LANES = 128
SUBLANES = 8
CONV_HALO = SUBLANES
POOL_HALO = 2 * SUBLANES
FF_TILE = 2 * LANES
VMEM_LIMIT_BYTES = 58 * 1024 * 1024
ROW_TILE = 1056
FFN_ROW_TILE = 1408
PROJ_ROW_TILE = 2112


def _divisor_tile(n, target, mult):
    best = None
    for t in range(mult, min(n, target) + 1, mult):
        if n % t == 0:
            best = t
    assert best is not None, (n, target, mult)
    return best


def _params(*sem):
    return pltpu.CompilerParams(dimension_semantics=sem, vmem_limit_bytes=VMEM_LIMIT_BYTES)


def _rms(xf, g):
    ms = jnp.mean(xf * xf, axis=-1, keepdims=True)
    return xf * lax.rsqrt(ms + EPS) * g


def _for_row_chunks(n_rows, body):
    rc = _divisor_tile(n_rows, 272, 16)

    def step(c, carry):
        body(pl.ds(pl.multiple_of(c * rc, rc), rc))
        return carry

    lax.fori_loop(0, n_rows // rc, step, 0)


def _merge_kernel(xp_ref, xs_ref, g_ref, x_ref, h_ref, *, n_prompt_blocks, row):
    x = jnp.where(pl.program_id(0) < n_prompt_blocks, xp_ref[...], xs_ref[...])
    x_ref[...] = x
    h_ref[...] = _rms(x, g_ref[row:row + 1, :]).astype(BF16)


def _merge(x_prompt, x_sample, norm_g, layer, row):
    (t_p, d), t_s = x_prompt.shape, x_sample.shape[0]
    assert t_p % t_s == 0 and t_s % 16 == 0
    nbp = t_p // t_s
    row_block = pl.BlockSpec((t_s, d), lambda i: (i, 0))
    return pl.pallas_call(
        functools.partial(_merge_kernel, n_prompt_blocks=nbp, row=row),
        grid=(nbp + 1,),
        in_specs=[
            pl.BlockSpec((t_s, d), lambda i: (jnp.minimum(i, nbp - 1), 0)),
            pl.BlockSpec((t_s, d), lambda i: (0, 0)),
            pl.BlockSpec((None,) + norm_g.shape[1:], lambda i: (layer, 0, 0)),
        ],
        out_specs=[row_block, row_block],
        out_shape=[jax.ShapeDtypeStruct((t_p + t_s, d), F32), jax.ShapeDtypeStruct((t_p + t_s, d), BF16)],
        compiler_params=_params("parallel"),
        name="merge_rows",
    )(x_prompt, x_sample, norm_g)


def _ffn_kernel(h_ref, xr_ref, g_ref, gn_ref, wg_ref, wul_ref, wuh_ref, wol_ref, woh_ref,
                xo_ref, *rest, d_ff, nj, nb, nc, post, nxt, n_head_chunks):
    assert nxt is None or n_head_chunks is None
    o2_ref, acc_ref = rest if len(rest) == 2 else (None,) + rest
    i, j = pl.program_id(0), pl.program_id(1)
    tm, d = h_ref.shape
    rc = xr_ref.shape[0]
    cur = lax.rem(i, 2)
    prv = 1 - cur

    @pl.when((i == 0) & (j == 0))
    def _():
        def zero(rows):
            acc_ref[0, rows, :] = jnp.zeros((rows.size, d), F32)
            acc_ref[1, rows, :] = jnp.zeros((rows.size, d), F32)
        _for_row_chunks(tm, zero)

    def tile():
        w = jnp.concatenate([wg_ref[...].astype(BF16), wul_ref[...].astype(BF16),
                             wuh_ref[...].astype(BF16)], axis=1)
        r = jnp.dot(h_ref[...], w, preferred_element_type=F32)
        gate, up = r[:, :FF_TILE], r[:, FF_TILE:]
        a = (gate * jax.nn.sigmoid(gate)) * up
        if d_ff % FF_TILE:
            col = lax.broadcasted_iota(jnp.int32, a.shape, 1)
            a = jnp.where(col < d_ff - j * FF_TILE, a, 0.0)
        a = a.astype(BF16)
        wo = jnp.concatenate([wol_ref[...].astype(BF16), woh_ref[...].astype(BF16)], axis=0)
        wc = _divisor_tile(d, 512, LANES)
        for c in range(d // wc):
            acc_ref[cur, :, c * wc:(c + 1) * wc] += jnp.dot(a, wo[:, c * wc:(c + 1) * wc],
                                                            preferred_element_type=F32)

    def finish_chunk():
        rows = pl.ds(pl.multiple_of(j * rc, rc), rc)
        y = xr_ref[...] + 0.5 * _rms(acc_ref[prv, rows, :], g_ref[post:post + 1, :])
        acc_ref[prv, rows, :] = jnp.zeros((rc, d), F32)
        if nxt is not None:
            xo_ref[...] = y
            o2_ref[...] = _rms(y, gn_ref[nxt:nxt + 1, :]).astype(BF16)
        elif n_head_chunks is not None:
            in_head = (i - 1) * nc + j < n_head_chunks

            @pl.when(in_head)
            def _():
                xo_ref[...] = y

            @pl.when(jnp.logical_not(in_head))
            def _():
                o2_ref[...] = y
        else:
            xo_ref[...] = y

    real = i < nb

    @pl.when(real & (j < nc))
    def _():
        finish_chunk()
        tile()

    @pl.when(real & (j >= nc))
    def _():
        tile()

    @pl.when(jnp.logical_not(real) & (j < nc))
    def _():
        finish_chunk()


def _ffn(h, x, norm_g, w_in, w_out, layer, which, tm, nxt=None, head_rows=None):
    t, d = x.shape
    d_ff = w_out.shape[2]
    half = FF_TILE // 2
    assert d_ff % half == 0 and w_in.shape[2:] == (d, 2 * d_ff) and t % tm == 0
    n_half = d_ff // half
    nj = pl.cdiv(d_ff, FF_TILE)
    nb = t // tm
    last_up = 2 * n_half - 1
    rc = min(c for c in range(16, tm + 1, 16) if tm % c == 0 and tm // c <= nj)
    nc = tm // rc
    nxt_layer, nxt_row = nxt if nxt is not None else (layer, None)
    n_head = None
    if head_rows is not None:
        assert nxt is None and head_rows % rc == 0 and (t - head_rows) % rc == 0
        n_head = head_rows // rc
    kern = functools.partial(_ffn_kernel, d_ff=d_ff, nj=nj, nb=nb, nc=nc, post=4 * which + 1, nxt=nxt_row,
                             n_head_chunks=n_head)
    jw = lambda i, j: jnp.where(i < nb, j, nj - 1)
    w_in_spec = lambda width, col: pl.BlockSpec(
        (None, None, d, width), lambda i, j: (layer, which, 0, col(jw(i, j))))
    w_out_spec = lambda row: pl.BlockSpec(
        (None, None, half, d), lambda i, j: (layer, which, row(jw(i, j)), 0))
    chunk = lambda i, j: (jnp.maximum((i - 1) * nc + jnp.minimum(j, nc - 1), 0), 0)
    g_spec = lambda l: pl.BlockSpec((None,) + norm_g.shape[1:], lambda i, j: (l, 0, 0))
    if nxt is not None:
        out_specs = [pl.BlockSpec((rc, d), chunk), pl.BlockSpec((rc, d), chunk)]
        out_shape = [jax.ShapeDtypeStruct((t, d), F32), jax.ShapeDtypeStruct((t, d), BF16)]
    elif n_head is not None:
        out_specs = [pl.BlockSpec((rc, d), lambda i, j: (jnp.minimum(chunk(i, j)[0], n_head - 1), 0)),
                     pl.BlockSpec((rc, d), lambda i, j: (jnp.maximum(chunk(i, j)[0] - n_head, 0), 0))]
        out_shape = [jax.ShapeDtypeStruct((head_rows, d), F32), jax.ShapeDtypeStruct((t - head_rows, d), F32)]
    else:
        out_specs = [pl.BlockSpec((rc, d), chunk)]
        out_shape = [jax.ShapeDtypeStruct((t, d), F32)]
    res = pl.pallas_call(
        kern,
        grid=(nb + 1, nj),
        in_specs=[
            pl.BlockSpec((tm, d), lambda i, j: (jnp.minimum(i, nb - 1), 0)),
            pl.BlockSpec((rc, d), chunk),
            g_spec(layer),
            g_spec(nxt_layer),
            w_in_spec(FF_TILE, lambda j: j),
            w_in_spec(half, lambda j: n_half + 2 * j),
            w_in_spec(half, lambda j: jnp.minimum(n_half + 2 * j + 1, last_up)),
            w_out_spec(lambda j: 2 * j),
            w_out_spec(lambda j: jnp.minimum(2 * j + 1, n_half - 1)),
        ],
        out_specs=out_specs,
        out_shape=out_shape,
        scratch_shapes=[pltpu.VMEM((2, tm, d), F32)],
        compiler_params=_params("arbitrary", "arbitrary"),
        name="ffn",
    )(h, x, norm_g, norm_g, w_in, w_in, w_in, w_out, w_out)
    return (res[0], res[1]) if len(res) == 2 else (res[0], None)


FFN1_PRE, MIX_PRE, MIX_POST, FFN2_PRE = 0, 2, 3, 4


def _proj_kernel(h_ref, w_ref, o_ref):
    o_ref[...] = jnp.dot(h_ref[...], w_ref[...].astype(BF16), preferred_element_type=F32)


def _proj(h, w, widx, tm):
    t, d = h.shape
    n = w.shape[2]
    tn = _divisor_tile(n, 512, LANES)
    return pl.pallas_call(
        _proj_kernel,
        grid=(t // tm, n // tn),
        in_specs=[
            pl.BlockSpec((tm, d), lambda i, j: (i, 0)),
            pl.BlockSpec((None, d, tn), lambda i, j: (widx, 0, j)),
        ],
        out_specs=pl.BlockSpec((tm, tn), lambda i, j: (i, j)),
        out_shape=jax.ShapeDtypeStruct((t, n), F32),
        compiler_params=_params("parallel", "arbitrary"),
        name="proj",
    )(h, w)


def _out_kernel(m_ref, w_ref, xr_ref, g_ref, xo_ref, ho_ref, acc_ref, *, nb, nc):
    i, j = pl.program_id(0), pl.program_id(1)
    ns, tm, tn = acc_ref.shape[1:]
    rc = xr_ref.shape[0]
    cur = lax.rem(i, 2)
    prv = 1 - cur

    @pl.when((i == 0) & (j == 0))
    def _():
        acc_ref[1] = jnp.zeros(acc_ref.shape[1:], F32)

    def tile():
        acc_ref[cur, j] = jnp.dot(m_ref[...], w_ref[...].astype(BF16), preferred_element_type=F32)

    def finish_chunk():
        sub = _divisor_tile(rc, 176, 16)
        for r in range(rc // sub):
            rows = pl.ds(pl.multiple_of(j * rc + r * sub, 16), sub)
            o = jnp.concatenate([acc_ref[prv, s, rows, :] for s in range(ns)], axis=1)
            y = xr_ref[r * sub:(r + 1) * sub, :] + _rms(o, g_ref[MIX_POST:MIX_POST + 1, :])
            xo_ref[r * sub:(r + 1) * sub, :] = y
            ho_ref[r * sub:(r + 1) * sub, :] = _rms(y, g_ref[FFN2_PRE:FFN2_PRE + 1, :]).astype(BF16)

    real = i < nb

    @pl.when(real & (j < nc))
    def _():
        finish_chunk()
        tile()

    @pl.when(real & (j >= nc))
    def _():
        tile()

    @pl.when(jnp.logical_not(real) & (j < nc))
    def _():
        finish_chunk()


def _out(m, w, widx, x, norm_g, layer, tm):
    t, d = x.shape
    kdim = w.shape[1]
    tn = _divisor_tile(d, 512, LANES)
    ns = d // tn
    nb = t // tm
    rc = min(c for c in range(16, tm + 1, 16) if tm % c == 0 and tm // c <= ns)
    nc = tm // rc
    chunk = pl.BlockSpec((rc, d), lambda i, j: (jnp.maximum((i - 1) * nc + jnp.minimum(j, nc - 1), 0), 0))
    return pl.pallas_call(
        functools.partial(_out_kernel, nb=nb, nc=nc),
        grid=(nb + 1, ns),
        in_specs=[
            pl.BlockSpec((tm, kdim), lambda i, j: (jnp.minimum(i, nb - 1), 0)),
            pl.BlockSpec((None, kdim, tn), lambda i, j: (widx, 0, jnp.where(i < nb, j, ns - 1))),
            chunk,
            pl.BlockSpec((None,) + norm_g.shape[1:], lambda i, j: (layer, 0, 0)),
        ],
        out_specs=[chunk, chunk],
        out_shape=[jax.ShapeDtypeStruct((t, d), F32), jax.ShapeDtypeStruct((t, d), BF16)],
        scratch_shapes=[pltpu.VMEM((2, ns, tm, tn), F32)],
        compiler_params=_params("arbitrary", "arbitrary"),
        name="outproj",
    )(m, w, x, norm_g)


def _conv3(ext, cw):
    return cw[0:1] * pltpu.roll(ext, 2, 0) + cw[1:2] * pltpu.roll(ext, 1, 0) + cw[2:3] * ext


def _window_sum(ext, w):
    s, k = ext, 1
    while k < w:
        s = s + pltpu.roll(s, k, 0)
        k *= 2
    return s


def _pool_group(win, cnt, ug, pw, scale):
    dlt = win / cnt - ug
    return jnp.dot(dlt.astype(BF16), pw.astype(BF16), preferred_element_type=F32) * scale


def _mix_prompt_kernel(hc_ref, gc_ref, gb_ref, u_ref, hch_ref, gch_ref, uh_ref, cw_ref, pw_ref,
                       ps_ref, m_ref, cst_ref, pst_ref, *, chunks_per_seq):
    rows, c = hc_ref.shape
    gw = c // len(POOL_WINDOWS)
    ci = pl.program_id(0) % chunks_per_seq
    first = ci == 0

    v = gc_ref[...] * hc_ref[...]
    v_halo = jnp.where(first, 0.0, gch_ref[...] * hch_ref[...])
    y = _conv3(jnp.concatenate([v_halo, v], axis=0), cw_ref[...])[CONV_HALO:]
    m_ref[:, :c] = (gb_ref[...] * y).astype(BF16)
    cst_ref[...] = v[rows - CONV_HALO:]

    u = u_ref[...]
    u_ext = jnp.concatenate([jnp.where(first, 0.0, uh_ref[...]), u], axis=0)
    pst_ref[...] = u_ext[rows:]
    pos1 = ci * rows + lax.broadcasted_iota(jnp.int32, (rows, 1), 0) + 1
    for gi, w in enumerate(POOL_WINDOWS):
        sl = slice(gi * gw, (gi + 1) * gw)
        win = _window_sum(u_ext[:, sl], w)[POOL_HALO:]
        cnt = jnp.minimum(w, pos1).astype(F32)
        yp = _pool_group(win, cnt, u[:, sl], pw_ref[gi], ps_ref[:, sl])
        m_ref[:, c + gi * gw:c + (gi + 1) * gw] = yp.astype(BF16)


def _mix_prompt(z, conv_w, pool_w, pool_scale, n_seq, seq, t_total):
    c = conv_w.shape[1]
    assert z.shape[1] == 4 * c
    rows = _divisor_tile(seq, 256, POOL_HALO)
    cps = seq // rows
    n_chunks = n_seq * cps
    rh_c, rh_p = rows // CONV_HALO, rows // POOL_HALO

    def halo(ratio, col):
        return lambda i: (jnp.maximum(i * ratio - 1, 0), col)

    kern = functools.partial(_mix_prompt_kernel, chunks_per_seq=cps)
    gw = c // len(POOL_WINDOWS)
    return pl.pallas_call(
        kern,
        grid=(n_chunks,),
        in_specs=[
            pl.BlockSpec((rows, c), lambda i: (i, 0)),
            pl.BlockSpec((rows, c), lambda i: (i, 1)),
            pl.BlockSpec((rows, c), lambda i: (i, 2)),
            pl.BlockSpec((rows, c), lambda i: (i, 3)),
            pl.BlockSpec((CONV_HALO, c), halo(rh_c, 0)),
            pl.BlockSpec((CONV_HALO, c), halo(rh_c, 1)),
            pl.BlockSpec((POOL_HALO, c), halo(rh_p, 3)),
            pl.BlockSpec((CONV_K, c), lambda i: (0, 0)),
            pl.BlockSpec((len(POOL_WINDOWS), gw, gw), lambda i: (0, 0, 0)),
            pl.BlockSpec((1, c), lambda i: (0, 0)),
        ],
        out_specs=[
            pl.BlockSpec((rows, 2 * c), lambda i: (i, 0)),
            pl.BlockSpec((None, CONV_HALO, c), lambda i: (i // cps, 0, 0)),
            pl.BlockSpec((None, POOL_HALO, c), lambda i: (i // cps, 0, 0)),
        ],
        out_shape=[
            jax.ShapeDtypeStruct((t_total, 2 * c), BF16),
            jax.ShapeDtypeStruct((n_seq, CONV_HALO, c), F32),
            jax.ShapeDtypeStruct((n_seq, POOL_HALO, c), F32),
        ],
        compiler_params=_params("arbitrary"),
        name="mix_prompt",
    )(z, z, z, z, z, z, z, conv_w, pool_w, pool_scale)


def _mix_sample_kernel(hc_ref, gc_ref, gb_ref, u_ref, cst_in_ref, pst_in_ref, cw_ref, pw_ref,
                       ps_ref, m_in_ref, m_ref, cst_ref, pst_ref, *, seq, past_len):
    del m_in_ref
    rows, c = hc_ref.shape
    n_seq = rows // seq
    gw = c // len(POOL_WINDOWS)

    def with_halo(halo3, x2):
        ext = jnp.concatenate([halo3, x2.reshape(n_seq, seq, c)], axis=1)
        return ext, ext.reshape(n_seq * ext.shape[1], c)

    def body_rows(flat, n_halo):
        return flat.reshape(n_seq, n_halo + seq, flat.shape[-1])[:, n_halo:].reshape(rows, flat.shape[-1])

    v = gc_ref[...] * hc_ref[...]
    v_ext3, v_ext = with_halo(cst_in_ref[...], v)
    y = body_rows(_conv3(v_ext, cw_ref[...]), CONV_HALO)
    m_ref[:, :c] = (gb_ref[...] * y).astype(BF16)
    cst_ref[...] = v_ext3[:, seq:]

    u = u_ref[...]
    u_ext3, u_ext = with_halo(pst_in_ref[...], u)
    pst_ref[...] = u_ext3[:, seq:]
    pos1 = past_len + jnp.bitwise_and(lax.broadcasted_iota(jnp.int32, (rows, 1), 0), seq - 1) + 1
    for gi, w in enumerate(POOL_WINDOWS):
        sl = slice(gi * gw, (gi + 1) * gw)
        win = body_rows(_window_sum(u_ext[:, sl], w), POOL_HALO)
        cnt = jnp.minimum(w, pos1).astype(F32)
        yp = _pool_group(win, cnt, u[:, sl], pw_ref[gi], ps_ref[:, sl])
        m_ref[:, c + gi * gw:c + (gi + 1) * gw] = yp.astype(BF16)


def _mix_sample(z, m, conv_state, pool_state, conv_w, pool_w, pool_scale, t_prompt, seq, past_len):
    c = conv_w.shape[1]
    n_seq = conv_state.shape[0]
    rows = n_seq * seq
    assert seq == SUBLANES and t_prompt % rows == 0
    rb = t_prompt // rows
    gw = c // len(POOL_WINDOWS)
    full3 = lambda i: (0, 0, 0)
    kern = functools.partial(_mix_sample_kernel, seq=seq, past_len=past_len)
    return pl.pallas_call(
        kern,
        grid=(1,),
        in_specs=[
            pl.BlockSpec((rows, c), lambda i: (rb, 0)),
            pl.BlockSpec((rows, c), lambda i: (rb, 1)),
            pl.BlockSpec((rows, c), lambda i: (rb, 2)),
            pl.BlockSpec((rows, c), lambda i: (rb, 3)),
            pl.BlockSpec((n_seq, CONV_HALO, c), full3),
            pl.BlockSpec((n_seq, POOL_HALO, c), full3),
            pl.BlockSpec((CONV_K, c), lambda i: (0, 0)),
            pl.BlockSpec((len(POOL_WINDOWS), gw, gw), full3),
            pl.BlockSpec((1, c), lambda i: (0, 0)),
            pl.BlockSpec(memory_space=pl.ANY),
        ],
        out_specs=[
            pl.BlockSpec((rows, 2 * c), lambda i: (rb, 0)),
            pl.BlockSpec((n_seq, CONV_HALO, c), full3),
            pl.BlockSpec((n_seq, POOL_HALO, c), full3),
        ],
        out_shape=[
            jax.ShapeDtypeStruct(m.shape, m.dtype),
            jax.ShapeDtypeStruct((n_seq, CONV_HALO, c), F32),
            jax.ShapeDtypeStruct((n_seq, POOL_HALO, c), F32),
        ],
        input_output_aliases={9: 0},
        compiler_params=_params("arbitrary"),
        name="mix_sample",
    )(z, z, z, z, conv_state, pool_state, conv_w, pool_w, pool_scale, m)


N_KEYS = 2 * BLOCK


def _t5_bucket(dist):
    max_exact = N_BUCKETS // 2
    n = jnp.maximum(dist, 0)
    ratio = jnp.log(jnp.maximum(n, 1).astype(F32) / max_exact) / math.log(MAX_DISTANCE / max_exact)
    large = jnp.minimum(max_exact + (ratio * (N_BUCKETS - max_exact)).astype(jnp.int32), N_BUCKETS - 1)
    return jnp.where(n < max_exact, n, large)


def _bias_prompt_kernel(rb_ref, o_ref):
    h = pl.program_id(0)
    s = lax.broadcasted_iota(jnp.int32, o_ref.shape, 0)
    q = lax.broadcasted_iota(jnp.int32, o_ref.shape, 1)
    bucket = _t5_bucket(BLOCK + q - s)
    acc = jnp.zeros(o_ref.shape, F32)
    for b in range(N_BUCKETS):
        acc = jnp.where(bucket == b, rb_ref[b, h], acc)
    o_ref[...] = acc


def _bias_sample_kernel(rb_ref, o_ref, *, seq, n_heads):
    half = pl.program_id(0)
    s = lax.broadcasted_iota(jnp.int32, o_ref.shape, 0)
    lane = lax.broadcasted_iota(jnp.int32, o_ref.shape, 1)
    bucket = _t5_bucket(BLOCK + jnp.bitwise_and(lane, seq - 1) - s)
    lane1 = lax.broadcasted_iota(jnp.int32, (1, LANES), 1)
    acc = jnp.zeros(o_ref.shape, F32)
    for b in range(N_BUCKETS):
        vec = jnp.zeros((1, LANES), F32)
        for slab in range(n_heads // 2):
            vec = jnp.where((lane1 >= slab * seq) & (lane1 < (slab + 1) * seq), rb_ref[b, 2 * slab + half], vec)
        acc = jnp.where(bucket == b, vec, acc)
    o_ref[...] = acc


def _bias_tables(rel_bias, seq_s):
    n_heads = rel_bias.shape[1]
    assert seq_s == SUBLANES and (n_heads // 2) * seq_s <= LANES
    prompt = pl.pallas_call(
        _bias_prompt_kernel,
        grid=(n_heads,),
        in_specs=[pl.BlockSpec(memory_space=pltpu.SMEM)],
        out_specs=pl.BlockSpec((None, N_KEYS, BLOCK), lambda h: (h, 0, 0)),
        out_shape=jax.ShapeDtypeStruct((n_heads, N_KEYS, BLOCK), F32),
        compiler_params=_params("arbitrary"),
        name="rel_bias_prompt",
    )(rel_bias)
    sample = pl.pallas_call(
        functools.partial(_bias_sample_kernel, seq=seq_s, n_heads=n_heads),
        grid=(2,),
        in_specs=[pl.BlockSpec(memory_space=pltpu.SMEM)],
        out_specs=pl.BlockSpec((None, N_KEYS, LANES), lambda h: (h, 0, 0)),
        out_shape=jax.ShapeDtypeStruct((2, N_KEYS, LANES), F32),
        compiler_params=_params("arbitrary"),
        name="rel_bias_sample",
    )(rel_bias)
    return prompt, sample


def _stage_kv(k_all, v_all, kpad_ref, vtpad_ref):
    n_keys, kvw = k_all.shape
    lane = lax.broadcasted_iota(jnp.int32, (n_keys, LANES), 1)
    row = lax.broadcasted_iota(jnp.int32, (LANES, n_keys), 0)
    for slab in range(kvw // LANES):
        ks = k_all[:, slab * LANES:(slab + 1) * LANES]
        vt = v_all[:, slab * LANES:(slab + 1) * LANES].T
        for own in range(2):
            kv = 2 * slab + own
            k_own = jnp.where((lane >= own * HEAD_DIM) & (lane < (own + 1) * HEAD_DIM), ks, 0.0)
            v_own = jnp.where((row >= own * HEAD_DIM) & (row < (own + 1) * HEAD_DIM), vt, 0.0)
            kpad_ref[kv, own] = k_own.astype(BF16)
            kpad_ref[kv, 1 - own] = pltpu.roll(k_own, HEAD_DIM, 1).astype(BF16)
            vtpad_ref[kv, own] = v_own.astype(BF16)
            vtpad_ref[kv, 1 - own] = pltpu.roll(v_own, HEAD_DIM, 0).astype(BF16)


_TRANS_B = (((1,), (1,)), ((), ()))


def _attn_prompt_kernel(sinks_ref, q_ref, kc_ref, kp_ref, vc_ref, vp_ref, bias_ref, o_ref,
                        kpad_ref, vtpad_ref, s_ref, e_ref, m_ref, den_ref):
    first = pl.program_id(1) == 0
    n_heads = bias_ref.shape[0]
    group = n_heads // kpad_ref.shape[0]
    _stage_kv(jnp.concatenate([kp_ref[...], kc_ref[...]], axis=0),
              jnp.concatenate([vp_ref[...], vc_ref[...]], axis=0), kpad_ref, vtpad_ref)
    key = lax.broadcasted_iota(jnp.int32, (N_KEYS, BLOCK), 0)
    qry = lax.broadcasted_iota(jnp.int32, (N_KEYS, BLOCK), 1)
    valid = (key >= qry) & (key <= qry + WINDOW) & ((key >= BLOCK) | jnp.logical_not(first))

    for p in range(n_heads // 2):
        q_slab = (q_ref[:, p * LANES:(p + 1) * LANES] * HEAD_DIM ** -0.5).astype(BF16)
        for half in range(2):
            head = 2 * p + half
            s = lax.dot_general(kpad_ref[head // group, half], q_slab, _TRANS_B, preferred_element_type=F32)
            s = jnp.where(valid, s + bias_ref[head], NEG)
            s_ref[head] = s
            m_ref[head:head + 1, :] = jnp.maximum(jnp.max(s, axis=0, keepdims=True), sinks_ref[head])
    for head in range(n_heads):
        m = m_ref[head:head + 1, :]
        e = jnp.exp(s_ref[head] - m)
        den_ref[head:head + 1, :] = jnp.sum(e, axis=0, keepdims=True) + jnp.exp(sinks_ref[head] - m)
        e_ref[head] = e.astype(BF16)
    first_head = lax.broadcasted_iota(jnp.int32, (LANES, BLOCK), 0) < HEAD_DIM
    for p in range(n_heads // 2):
        kv = 2 * p // group
        acc = (jnp.dot(vtpad_ref[kv, 0], e_ref[2 * p], preferred_element_type=F32)
               + jnp.dot(vtpad_ref[kv, 1], e_ref[2 * p + 1], preferred_element_type=F32))
        inv = jnp.where(first_head, 1.0 / den_ref[2 * p:2 * p + 1, :], 1.0 / den_ref[2 * p + 1:2 * p + 2, :])
        o_ref[:, p * LANES:(p + 1) * LANES] = (acc * inv).T.astype(o_ref.dtype)


def _attn_prompt(qkv, bias, sinks, n_seq, seq, n_heads, n_kv, t_total):
    d = n_heads * HEAD_DIM
    kvw = n_kv * HEAD_DIM
    assert kvw % LANES == 0 and d % kvw == 0 and seq % BLOCK == 0
    nb = seq // BLOCK
    kcol, vcol = d // kvw, d // kvw + 1
    cur = lambda col: (lambda n, b: (n * nb + b, col))
    prev = lambda col: (lambda n, b: (n * nb + jnp.maximum(b - 1, 0), col))
    return pl.pallas_call(
        _attn_prompt_kernel,
        grid=(n_seq, nb),
        in_specs=[
            pl.BlockSpec(memory_space=pltpu.SMEM),
            pl.BlockSpec((BLOCK, d), cur(0)),
            pl.BlockSpec((BLOCK, kvw), cur(kcol)),
            pl.BlockSpec((BLOCK, kvw), prev(kcol)),
            pl.BlockSpec((BLOCK, kvw), cur(vcol)),
            pl.BlockSpec((BLOCK, kvw), prev(vcol)),
            pl.BlockSpec((n_heads, N_KEYS, BLOCK), lambda n, b: (0, 0, 0), pipeline_mode=pl.Buffered(1)),
        ],
        out_specs=pl.BlockSpec((BLOCK, d), cur(0)),
        out_shape=jax.ShapeDtypeStruct((t_total, d), BF16),
        scratch_shapes=[
            pltpu.VMEM((n_kv, 2, N_KEYS, LANES), BF16),
            pltpu.VMEM((n_kv, 2, LANES, N_KEYS), BF16),
            pltpu.VMEM((n_heads, N_KEYS, BLOCK), F32),
            pltpu.VMEM((n_heads, N_KEYS, BLOCK), BF16),
            pltpu.VMEM((n_heads, BLOCK), F32),
            pltpu.VMEM((n_heads, BLOCK), F32),
        ],
        compiler_params=_params("parallel", "arbitrary"),
        name="attn_prompt",
    )(sinks, qkv, qkv, qkv, qkv, qkv, bias)


def _attn_sample_kernel(sinks_ref, q_ref, kn_ref, vn_ref, kc_ref, vc_ref, bias_ref, att_in_ref, o_ref,
                        kpad_ref, vtpad_ref):
    del att_in_ref
    seq, kvw = kn_ref.shape
    n_pairs = q_ref.shape[1] // LANES
    n_kv = kvw // HEAD_DIM
    kv_lanes = n_pairs // n_kv * seq
    pad = jnp.zeros((N_KEYS - kc_ref.shape[0] - seq, kvw), F32)
    _stage_kv(jnp.concatenate([kc_ref[...], kn_ref[...], pad], axis=0),
              jnp.concatenate([vc_ref[...], vn_ref[...], pad], axis=0), kpad_ref, vtpad_ref)

    rows = [q_ref[:, p * LANES:(p + 1) * LANES] for p in range(n_pairs)]
    if n_pairs * seq < LANES:
        rows.append(jnp.zeros((LANES - n_pairs * seq, LANES), F32))
    qs = (jnp.concatenate(rows, axis=0) * HEAD_DIM ** -0.5).astype(BF16)

    key = lax.broadcasted_iota(jnp.int32, (N_KEYS, LANES), 0)
    lane = lax.broadcasted_iota(jnp.int32, (N_KEYS, LANES), 1)
    lane1 = lax.broadcasted_iota(jnp.int32, (1, LANES), 1)
    qry = jnp.bitwise_and(lane, seq - 1)
    valid = (key >= qry) & (key <= qry + WINDOW)

    probs, dens = [], []
    for half in range(2):
        s = jnp.zeros((N_KEYS, LANES), F32)
        for kv in range(n_kv):
            s_kv = lax.dot_general(kpad_ref[kv, half], qs, _TRANS_B, preferred_element_type=F32)
            s = jnp.where((lane >= kv * kv_lanes) & (lane < (kv + 1) * kv_lanes), s_kv, s)
        sink = jnp.zeros((1, LANES), F32)
        for p in range(n_pairs):
            sink = jnp.where((lane1 >= p * seq) & (lane1 < (p + 1) * seq), sinks_ref[2 * p + half], sink)
        s = jnp.where(valid, s + bias_ref[half], NEG)
        m = jnp.maximum(jnp.max(s, axis=0, keepdims=True), sink)
        e = jnp.exp(s - m)
        dens.append(jnp.sum(e, axis=0, keepdims=True) + jnp.exp(sink - m))
        probs.append(e.astype(BF16))

    row_t = lax.broadcasted_iota(jnp.int32, (LANES, LANES), 0)
    lane_t = lax.broadcasted_iota(jnp.int32, (LANES, LANES), 1)
    out_t = jnp.zeros((LANES, LANES), F32)
    for kv in range(n_kv):
        acc = (jnp.dot(vtpad_ref[kv, 0], probs[0], preferred_element_type=F32)
               + jnp.dot(vtpad_ref[kv, 1], probs[1], preferred_element_type=F32))
        out_t = jnp.where((lane_t >= kv * kv_lanes) & (lane_t < (kv + 1) * kv_lanes), acc, out_t)
    inv = jnp.where(row_t < HEAD_DIM, 1.0 / dens[0], 1.0 / dens[1])
    out = (out_t * inv).T
    for p in range(n_pairs):
        o_ref[:, p * LANES:(p + 1) * LANES] = out[p * seq:(p + 1) * seq].astype(o_ref.dtype)


def _attn_sample(qkv, att, cache_k, cache_v, bias, sinks, t_prompt, seq, n_heads):
    n_seq, kv_buf, kvw = cache_k.shape
    d = n_heads * HEAD_DIM
    assert kv_buf == BLOCK and seq == SUBLANES and t_prompt % seq == 0
    rb = t_prompt // seq
    kcol, vcol = d // kvw, d // kvw + 1
    return pl.pallas_call(
        _attn_sample_kernel,
        grid=(n_seq,),
        in_specs=[
            pl.BlockSpec(memory_space=pltpu.SMEM),
            pl.BlockSpec((seq, d), lambda n: (rb + n, 0)),
            pl.BlockSpec((seq, kvw), lambda n: (rb + n, kcol)),
            pl.BlockSpec((seq, kvw), lambda n: (rb + n, vcol)),
            pl.BlockSpec((None, kv_buf, kvw), lambda n: (n, 0, 0)),
            pl.BlockSpec((None, kv_buf, kvw), lambda n: (n, 0, 0)),
            pl.BlockSpec((2, N_KEYS, LANES), lambda n: (0, 0, 0)),
            pl.BlockSpec(memory_space=pl.ANY),
        ],
        out_specs=pl.BlockSpec((seq, d), lambda n: (rb + n, 0)),
        out_shape=jax.ShapeDtypeStruct(att.shape, att.dtype),
        scratch_shapes=[
            pltpu.VMEM((kvw // HEAD_DIM, 2, N_KEYS, LANES), BF16),
            pltpu.VMEM((kvw // HEAD_DIM, 2, LANES, N_KEYS), BF16),
        ],
        input_output_aliases={7: 0},
        compiler_params=_params("arbitrary"),
        name="attn_sample",
    )(sinks, qkv, qkv, qkv, cache_k, cache_v, bias, att)


def kernel(x_prompt, x_sample, state_conv, state_pool, cache_k, cache_v, norm_g, w_ffn_in,
           w_ffn_out, w_mix_in, conv_w, pool_w, pool_scale, w_mix_out, w_qkv, w_o,
           attn_sinks, rel_bias):
    n_p, seq_p, d = x_prompt.shape
    n_s, seq_s, _ = x_sample.shape
    depth = norm_g.shape[0]
    n_heads = d // HEAD_DIM
    n_kv = cache_k.shape[3]
    kv_buf = cache_k.shape[2]
    kvw = n_kv * HEAD_DIM
    conv_ctx = state_conv.shape[2]
    pool_ctx = state_pool.shape[2]
    t_p, t_s = n_p * seq_p, n_s * seq_s
    t = t_p + t_s
    tm = _divisor_tile(t, ROW_TILE, 16)
    tm_proj = _divisor_tile(t, PROJ_ROW_TILE, 16)

    tm_ffn = _divisor_tile(t, FFN_ROW_TILE, 16)
    x, h = _merge(x_prompt.reshape(t_p, d), x_sample.reshape(t_s, d), norm_g, 0, FFN1_PRE)
    bias_p, bias_s = _bias_tables(rel_bias, seq_s)

    conv_p, pool_p, k_p, v_p, conv_s, pool_s, k_s, v_s = ([] for _ in range(8))
    for i in range(depth):
        x, h = _ffn(h, x, norm_g, w_ffn_in, w_ffn_out, i, 0, tm_ffn, nxt=(i, MIX_PRE))
        j = i // 2
        if i % 2 == 0:
            z = _proj(h, w_mix_in, j, tm_proj)
            m, cst, pst = _mix_prompt(z, conv_w[j], pool_w[j], pool_scale[j][None], n_p, seq_p, t)
            conv_p.append(cst[:, CONV_HALO - conv_ctx:])
            pool_p.append(pst[:, POOL_HALO - pool_ctx:])
            cst_in = jnp.pad(state_conv[j], ((0, 0), (CONV_HALO - conv_ctx, 0), (0, 0)))
            pst_in = jnp.pad(state_pool[j], ((0, 0), (POOL_HALO - pool_ctx, 0), (0, 0)))
            m, cst, pst = _mix_sample(z, m, cst_in, pst_in, conv_w[j], pool_w[j], pool_scale[j][None],
                                      t_p, seq_s, PAST_LEN)
            conv_s.append(cst[:, CONV_HALO - conv_ctx:])
            pool_s.append(pst[:, POOL_HALO - pool_ctx:])
            x, h = _out(m, w_mix_out, j, x, norm_g, i, tm)
        else:
            qkv = _proj(h, w_qkv, j, tm_proj)
            att = _attn_prompt(qkv, bias_p, attn_sinks[j], n_p, seq_p, n_heads, n_kv, t)
            ck = cache_k[j].reshape(n_s, kv_buf, kvw)
            cv = cache_v[j].reshape(n_s, kv_buf, kvw)
            att = _attn_sample(qkv, att, ck, cv, bias_s, attn_sinks[j], t_p, seq_s, n_heads)
            k_new, v_new = qkv[:, d:d + kvw], qkv[:, d + kvw:]
            kv_p = lambda a: a[:t_p].reshape(n_p, seq_p, n_kv, HEAD_DIM)[:, seq_p - kv_buf:]
            k_p.append(kv_p(k_new))
            v_p.append(kv_p(v_new))
            kv_s = lambda c, a: jnp.concatenate(
                [c, a[t_p:].reshape(n_s, seq_s, kvw)], axis=1)[:, seq_s:].reshape(n_s, kv_buf, n_kv, HEAD_DIM)
            k_s.append(kv_s(ck, k_new))
            v_s.append(kv_s(cv, v_new))
            x, h = _out(att, w_o, j, x, norm_g, i, tm)
        if i + 1 < depth:
            x, h = _ffn(h, x, norm_g, w_ffn_in, w_ffn_out, i, 1, tm_ffn, nxt=(i + 1, FFN1_PRE))
        else:
            y_p, y_s = _ffn(h, x, norm_g, w_ffn_in, w_ffn_out, i, 1, tm_ffn, head_rows=t_p)

    y_p = y_p.reshape(n_p, seq_p, d)
    y_s = y_s.reshape(n_s, seq_s, d)
    st = jnp.stack
    return (y_p, y_s, st(conv_p), st(pool_p), st(k_p), st(v_p), st(conv_s), st(pool_s), st(k_s), st(v_s))
```

```python
import functools
import math

import jax
import jax.numpy as jnp
from jax import lax
from jax.experimental import pallas as pl
from jax.experimental.pallas import tpu as pltpu

F32 = jnp.float32
BF16 = jnp.bfloat16

EPS = 1e-6
NEG = -1e30
HEAD_DIM = 64
WINDOW = 128
BLOCK = WINDOW
CONV_K = 3
POOL_WINDOWS = (2, 4, 8, 16)
N_BUCKETS = 32
MAX_DISTANCE = 128
PAST_LEN = 16384

LANES = 128
SUBLANES = 8
CONV_HALO = SUBLANES
POOL_HALO = 2 * SUBLANES
FF_TILE = 2 * LANES
VMEM_LIMIT_BYTES = 58 * 1024 * 1024
ROW_TILE = 1056
FFN_ROW_TILE = 1408
PROJ_ROW_TILE = 2112


def _divisor_tile(n, target, mult):
    best = None
    for t in range(mult, min(n, target) + 1, mult):
        if n % t == 0:
            best = t
    assert best is not None, (n, target, mult)
    return best


def _params(*sem):
    return pltpu.CompilerParams(dimension_semantics=sem, vmem_limit_bytes=VMEM_LIMIT_BYTES)


def _rms(xf, g):
    ms = jnp.mean(xf * xf, axis=-1, keepdims=True)
    return xf * lax.rsqrt(ms + EPS) * g


def _for_row_chunks(n_rows, body):
    rc = _divisor_tile(n_rows, 272, 16)

    def step(c, carry):
        body(pl.ds(pl.multiple_of(c * rc, rc), rc))
        return carry

    lax.fori_loop(0, n_rows // rc, step, 0)


def _merge_kernel(xp_ref, xs_ref, g_ref, x_ref, h_ref, *, n_prompt_blocks, row):
    x = jnp.where(pl.program_id(0) < n_prompt_blocks, xp_ref[...], xs_ref[...])
    x_ref[...] = x
    h_ref[...] = _rms(x, g_ref[row:row + 1, :]).astype(BF16)


def _merge(x_prompt, x_sample, norm_g, layer, row):
    (t_p, d), t_s = x_prompt.shape, x_sample.shape[0]
    assert t_p % t_s == 0 and t_s % 16 == 0
    nbp = t_p // t_s
    row_block = pl.BlockSpec((t_s, d), lambda i: (i, 0))
    return pl.pallas_call(
        functools.partial(_merge_kernel, n_prompt_blocks=nbp, row=row),
        grid=(nbp + 1,),
        in_specs=[
            pl.BlockSpec((t_s, d), lambda i: (jnp.minimum(i, nbp - 1), 0)),
            pl.BlockSpec((t_s, d), lambda i: (0, 0)),
            pl.BlockSpec((None,) + norm_g.shape[1:], lambda i: (layer, 0, 0)),
        ],
        out_specs=[row_block, row_block],
        out_shape=[jax.ShapeDtypeStruct((t_p + t_s, d), F32), jax.ShapeDtypeStruct((t_p + t_s, d), BF16)],
        compiler_params=_params("parallel"),
        name="merge_rows",
    )(x_prompt, x_sample, norm_g)


def _ffn_kernel(h_ref, xr_ref, g_ref, gn_ref, wg_ref, wul_ref, wuh_ref, wol_ref, woh_ref,
                xo_ref, *rest, d_ff, nj, nb, nc, post, nxt, n_head_chunks):
    assert nxt is None or n_head_chunks is None
    o2_ref, acc_ref = rest if len(rest) == 2 else (None,) + rest
    i, j = pl.program_id(0), pl.program_id(1)
    tm, d = h_ref.shape
    rc = xr_ref.shape[0]
    cur = lax.rem(i, 2)
    prv = 1 - cur

    @pl.when((i == 0) & (j == 0))
    def _():
        def zero(rows):
            acc_ref[0, rows, :] = jnp.zeros((rows.size, d), F32)
            acc_ref[1, rows, :] = jnp.zeros((rows.size, d), F32)
        _for_row_chunks(tm, zero)

    def tile():
        w = jnp.concatenate([wg_ref[...].astype(BF16), wul_ref[...].astype(BF16),
                             wuh_ref[...].astype(BF16)], axis=1)
        r = jnp.dot(h_ref[...], w, preferred_element_type=F32)
        gate, up = r[:, :FF_TILE], r[:, FF_TILE:]
        a = (gate * jax.nn.sigmoid(gate)) * up
        if d_ff % FF_TILE:
            col = lax.broadcasted_iota(jnp.int32, a.shape, 1)
            a = jnp.where(col < d_ff - j * FF_TILE, a, 0.0)
        a = a.astype(BF16)
        wo = jnp.concatenate([wol_ref[...].astype(BF16), woh_ref[...].astype(BF16)], axis=0)
        wc = _divisor_tile(d, 512, LANES)
        for c in range(d // wc):
            acc_ref[cur, :, c * wc:(c + 1) * wc] += jnp.dot(a, wo[:, c * wc:(c + 1) * wc],
                                                            preferred_element_type=F32)

    def finish_chunk():
        rows = pl.ds(pl.multiple_of(j * rc, rc), rc)
        y = xr_ref[...] + 0.5 * _rms(acc_ref[prv, rows, :], g_ref[post:post + 1, :])
        acc_ref[prv, rows, :] = jnp.zeros((rc, d), F32)
        if nxt is not None:
            xo_ref[...] = y
            o2_ref[...] = _rms(y, gn_ref[nxt:nxt + 1, :]).astype(BF16)
        elif n_head_chunks is not None:
            in_head = (i - 1) * nc + j < n_head_chunks

            @pl.when(in_head)
            def _():
                xo_ref[...] = y

            @pl.when(jnp.logical_not(in_head))
            def _():
                o2_ref[...] = y
        else:
            xo_ref[...] = y

    real = i < nb

    @pl.when(real & (j < nc))
    def _():
        finish_chunk()
        tile()

    @pl.when(real & (j >= nc))
    def _():
        tile()

    @pl.when(jnp.logical_not(real) & (j < nc))
    def _():
        finish_chunk()


def _ffn(h, x, norm_g, w_in, w_out, layer, which, tm, nxt=None, head_rows=None):
    t, d = x.shape
    d_ff = w_out.shape[2]
    half = FF_TILE // 2
    assert d_ff % half == 0 and w_in.shape[2:] == (d, 2 * d_ff) and t % tm == 0
    n_half = d_ff // half
    nj = pl.cdiv(d_ff, FF_TILE)
    nb = t // tm
    last_up = 2 * n_half - 1
    rc = min(c for c in range(16, tm + 1, 16) if tm % c == 0 and tm // c <= nj)
    nc = tm // rc
    nxt_layer, nxt_row = nxt if nxt is not None else (layer, None)
    n_head = None
    if head_rows is not None:
        assert nxt is None and head_rows % rc == 0 and (t - head_rows) % rc == 0
        n_head = head_rows // rc
    kern = functools.partial(_ffn_kernel, d_ff=d_ff, nj=nj, nb=nb, nc=nc, post=4 * which + 1, nxt=nxt_row,
                             n_head_chunks=n_head)
    jw = lambda i, j: jnp.where(i < nb, j, nj - 1)
    w_in_spec = lambda width, col: pl.BlockSpec(
        (None, None, d, width), lambda i, j: (layer, which, 0, col(jw(i, j))))
    w_out_spec = lambda row: pl.BlockSpec(
        (None, None, half, d), lambda i, j: (layer, which, row(jw(i, j)), 0))
    chunk = lambda i, j: (jnp.maximum((i - 1) * nc + jnp.minimum(j, nc - 1), 0), 0)
    g_spec = lambda l: pl.BlockSpec((None,) + norm_g.shape[1:], lambda i, j: (l, 0, 0))
    if nxt is not None:
        out_specs = [pl.BlockSpec((rc, d), chunk), pl.BlockSpec((rc, d), chunk)]
        out_shape = [jax.ShapeDtypeStruct((t, d), F32), jax.ShapeDtypeStruct((t, d), BF16)]
    elif n_head is not None:
        out_specs = [pl.BlockSpec((rc, d), lambda i, j: (jnp.minimum(chunk(i, j)[0], n_head - 1), 0)),
                     pl.BlockSpec((rc, d), lambda i, j: (jnp.maximum(chunk(i, j)[0] - n_head, 0), 0))]
        out_shape = [jax.ShapeDtypeStruct((head_rows, d), F32), jax.ShapeDtypeStruct((t - head_rows, d), F32)]
    else:
        out_specs = [pl.BlockSpec((rc, d), chunk)]
        out_shape = [jax.ShapeDtypeStruct((t, d), F32)]
    res = pl.pallas_call(
        kern,
        grid=(nb + 1, nj),
        in_specs=[
            pl.BlockSpec((tm, d), lambda i, j: (jnp.minimum(i, nb - 1), 0)),
            pl.BlockSpec((rc, d), chunk),
            g_spec(layer),
            g_spec(nxt_layer),
            w_in_spec(FF_TILE, lambda j: j),
            w_in_spec(half, lambda j: n_half + 2 * j),
            w_in_spec(half, lambda j: jnp.minimum(n_half + 2 * j + 1, last_up)),
            w_out_spec(lambda j: 2 * j),
            w_out_spec(lambda j: jnp.minimum(2 * j + 1, n_half - 1)),
        ],
        out_specs=out_specs,
        out_shape=out_shape,
        scratch_shapes=[pltpu.VMEM((2, tm, d), F32)],
        compiler_params=_params("arbitrary", "arbitrary"),
        name="ffn",
    )(h, x, norm_g, norm_g, w_in, w_in, w_in, w_out, w_out)
    return (res[0], res[1]) if len(res) == 2 else (res[0], None)


FFN1_PRE, MIX_PRE, MIX_POST, FFN2_PRE = 0, 2, 3, 4


def _proj_kernel(h_ref, w_ref, o_ref):
    o_ref[...] = jnp.dot(h_ref[...], w_ref[...].astype(BF16), preferred_element_type=F32)


def _proj(h, w, widx, tm):
    t, d = h.shape
    n = w.shape[2]
    tn = _divisor_tile(n, 512, LANES)
    return pl.pallas_call(
        _proj_kernel,
        grid=(t // tm, n // tn),
        in_specs=[
            pl.BlockSpec((tm, d), lambda i, j: (i, 0)),
            pl.BlockSpec((None, d, tn), lambda i, j: (widx, 0, j)),
        ],
        out_specs=pl.BlockSpec((tm, tn), lambda i, j: (i, j)),
        out_shape=jax.ShapeDtypeStruct((t, n), F32),
        compiler_params=_params("parallel", "arbitrary"),
        name="proj",
    )(h, w)


def _column_tiles_kernel(w_ref, o_ref):
    o_ref[...] = w_ref[...].astype(BF16)


def _column_tiles_bf16(w, tn):
    n, kdim, d = w.shape
    return pl.pallas_call(
        _column_tiles_kernel,
        grid=(n, d // tn),
        in_specs=[pl.BlockSpec((None, kdim, tn), lambda a, j: (a, 0, j))],
        out_specs=pl.BlockSpec((None, None, kdim, tn), lambda a, j: (a, j, 0, 0)),
        out_shape=jax.ShapeDtypeStruct((n, d // tn, kdim, tn), BF16),
        compiler_params=_params("parallel", "parallel"),
        name="column_tiles_bf16",
    )(w)


def _out_kernel(m_ref, w_ref, xr_ref, g_ref, xo_ref, ho_ref, acc_ref, *, nb, nc):
    i, j = pl.program_id(0), pl.program_id(1)
    ns, tm, tn = acc_ref.shape[1:]
    rc = xr_ref.shape[0]
    cur = lax.rem(i, 2)
    prv = 1 - cur

    @pl.when((i == 0) & (j == 0))
    def _():
        acc_ref[1] = jnp.zeros(acc_ref.shape[1:], F32)

    def tile():
        acc_ref[cur, j] = jnp.dot(m_ref[...], w_ref[j], preferred_element_type=F32)

    def finish_chunk():
        sub = _divisor_tile(rc, 176, 16)
        for r in range(rc // sub):
            rows = pl.ds(pl.multiple_of(j * rc + r * sub, 16), sub)
            o = jnp.concatenate([acc_ref[prv, s, rows, :] for s in range(ns)], axis=1)
            y = xr_ref[r * sub:(r + 1) * sub, :] + _rms(o, g_ref[MIX_POST:MIX_POST + 1, :])
            xo_ref[r * sub:(r + 1) * sub, :] = y
            ho_ref[r * sub:(r + 1) * sub, :] = _rms(y, g_ref[FFN2_PRE:FFN2_PRE + 1, :]).astype(BF16)

    real = i < nb

    @pl.when(real & (j < nc))
    def _():
        finish_chunk()
        tile()

    @pl.when(real & (j >= nc))
    def _():
        tile()

    @pl.when(jnp.logical_not(real) & (j < nc))
    def _():
        finish_chunk()


def _out(m, w, widx, x, norm_g, layer, tm):
    t, d = x.shape
    ns, kdim, tn = w.shape[1:]
    assert ns * tn == d
    nb = t // tm
    rc = min(c for c in range(16, tm + 1, 16) if tm % c == 0 and tm // c <= ns)
    nc = tm // rc
    chunk = pl.BlockSpec((rc, d), lambda i, j: (jnp.maximum((i - 1) * nc + jnp.minimum(j, nc - 1), 0), 0))
    return pl.pallas_call(
        functools.partial(_out_kernel, nb=nb, nc=nc),
        grid=(nb + 1, ns),
        in_specs=[
            pl.BlockSpec((tm, kdim), lambda i, j: (jnp.minimum(i, nb - 1), 0)),
            pl.BlockSpec((None, ns, kdim, tn), lambda i, j: (widx, 0, 0, 0), pipeline_mode=pl.Buffered(1)),
            chunk,
            pl.BlockSpec((None,) + norm_g.shape[1:], lambda i, j: (layer, 0, 0)),
        ],
        out_specs=[chunk, chunk],
        out_shape=[jax.ShapeDtypeStruct((t, d), F32), jax.ShapeDtypeStruct((t, d), BF16)],
        scratch_shapes=[pltpu.VMEM((2, ns, tm, tn), F32)],
        compiler_params=_params("arbitrary", "arbitrary"),
        name="outproj",
    )(m, w, x, norm_g)


def _conv3(ext, cw):
    return cw[0:1] * pltpu.roll(ext, 2, 0) + cw[1:2] * pltpu.roll(ext, 1, 0) + cw[2:3] * ext


def _window_sum(ext, w):
    s, k = ext, 1
    while k < w:
        s = s + pltpu.roll(s, k, 0)
        k *= 2
    return s


def _pool_group(win, cnt, ug, pw, scale):
    dlt = win / cnt - ug
    return jnp.dot(dlt.astype(BF16), pw.astype(BF16), preferred_element_type=F32) * scale


def _mix_prompt_kernel(hc_ref, gc_ref, gb_ref, u_ref, hch_ref, gch_ref, uh_ref, cw_ref, pw_ref,
                       ps_ref, m_ref, cst_ref, pst_ref, *, chunks_per_seq, n_chunks):
    rows, c = hc_ref.shape
    gw = c // len(POOL_WINDOWS)
    ci = pl.program_id(0) % chunks_per_seq
    first = ci == 0

    @pl.when(pl.program_id(0) >= n_chunks)
    def _():
        m_ref[...] = jnp.zeros(m_ref.shape, BF16)

    @pl.when(pl.program_id(0) < n_chunks)
    def _():
        v = gc_ref[...] * hc_ref[...]
        v_halo = jnp.where(first, 0.0, gch_ref[...] * hch_ref[...])
        y = _conv3(jnp.concatenate([v_halo, v], axis=0), cw_ref[...])[CONV_HALO:]
        m_ref[:, :c] = (gb_ref[...] * y).astype(BF16)
        cst_ref[...] = v[rows - CONV_HALO:]

        u = u_ref[...]
        u_ext = jnp.concatenate([jnp.where(first, 0.0, uh_ref[...]), u], axis=0)
        pst_ref[...] = u_ext[rows:]
        pos1 = ci * rows + lax.broadcasted_iota(jnp.int32, (rows, 1), 0) + 1
        for gi, w in enumerate(POOL_WINDOWS):
            sl = slice(gi * gw, (gi + 1) * gw)
            win = _window_sum(u_ext[:, sl], w)[POOL_HALO:]
            cnt = jnp.minimum(w, pos1).astype(F32)
            yp = _pool_group(win, cnt, u[:, sl], pw_ref[gi], ps_ref[:, sl])
            m_ref[:, c + gi * gw:c + (gi + 1) * gw] = yp.astype(BF16)


def _mix_prompt(z, conv_w, pool_w, pool_scale, n_seq, seq, t_total):
    c = conv_w.shape[1]
    assert z.shape[1] == 4 * c
    t_tail = t_total - n_seq * seq
    rows = max(r for r in range(POOL_HALO, 257, POOL_HALO) if seq % r == 0 and t_tail % r == 0)
    cps = seq // rows
    n_chunks = n_seq * cps
    rh_c, rh_p = rows // CONV_HALO, rows // POOL_HALO

    def halo(ratio, col):
        return lambda i: (jnp.maximum(i * ratio - 1, 0), col)

    kern = functools.partial(_mix_prompt_kernel, chunks_per_seq=cps, n_chunks=n_chunks)
    gw = c // len(POOL_WINDOWS)
    state = lambda i: (jnp.minimum(i // cps, n_seq - 1), 0, 0)
    return pl.pallas_call(
        kern,
        grid=(n_chunks + t_tail // rows,),
        in_specs=[
            pl.BlockSpec((rows, c), lambda i: (i, 0)),
            pl.BlockSpec((rows, c), lambda i: (i, 1)),
            pl.BlockSpec((rows, c), lambda i: (i, 2)),
            pl.BlockSpec((rows, c), lambda i: (i, 3)),
            pl.BlockSpec((CONV_HALO, c), halo(rh_c, 0)),
            pl.BlockSpec((CONV_HALO, c), halo(rh_c, 1)),
            pl.BlockSpec((POOL_HALO, c), halo(rh_p, 3)),
            pl.BlockSpec((CONV_K, c), lambda i: (0, 0)),
            pl.BlockSpec((len(POOL_WINDOWS), gw, gw), lambda i: (0, 0, 0)),
            pl.BlockSpec((1, c), lambda i: (0, 0)),
        ],
        out_specs=[
            pl.BlockSpec((rows, 2 * c), lambda i: (i, 0)),
            pl.BlockSpec((None, CONV_HALO, c), state),
            pl.BlockSpec((None, POOL_HALO, c), state),
        ],
        out_shape=[
            jax.ShapeDtypeStruct((t_total, 2 * c), BF16),
            jax.ShapeDtypeStruct((n_seq, CONV_HALO, c), F32),
            jax.ShapeDtypeStruct((n_seq, POOL_HALO, c), F32),
        ],
        compiler_params=_params("arbitrary"),
        name="mix_prompt",
    )(z, z, z, z, z, z, z, conv_w, pool_w, pool_scale)


def _mix_sample_kernel(hc_ref, gc_ref, gb_ref, u_ref, cst_in_ref, pst_in_ref, cw_ref, pw_ref,
                       ps_ref, m_in_ref, m_ref, cst_ref, pst_ref, *, seq, past_len):
    del m_in_ref
    rows, c = hc_ref.shape
    n_seq = rows // seq
    gw = c // len(POOL_WINDOWS)

    def with_halo(halo3, x2):
        ext = jnp.concatenate([halo3, x2.reshape(n_seq, seq, c)], axis=1)
        return ext, ext.reshape(n_seq * ext.shape[1], c)

    def body_rows(flat, n_halo):
        return flat.reshape(n_seq, n_halo + seq, flat.shape[-1])[:, n_halo:].reshape(rows, flat.shape[-1])

    v = gc_ref[...] * hc_ref[...]
    v_ext3, v_ext = with_halo(cst_in_ref[...], v)
    y = body_rows(_conv3(v_ext, cw_ref[...]), CONV_HALO)
    m_ref[:, :c] = (gb_ref[...] * y).astype(BF16)
    cst_ref[...] = v_ext3[:, seq:]

    u = u_ref[...]
    u_ext3, u_ext = with_halo(pst_in_ref[...], u)
    pst_ref[...] = u_ext3[:, seq:]
    pos1 = past_len + jnp.bitwise_and(lax.broadcasted_iota(jnp.int32, (rows, 1), 0), seq - 1) + 1
    for gi, w in enumerate(POOL_WINDOWS):
        sl = slice(gi * gw, (gi + 1) * gw)
        win = body_rows(_window_sum(u_ext[:, sl], w), POOL_HALO)
        cnt = jnp.minimum(w, pos1).astype(F32)
        yp = _pool_group(win, cnt, u[:, sl], pw_ref[gi], ps_ref[:, sl])
        m_ref[:, c + gi * gw:c + (gi + 1) * gw] = yp.astype(BF16)


def _mix_sample(z, m, conv_state, pool_state, conv_w, pool_w, pool_scale, t_prompt, seq, past_len):
    c = conv_w.shape[1]
    n_seq = conv_state.shape[0]
    rows = n_seq * seq
    assert seq == SUBLANES and t_prompt % rows == 0
    rb = t_prompt // rows
    gw = c // len(POOL_WINDOWS)
    full3 = lambda i: (0, 0, 0)
    kern = functools.partial(_mix_sample_kernel, seq=seq, past_len=past_len)
    return pl.pallas_call(
        kern,
        grid=(1,),
        in_specs=[
            pl.BlockSpec((rows, c), lambda i: (rb, 0)),
            pl.BlockSpec((rows, c), lambda i: (rb, 1)),
            pl.BlockSpec((rows, c), lambda i: (rb, 2)),
            pl.BlockSpec((rows, c), lambda i: (rb, 3)),
            pl.BlockSpec((n_seq, CONV_HALO, c), full3),
            pl.BlockSpec((n_seq, POOL_HALO, c), full3),
            pl.BlockSpec((CONV_K, c), lambda i: (0, 0)),
            pl.BlockSpec((len(POOL_WINDOWS), gw, gw), full3),
            pl.BlockSpec((1, c), lambda i: (0, 0)),
            pl.BlockSpec(memory_space=pl.ANY),
        ],
        out_specs=[
            pl.BlockSpec((rows, 2 * c), lambda i: (rb, 0)),
            pl.BlockSpec((n_seq, CONV_HALO, c), full3),
            pl.BlockSpec((n_seq, POOL_HALO, c), full3),
        ],
        out_shape=[
            jax.ShapeDtypeStruct(m.shape, m.dtype),
            jax.ShapeDtypeStruct((n_seq, CONV_HALO, c), F32),
            jax.ShapeDtypeStruct((n_seq, POOL_HALO, c), F32),
        ],
        input_output_aliases={9: 0},
        compiler_params=_params("arbitrary"),
        name="mix_sample",
    )(z, z, z, z, conv_state, pool_state, conv_w, pool_w, pool_scale, m)


N_KEYS = 2 * BLOCK


def _t5_bucket(dist):
    max_exact = N_BUCKETS // 2
    n = jnp.maximum(dist, 0)
    ratio = jnp.log(jnp.maximum(n, 1).astype(F32) / max_exact) / math.log(MAX_DISTANCE / max_exact)
    large = jnp.minimum(max_exact + (ratio * (N_BUCKETS - max_exact)).astype(jnp.int32), N_BUCKETS - 1)
    return jnp.where(n < max_exact, n, large)


def _bias_prompt_kernel(rb_ref, o_ref):
    h = pl.program_id(0)
    s = lax.broadcasted_iota(jnp.int32, o_ref.shape, 0)
    q = lax.broadcasted_iota(jnp.int32, o_ref.shape, 1)
    bucket = _t5_bucket(BLOCK + q - s)
    acc = jnp.zeros(o_ref.shape, F32)
    for b in range(N_BUCKETS):
        acc = jnp.where(bucket == b, rb_ref[b, h], acc)
    o_ref[...] = acc


def _bias_sample_kernel(rb_ref, o_ref, *, seq, n_heads):
    half = pl.program_id(0)
    s = lax.broadcasted_iota(jnp.int32, o_ref.shape, 0)
    lane = lax.broadcasted_iota(jnp.int32, o_ref.shape, 1)
    bucket = _t5_bucket(BLOCK + jnp.bitwise_and(lane, seq - 1) - s)
    lane1 = lax.broadcasted_iota(jnp.int32, (1, LANES), 1)
    acc = jnp.zeros(o_ref.shape, F32)
    for b in range(N_BUCKETS):
        vec = jnp.zeros((1, LANES), F32)
        for slab in range(n_heads // 2):
            vec = jnp.where((lane1 >= slab * seq) & (lane1 < (slab + 1) * seq), rb_ref[b, 2 * slab + half], vec)
        acc = jnp.where(bucket == b, vec, acc)
    o_ref[...] = acc


def _bias_tables(rel_bias, seq_s):
    n_heads = rel_bias.shape[1]
    assert seq_s == SUBLANES and (n_heads // 2) * seq_s <= LANES
    prompt = pl.pallas_call(
        _bias_prompt_kernel,
        grid=(n_heads,),
        in_specs=[pl.BlockSpec(memory_space=pltpu.SMEM)],
        out_specs=pl.BlockSpec((None, N_KEYS, BLOCK), lambda h: (h, 0, 0)),
        out_shape=jax.ShapeDtypeStruct((n_heads, N_KEYS, BLOCK), F32),
        compiler_params=_params("arbitrary"),
        name="rel_bias_prompt",
    )(rel_bias)
    sample = pl.pallas_call(
        functools.partial(_bias_sample_kernel, seq=seq_s, n_heads=n_heads),
        grid=(2,),
        in_specs=[pl.BlockSpec(memory_space=pltpu.SMEM)],
        out_specs=pl.BlockSpec((None, N_KEYS, LANES), lambda h: (h, 0, 0)),
        out_shape=jax.ShapeDtypeStruct((2, N_KEYS, LANES), F32),
        compiler_params=_params("arbitrary"),
        name="rel_bias_sample",
    )(rel_bias)
    return prompt, sample


def _stage_kv(k_all, v_all, kpad_ref, vtpad_ref):
    n_keys, kvw = k_all.shape
    lane = lax.broadcasted_iota(jnp.int32, (n_keys, LANES), 1)
    row = lax.broadcasted_iota(jnp.int32, (LANES, n_keys), 0)
    for slab in range(kvw // LANES):
        ks = k_all[:, slab * LANES:(slab + 1) * LANES]
        vt = v_all[:, slab * LANES:(slab + 1) * LANES].T
        for own in range(2):
            kv = 2 * slab + own
            k_own = jnp.where((lane >= own * HEAD_DIM) & (lane < (own + 1) * HEAD_DIM), ks, 0.0)
            v_own = jnp.where((row >= own * HEAD_DIM) & (row < (own + 1) * HEAD_DIM), vt, 0.0)
            kpad_ref[kv, own] = k_own.astype(BF16)
            kpad_ref[kv, 1 - own] = pltpu.roll(k_own, HEAD_DIM, 1).astype(BF16)
            vtpad_ref[kv, own] = v_own.astype(BF16)
            vtpad_ref[kv, 1 - own] = pltpu.roll(v_own, HEAD_DIM, 0).astype(BF16)


_TRANS_B = (((1,), (1,)), ((), ()))


def _attn_prompt_kernel(*refs, blocks_per_seq, n_blocks):
    step = pl.program_id(0)

    @pl.when(step >= n_blocks)
    def _():
        o_ref = refs[7]
        o_ref[...] = jnp.zeros(o_ref.shape, o_ref.dtype)

    @pl.when(step < n_blocks)
    def _():
        _attn_prompt_block(lax.rem(step, blocks_per_seq) == 0, *refs)


def _attn_prompt_block(first, sinks_ref, q_ref, kc_ref, kp_ref, vc_ref, vp_ref, bias_ref, o_ref,
                       kpad_ref, vtpad_ref, s_ref, e_ref, m_ref, den_ref):
    n_heads = bias_ref.shape[0]
    group = n_heads // kpad_ref.shape[0]
    _stage_kv(jnp.concatenate([kp_ref[...], kc_ref[...]], axis=0),
              jnp.concatenate([vp_ref[...], vc_ref[...]], axis=0), kpad_ref, vtpad_ref)
    key = lax.broadcasted_iota(jnp.int32, (N_KEYS, BLOCK), 0)
    qry = lax.broadcasted_iota(jnp.int32, (N_KEYS, BLOCK), 1)
    valid = (key >= qry) & (key <= qry + WINDOW) & ((key >= BLOCK) | jnp.logical_not(first))

    for p in range(n_heads // 2):
        q_slab = (q_ref[:, p * LANES:(p + 1) * LANES] * HEAD_DIM ** -0.5).astype(BF16)
        for half in range(2):
            head = 2 * p + half
            s = lax.dot_general(kpad_ref[head // group, half], q_slab, _TRANS_B, preferred_element_type=F32)
            s = jnp.where(valid, s + bias_ref[head], NEG)
            s_ref[head] = s
            m_ref[head:head + 1, :] = jnp.maximum(jnp.max(s, axis=0, keepdims=True), sinks_ref[head])
    for head in range(n_heads):
        m = m_ref[head:head + 1, :]
        e = jnp.exp(s_ref[head] - m)
        den_ref[head:head + 1, :] = jnp.sum(e, axis=0, keepdims=True) + jnp.exp(sinks_ref[head] - m)
        e_ref[head] = e.astype(BF16)
    first_head = lax.broadcasted_iota(jnp.int32, (LANES, BLOCK), 0) < HEAD_DIM
    for p in range(n_heads // 2):
        kv = 2 * p // group
        acc = (jnp.dot(vtpad_ref[kv, 0], e_ref[2 * p], preferred_element_type=F32)
               + jnp.dot(vtpad_ref[kv, 1], e_ref[2 * p + 1], preferred_element_type=F32))
        inv = jnp.where(first_head, 1.0 / den_ref[2 * p:2 * p + 1, :], 1.0 / den_ref[2 * p + 1:2 * p + 2, :])
        o_ref[:, p * LANES:(p + 1) * LANES] = (acc * inv).T.astype(o_ref.dtype)


def _attn_prompt(qkv, bias, sinks, n_seq, seq, n_heads, n_kv, t_total):
    d = n_heads * HEAD_DIM
    kvw = n_kv * HEAD_DIM
    assert kvw % LANES == 0 and d % kvw == 0 and seq % BLOCK == 0
    nb = seq // BLOCK
    kcol, vcol = d // kvw, d // kvw + 1
    n_blocks = n_seq * nb
    n_tail = pl.cdiv(t_total - n_seq * seq, BLOCK)
    cur = lambda col: (lambda s: (s, col))
    prev = lambda col: (lambda s: (jnp.maximum(s - 1, 0), col))
    return pl.pallas_call(
        functools.partial(_attn_prompt_kernel, blocks_per_seq=nb, n_blocks=n_blocks),
        grid=(n_blocks + n_tail,),
        in_specs=[
            pl.BlockSpec(memory_space=pltpu.SMEM),
            pl.BlockSpec((BLOCK, d), cur(0)),
            pl.BlockSpec((BLOCK, kvw), cur(kcol)),
            pl.BlockSpec((BLOCK, kvw), prev(kcol)),
            pl.BlockSpec((BLOCK, kvw), cur(vcol)),
            pl.BlockSpec((BLOCK, kvw), prev(vcol)),
            pl.BlockSpec((n_heads, N_KEYS, BLOCK), lambda s: (0, 0, 0), pipeline_mode=pl.Buffered(1)),
        ],
        out_specs=pl.BlockSpec((BLOCK, d), cur(0)),
        out_shape=jax.ShapeDtypeStruct((t_total, d), BF16),
        scratch_shapes=[
            pltpu.VMEM((n_kv, 2, N_KEYS, LANES), BF16),
            pltpu.VMEM((n_kv, 2, LANES, N_KEYS), BF16),
            pltpu.VMEM((n_heads, N_KEYS, BLOCK), F32),
            pltpu.VMEM((n_heads, N_KEYS, BLOCK), BF16),
            pltpu.VMEM((n_heads, BLOCK), F32),
            pltpu.VMEM((n_heads, BLOCK), F32),
        ],
        compiler_params=_params("arbitrary"),
        name="attn_prompt",
    )(sinks, qkv, qkv, qkv, qkv, qkv, bias)


def _attn_sample_kernel(sinks_ref, q_ref, kn_ref, vn_ref, kc_ref, vc_ref, bias_ref, att_in_ref, o_ref,
                        kpad_ref, vtpad_ref):
    del att_in_ref
    seq, kvw = kn_ref.shape
    n_pairs = q_ref.shape[1] // LANES
    n_kv = kvw // HEAD_DIM
    kv_lanes = n_pairs // n_kv * seq
    pad = jnp.zeros((N_KEYS - kc_ref.shape[0] - seq, kvw), F32)
    _stage_kv(jnp.concatenate([kc_ref[...], kn_ref[...], pad], axis=0),
              jnp.concatenate([vc_ref[...], vn_ref[...], pad], axis=0), kpad_ref, vtpad_ref)

    rows = [q_ref[:, p * LANES:(p + 1) * LANES] for p in range(n_pairs)]
    if n_pairs * seq < LANES:
        rows.append(jnp.zeros((LANES - n_pairs * seq, LANES), F32))
    qs = (jnp.concatenate(rows, axis=0) * HEAD_DIM ** -0.5).astype(BF16)

    key = lax.broadcasted_iota(jnp.int32, (N_KEYS, LANES), 0)
    lane = lax.broadcasted_iota(jnp.int32, (N_KEYS, LANES), 1)
    lane1 = lax.broadcasted_iota(jnp.int32, (1, LANES), 1)
    qry = jnp.bitwise_and(lane, seq - 1)
    valid = (key >= qry) & (key <= qry + WINDOW)

    probs, dens = [], []
    for half in range(2):
        s = jnp.zeros((N_KEYS, LANES), F32)
        for kv in range(n_kv):
            s_kv = lax.dot_general(kpad_ref[kv, half], qs, _TRANS_B, preferred_element_type=F32)
            s = jnp.where((lane >= kv * kv_lanes) & (lane < (kv + 1) * kv_lanes), s_kv, s)
        sink = jnp.zeros((1, LANES), F32)
        for p in range(n_pairs):
            sink = jnp.where((lane1 >= p * seq) & (lane1 < (p + 1) * seq), sinks_ref[2 * p + half], sink)
        s = jnp.where(valid, s + bias_ref[half], NEG)
        m = jnp.maximum(jnp.max(s, axis=0, keepdims=True), sink)
        e = jnp.exp(s - m)
        dens.append(jnp.sum(e, axis=0, keepdims=True) + jnp.exp(sink - m))
        probs.append(e.astype(BF16))

    row_t = lax.broadcasted_iota(jnp.int32, (LANES, LANES), 0)
    lane_t = lax.broadcasted_iota(jnp.int32, (LANES, LANES), 1)
    out_t = jnp.zeros((LANES, LANES), F32)
    for kv in range(n_kv):
        acc = (jnp.dot(vtpad_ref[kv, 0], probs[0], preferred_element_type=F32)
               + jnp.dot(vtpad_ref[kv, 1], probs[1], preferred_element_type=F32))
        out_t = jnp.where((lane_t >= kv * kv_lanes) & (lane_t < (kv + 1) * kv_lanes), acc, out_t)
    inv = jnp.where(row_t < HEAD_DIM, 1.0 / dens[0], 1.0 / dens[1])
    out = (out_t * inv).T
    for p in range(n_pairs):
        o_ref[:, p * LANES:(p + 1) * LANES] = out[p * seq:(p + 1) * seq].astype(o_ref.dtype)


def _attn_sample(qkv, att, cache_k, cache_v, bias, sinks, t_prompt, seq, n_heads):
    n_seq, kv_buf, kvw = cache_k.shape
    d = n_heads * HEAD_DIM
    assert kv_buf == BLOCK and seq == SUBLANES and t_prompt % seq == 0
    rb = t_prompt // seq
    kcol, vcol = d // kvw, d // kvw + 1
    return pl.pallas_call(
        _attn_sample_kernel,
        grid=(n_seq,),
        in_specs=[
            pl.BlockSpec(memory_space=pltpu.SMEM),
            pl.BlockSpec((seq, d), lambda n: (rb + n, 0)),
            pl.BlockSpec((seq, kvw), lambda n: (rb + n, kcol)),
            pl.BlockSpec((seq, kvw), lambda n: (rb + n, vcol)),
            pl.BlockSpec((None, kv_buf, kvw), lambda n: (n, 0, 0)),
            pl.BlockSpec((None, kv_buf, kvw), lambda n: (n, 0, 0)),
            pl.BlockSpec((2, N_KEYS, LANES), lambda n: (0, 0, 0)),
            pl.BlockSpec(memory_space=pl.ANY),
        ],
        out_specs=pl.BlockSpec((seq, d), lambda n: (rb + n, 0)),
        out_shape=jax.ShapeDtypeStruct(att.shape, att.dtype),
        scratch_shapes=[
            pltpu.VMEM((kvw // HEAD_DIM, 2, N_KEYS, LANES), BF16),
            pltpu.VMEM((kvw // HEAD_DIM, 2, LANES, N_KEYS), BF16),
        ],
        input_output_aliases={7: 0},
        compiler_params=_params("arbitrary"),
        name="attn_sample",
    )(sinks, qkv, qkv, qkv, cache_k, cache_v, bias, att)


def kernel(x_prompt, x_sample, state_conv, state_pool, cache_k, cache_v, norm_g, w_ffn_in,
           w_ffn_out, w_mix_in, conv_w, pool_w, pool_scale, w_mix_out, w_qkv, w_o,
           attn_sinks, rel_bias):
    n_p, seq_p, d = x_prompt.shape
    n_s, seq_s, _ = x_sample.shape
    depth = norm_g.shape[0]
    n_heads = d // HEAD_DIM
    n_kv = cache_k.shape[3]
    kv_buf = cache_k.shape[2]
    kvw = n_kv * HEAD_DIM
    conv_ctx = state_conv.shape[2]
    pool_ctx = state_pool.shape[2]
    t_p, t_s = n_p * seq_p, n_s * seq_s
    t = t_p + t_s
    tm = _divisor_tile(t, ROW_TILE, 16)
    tm_proj = _divisor_tile(t, PROJ_ROW_TILE, 16)

    tm_ffn = _divisor_tile(t, FFN_ROW_TILE, 16)
    x, h = _merge(x_prompt.reshape(t_p, d), x_sample.reshape(t_s, d), norm_g, 0, FFN1_PRE)
    bias_p, bias_s = _bias_tables(rel_bias, seq_s)
    out_tile = _divisor_tile(d, 512, LANES)
    w_mix_out = _column_tiles_bf16(w_mix_out, out_tile)
    w_o = _column_tiles_bf16(w_o, out_tile)

    conv_p, pool_p, k_p, v_p, conv_s, pool_s, k_s, v_s = ([] for _ in range(8))
    for i in range(depth):
        x, h = _ffn(h, x, norm_g, w_ffn_in, w_ffn_out, i, 0, tm_ffn, nxt=(i, MIX_PRE))
        j = i // 2
        if i % 2 == 0:
            z = _proj(h, w_mix_in, j, tm_proj)
            m, cst, pst = _mix_prompt(z, conv_w[j], pool_w[j], pool_scale[j][None], n_p, seq_p, t)
            conv_p.append(cst[:, CONV_HALO - conv_ctx:])
            pool_p.append(pst[:, POOL_HALO - pool_ctx:])
            cst_in = jnp.pad(state_conv[j], ((0, 0), (CONV_HALO - conv_ctx, 0), (0, 0)))
            pst_in = jnp.pad(state_pool[j], ((0, 0), (POOL_HALO - pool_ctx, 0), (0, 0)))
            m, cst, pst = _mix_sample(z, m, cst_in, pst_in, conv_w[j], pool_w[j], pool_scale[j][None],
                                      t_p, seq_s, PAST_LEN)
            conv_s.append(cst[:, CONV_HALO - conv_ctx:])
            pool_s.append(pst[:, POOL_HALO - pool_ctx:])
            x, h = _out(m, w_mix_out, j, x, norm_g, i, tm)
        else:
            qkv = _proj(h, w_qkv, j, tm_proj)
            att = _attn_prompt(qkv, bias_p, attn_sinks[j], n_p, seq_p, n_heads, n_kv, t)
            ck = cache_k[j].reshape(n_s, kv_buf, kvw)
            cv = cache_v[j].reshape(n_s, kv_buf, kvw)
            att = _attn_sample(qkv, att, ck, cv, bias_s, attn_sinks[j], t_p, seq_s, n_heads)
            k_new, v_new = qkv[:, d:d + kvw], qkv[:, d + kvw:]
            kv_p = lambda a: a[:t_p].reshape(n_p, seq_p, n_kv, HEAD_DIM)[:, seq_p - kv_buf:]
            k_p.append(kv_p(k_new))
            v_p.append(kv_p(v_new))
            kv_s = lambda c, a: jnp.concatenate(
                [c, a[t_p:].reshape(n_s, seq_s, kvw)], axis=1)[:, seq_s:].reshape(n_s, kv_buf, n_kv, HEAD_DIM)
            k_s.append(kv_s(ck, k_new))
            v_s.append(kv_s(cv, v_new))
            x, h = _out(att, w_o, j, x, norm_g, i, tm)
        if i + 1 < depth:
            x, h = _ffn(h, x, norm_g, w_ffn_in, w_ffn_out, i, 1, tm_ffn, nxt=(i + 1, FFN1_PRE))
        else:
            y_p, y_s = _ffn(h, x, norm_g, w_ffn_in, w_ffn_out, i, 1, tm_ffn, head_rows=t_p)

    y_p = y_p.reshape(n_p, seq_p, d)
    y_s = y_s.reshape(n_s, seq_s, d)
    st = jnp.stack
    return (y_p, y_s, st(conv_p), st(pool_p), st(k_p), st(v_p), st(conv_s), st(pool_s), st(k_s), st(v_s))
```

```python
import functools
import math

import jax
import jax.numpy as jnp
from jax import lax
from jax.experimental import pallas as pl
from jax.experimental.pallas import tpu as pltpu

F32 = jnp.float32
BF16 = jnp.bfloat16

EPS = 1e-6
NEG = -1e30
HEAD_DIM = 64
WINDOW = 128
BLOCK = WINDOW
CONV_K = 3
POOL_WINDOWS = (2, 4, 8, 16)
N_BUCKETS = 32
MAX_DISTANCE = 128
PAST_LEN = 16384

LANES = 128
SUBLANES = 8
CONV_HALO = SUBLANES
POOL_HALO = 2 * SUBLANES
FF_TILE = 2 * LANES
VMEM_LIMIT_BYTES = 58 * 1024 * 1024
ROW_TILE = 1056
FFN_ROW_TILE = 1408
PROJ_ROW_TILE = 2112


def _divisor_tile(n, target, mult):
    best = None
    for t in range(mult, min(n, target) + 1, mult):
        if n % t == 0:
            best = t
    assert best is not None, (n, target, mult)
    return best


def _params(*sem):
    return pltpu.CompilerParams(dimension_semantics=sem, vmem_limit_bytes=VMEM_LIMIT_BYTES)


def _rms(xf, g):
    ms = jnp.mean(xf * xf, axis=-1, keepdims=True)
    return xf * lax.rsqrt(ms + EPS) * g


def _for_row_chunks(n_rows, body):
    rc = _divisor_tile(n_rows, 272, 16)

    def step(c, carry):
        body(pl.ds(pl.multiple_of(c * rc, rc), rc))
        return carry

    lax.fori_loop(0, n_rows // rc, step, 0)


def _merge_kernel(xp_ref, xs_ref, g_ref, x_ref, h_ref, *, n_prompt_blocks, row):
    x = jnp.where(pl.program_id(0) < n_prompt_blocks, xp_ref[...], xs_ref[...])
    x_ref[...] = x
    h_ref[...] = _rms(x, g_ref[row:row + 1, :]).astype(BF16)


def _merge(x_prompt, x_sample, norm_g, layer, row):
    (t_p, d), t_s = x_prompt.shape, x_sample.shape[0]
    assert t_p % t_s == 0 and t_s % 16 == 0
    nbp = t_p // t_s
    row_block = pl.BlockSpec((t_s, d), lambda i: (i, 0))
    return pl.pallas_call(
        functools.partial(_merge_kernel, n_prompt_blocks=nbp, row=row),
        grid=(nbp + 1,),
        in_specs=[
            pl.BlockSpec((t_s, d), lambda i: (jnp.minimum(i, nbp - 1), 0)),
            pl.BlockSpec((t_s, d), lambda i: (0, 0)),
            pl.BlockSpec((None,) + norm_g.shape[1:], lambda i: (layer, 0, 0)),
        ],
        out_specs=[row_block, row_block],
        out_shape=[jax.ShapeDtypeStruct((t_p + t_s, d), F32), jax.ShapeDtypeStruct((t_p + t_s, d), BF16)],
        compiler_params=_params("parallel"),
        name="merge_rows",
    )(x_prompt, x_sample, norm_g)


def _ffn_kernel(h_ref, xr_ref, g_ref, gn_ref, wg_ref, wul_ref, wuh_ref, wol_ref, woh_ref,
                xo_ref, *rest, d_ff, nj, nb, nc, post, nxt, n_head_chunks):
    assert nxt is None or n_head_chunks is None
    o2_ref, acc_ref = rest if len(rest) == 2 else (None,) + rest
    i, j = pl.program_id(0), pl.program_id(1)
    tm, d = h_ref.shape
    rc = xr_ref.shape[0]
    cur = lax.rem(i, 2)
    prv = 1 - cur

    @pl.when((i == 0) & (j == 0))
    def _():
        def zero(rows):
            acc_ref[0, rows, :] = jnp.zeros((rows.size, d), F32)
            acc_ref[1, rows, :] = jnp.zeros((rows.size, d), F32)
        _for_row_chunks(tm, zero)

    half = FF_TILE // 2
    wc = _divisor_tile(d, 512, LANES)
    row_halves = [slice(0, tm // 2), slice(tm // 2, tm)]

    def swiglu(row_sets, w, wo):
        width = wo.shape[0]
        hidden = []
        for rows in row_sets:
            r = jnp.dot(h_ref[rows, :], w, preferred_element_type=F32)
            gate, up = r[:, :width], r[:, width:]
            hidden.append(((gate * jax.nn.sigmoid(gate)) * up).astype(BF16))
        for rows, a in zip(row_sets, hidden):
            for c in range(d // wc):
                acc_ref[cur, rows, c * wc:(c + 1) * wc] += jnp.dot(a, wo[:, c * wc:(c + 1) * wc],
                                                                   preferred_element_type=F32)

    def tile():
        w = jnp.concatenate([wg_ref[...].astype(BF16), wul_ref[...].astype(BF16),
                             wuh_ref[...].astype(BF16)], axis=1)
        wo = jnp.concatenate([wol_ref[...].astype(BF16), woh_ref[...].astype(BF16)], axis=0)
        swiglu(row_halves, w, wo)

    def end_tile():
        w = jnp.concatenate([wg_ref[:, :half].astype(BF16), wul_ref[...].astype(BF16)], axis=1)
        wo = wol_ref[...].astype(BF16)
        swiglu(row_halves, w, wo)

    def finish_chunk():
        rows = pl.ds(pl.multiple_of(j * rc, rc), rc)
        y = xr_ref[...] + 0.5 * _rms(acc_ref[prv, rows, :], g_ref[post:post + 1, :])
        acc_ref[prv, rows, :] = jnp.zeros((rc, d), F32)
        if nxt is not None:
            xo_ref[...] = y
            o2_ref[...] = _rms(y, gn_ref[nxt:nxt + 1, :]).astype(BF16)
        elif n_head_chunks is not None:
            in_head = (i - 1) * nc + j < n_head_chunks

            @pl.when(in_head)
            def _():
                xo_ref[...] = y

            @pl.when(jnp.logical_not(in_head))
            def _():
                o2_ref[...] = y
        else:
            xo_ref[...] = y

    real = i < nb
    n_full = nj if d_ff % FF_TILE == 0 else nj - 1

    @pl.when(real & (j < nc) & (j < n_full))
    def _():
        finish_chunk()
        tile()

    @pl.when(real & (j >= nc) & (j < n_full))
    def _():
        tile()

    if n_full < nj:
        @pl.when(real & (j == n_full))
        def _():
            if n_full < nc:
                finish_chunk()
            end_tile()

    @pl.when(jnp.logical_not(real) & (j < nc))
    def _():
        finish_chunk()


def _ffn(h, x, norm_g, w_in, w_out, layer, which, tm, nxt=None, head_rows=None):
    t, d = x.shape
    d_ff = w_out.shape[2]
    half = FF_TILE // 2
    assert d_ff % half == 0 and w_in.shape[2:] == (d, 2 * d_ff) and t % tm == 0
    n_half = d_ff // half
    nj = pl.cdiv(d_ff, FF_TILE)
    nb = t // tm
    last_up = 2 * n_half - 1
    rc = min(c for c in range(16, tm + 1, 16) if tm % c == 0 and tm // c <= nj)
    nc = tm // rc
    nxt_layer, nxt_row = nxt if nxt is not None else (layer, None)
    n_head = None
    if head_rows is not None:
        assert nxt is None and head_rows % rc == 0 and (t - head_rows) % rc == 0
        n_head = head_rows // rc
    kern = functools.partial(_ffn_kernel, d_ff=d_ff, nj=nj, nb=nb, nc=nc, post=4 * which + 1, nxt=nxt_row,
                             n_head_chunks=n_head)
    jw = lambda i, j: jnp.where(i < nb, j, nj - 1)
    w_in_spec = lambda width, col: pl.BlockSpec(
        (None, None, d, width), lambda i, j: (layer, which, 0, col(jw(i, j))))
    w_out_spec = lambda row: pl.BlockSpec(
        (None, None, half, d), lambda i, j: (layer, which, row(jw(i, j)), 0))
    chunk = lambda i, j: (jnp.maximum((i - 1) * nc + jnp.minimum(j, nc - 1), 0), 0)
    g_spec = lambda l: pl.BlockSpec((None,) + norm_g.shape[1:], lambda i, j: (l, 0, 0))
    if nxt is not None:
        out_specs = [pl.BlockSpec((rc, d), chunk), pl.BlockSpec((rc, d), chunk)]
        out_shape = [jax.ShapeDtypeStruct((t, d), F32), jax.ShapeDtypeStruct((t, d), BF16)]
    elif n_head is not None:
        out_specs = [pl.BlockSpec((rc, d), lambda i, j: (jnp.minimum(chunk(i, j)[0], n_head - 1), 0)),
                     pl.BlockSpec((rc, d), lambda i, j: (jnp.maximum(chunk(i, j)[0] - n_head, 0), 0))]
        out_shape = [jax.ShapeDtypeStruct((head_rows, d), F32), jax.ShapeDtypeStruct((t - head_rows, d), F32)]
    else:
        out_specs = [pl.BlockSpec((rc, d), chunk)]
        out_shape = [jax.ShapeDtypeStruct((t, d), F32)]
    res = pl.pallas_call(
        kern,
        grid=(nb + 1, nj),
        in_specs=[
            pl.BlockSpec((tm, d), lambda i, j: (jnp.minimum(i, nb - 1), 0)),
            pl.BlockSpec((rc, d), chunk),
            g_spec(layer),
            g_spec(nxt_layer),
            w_in_spec(FF_TILE, lambda j: j),
            w_in_spec(half, lambda j: n_half + 2 * j),
            w_in_spec(half, lambda j: jnp.minimum(n_half + 2 * j + 1, last_up)),
            w_out_spec(lambda j: 2 * j),
            w_out_spec(lambda j: jnp.minimum(2 * j + 1, n_half - 1)),
        ],
        out_specs=out_specs,
        out_shape=out_shape,
        scratch_shapes=[pltpu.VMEM((2, tm, d), F32)],
        compiler_params=_params("arbitrary", "arbitrary"),
        name="ffn",
    )(h, x, norm_g, norm_g, w_in, w_in, w_in, w_out, w_out)
    return (res[0], res[1]) if len(res) == 2 else (res[0], None)


FFN1_PRE, MIX_PRE, MIX_POST, FFN2_PRE = 0, 2, 3, 4


def _proj_kernel(h_ref, w_ref, o_ref):
    o_ref[...] = jnp.dot(h_ref[...], w_ref[...].astype(BF16), preferred_element_type=F32)


def _proj(h, w, widx, tm):
    t, d = h.shape
    n = w.shape[2]
    tn = _divisor_tile(n, 512, LANES)
    return pl.pallas_call(
        _proj_kernel,
        grid=(t // tm, n // tn),
        in_specs=[
            pl.BlockSpec((tm, d), lambda i, j: (i, 0)),
            pl.BlockSpec((None, d, tn), lambda i, j: (widx, 0, j)),
        ],
        out_specs=pl.BlockSpec((tm, tn), lambda i, j: (i, j)),
        out_shape=jax.ShapeDtypeStruct((t, n), F32),
        compiler_params=_params("parallel", "arbitrary"),
        name="proj",
    )(h, w)


def _column_tiles_kernel(w_ref, o_ref):
    o_ref[...] = w_ref[...].astype(BF16)


def _column_tiles_bf16(w, tn):
    n, kdim, d = w.shape
    return pl.pallas_call(
        _column_tiles_kernel,
        grid=(n, d // tn),
        in_specs=[pl.BlockSpec((None, kdim, tn), lambda a, j: (a, 0, j))],
        out_specs=pl.BlockSpec((None, None, kdim, tn), lambda a, j: (a, j, 0, 0)),
        out_shape=jax.ShapeDtypeStruct((n, d // tn, kdim, tn), BF16),
        compiler_params=_params("parallel", "parallel"),
        name="column_tiles_bf16",
    )(w)


def _out_kernel(m_ref, w_ref, xr_ref, g_ref, xo_ref, ho_ref, acc_ref, *, nb, nc):
    i, j = pl.program_id(0), pl.program_id(1)
    ns, tm, tn = acc_ref.shape[1:]
    rc = xr_ref.shape[0]
    cur = lax.rem(i, 2)
    prv = 1 - cur

    @pl.when((i == 0) & (j == 0))
    def _():
        acc_ref[1] = jnp.zeros(acc_ref.shape[1:], F32)

    def tile():
        acc_ref[cur, j] = jnp.dot(m_ref[...], w_ref[j], preferred_element_type=F32)

    def finish_chunk():
        sub = _divisor_tile(rc, 176, 16)
        for r in range(rc // sub):
            rows = pl.ds(pl.multiple_of(j * rc + r * sub, 16), sub)
            o = jnp.concatenate([acc_ref[prv, s, rows, :] for s in range(ns)], axis=1)
            y = xr_ref[r * sub:(r + 1) * sub, :] + _rms(o, g_ref[MIX_POST:MIX_POST + 1, :])
            xo_ref[r * sub:(r + 1) * sub, :] = y
            ho_ref[r * sub:(r + 1) * sub, :] = _rms(y, g_ref[FFN2_PRE:FFN2_PRE + 1, :]).astype(BF16)

    real = i < nb

    @pl.when(real & (j < nc))
    def _():
        finish_chunk()
        tile()

    @pl.when(real & (j >= nc))
    def _():
        tile()

    @pl.when(jnp.logical_not(real) & (j < nc))
    def _():
        finish_chunk()


def _out(m, w, widx, x, norm_g, layer, tm):
    t, d = x.shape
    ns, kdim, tn = w.shape[1:]
    assert ns * tn == d
    nb = t // tm
    rc = min(c for c in range(16, tm + 1, 16) if tm % c == 0 and tm // c <= ns)
    nc = tm // rc
    chunk = pl.BlockSpec((rc, d), lambda i, j: (jnp.maximum((i - 1) * nc + jnp.minimum(j, nc - 1), 0), 0))
    return pl.pallas_call(
        functools.partial(_out_kernel, nb=nb, nc=nc),
        grid=(nb + 1, ns),
        in_specs=[
            pl.BlockSpec((tm, kdim), lambda i, j: (jnp.minimum(i, nb - 1), 0)),
            pl.BlockSpec((None, ns, kdim, tn), lambda i, j: (widx, 0, 0, 0), pipeline_mode=pl.Buffered(1)),
            chunk,
            pl.BlockSpec((None,) + norm_g.shape[1:], lambda i, j: (layer, 0, 0)),
        ],
        out_specs=[chunk, chunk],
        out_shape=[jax.ShapeDtypeStruct((t, d), F32), jax.ShapeDtypeStruct((t, d), BF16)],
        scratch_shapes=[pltpu.VMEM((2, ns, tm, tn), F32)],
        compiler_params=_params("arbitrary", "arbitrary"),
        name="outproj",
    )(m, w, x, norm_g)


def _conv3(ext, cw):
    return cw[0:1] * pltpu.roll(ext, 2, 0) + cw[1:2] * pltpu.roll(ext, 1, 0) + cw[2:3] * ext


def _window_sum(ext, w):
    s, k = ext, 1
    while k < w:
        s = s + pltpu.roll(s, k, 0)
        k *= 2
    return s


def _pool_group(win, cnt, ug, pw, scale):
    dlt = win / cnt - ug
    return jnp.dot(dlt.astype(BF16), pw.astype(BF16), preferred_element_type=F32) * scale


def _mix_prompt_kernel(hc_ref, gc_ref, gb_ref, u_ref, hch_ref, gch_ref, uh_ref, cw_ref, pw_ref,
                       ps_ref, m_ref, cst_ref, pst_ref, *, chunks_per_seq, n_chunks):
    rows, c = hc_ref.shape
    gw = c // len(POOL_WINDOWS)
    ci = pl.program_id(0) % chunks_per_seq
    first = ci == 0

    @pl.when(pl.program_id(0) >= n_chunks)
    def _():
        m_ref[...] = jnp.zeros(m_ref.shape, BF16)

    @pl.when(pl.program_id(0) < n_chunks)
    def _():
        v = gc_ref[...] * hc_ref[...]
        v_halo = jnp.where(first, 0.0, gch_ref[...] * hch_ref[...])
        y = _conv3(jnp.concatenate([v_halo, v], axis=0), cw_ref[...])[CONV_HALO:]
        m_ref[:, :c] = (gb_ref[...] * y).astype(BF16)
        cst_ref[...] = v[rows - CONV_HALO:]

        u = u_ref[...]
        u_ext = jnp.concatenate([jnp.where(first, 0.0, uh_ref[...]), u], axis=0)
        pst_ref[...] = u_ext[rows:]
        pos1 = ci * rows + lax.broadcasted_iota(jnp.int32, (rows, 1), 0) + 1
        for gi, w in enumerate(POOL_WINDOWS):
            sl = slice(gi * gw, (gi + 1) * gw)
            win = _window_sum(u_ext[:, sl], w)[POOL_HALO:]
            cnt = jnp.minimum(w, pos1).astype(F32)
            yp = _pool_group(win, cnt, u[:, sl], pw_ref[gi], ps_ref[:, sl])
            m_ref[:, c + gi * gw:c + (gi + 1) * gw] = yp.astype(BF16)


def _mix_prompt(z, conv_w, pool_w, pool_scale, n_seq, seq, t_total):
    c = conv_w.shape[1]
    assert z.shape[1] == 4 * c
    t_tail = t_total - n_seq * seq
    rows = max(r for r in range(POOL_HALO, 257, POOL_HALO) if seq % r == 0 and t_tail % r == 0)
    cps = seq // rows
    n_chunks = n_seq * cps
    rh_c, rh_p = rows // CONV_HALO, rows // POOL_HALO

    def halo(ratio, col):
        return lambda i: (jnp.maximum(i * ratio - 1, 0), col)

    kern = functools.partial(_mix_prompt_kernel, chunks_per_seq=cps, n_chunks=n_chunks)
    gw = c // len(POOL_WINDOWS)
    state = lambda i: (jnp.minimum(i // cps, n_seq - 1), 0, 0)
    return pl.pallas_call(
        kern,
        grid=(n_chunks + t_tail // rows,),
        in_specs=[
            pl.BlockSpec((rows, c), lambda i: (i, 0)),
            pl.BlockSpec((rows, c), lambda i: (i, 1)),
            pl.BlockSpec((rows, c), lambda i: (i, 2)),
            pl.BlockSpec((rows, c), lambda i: (i, 3)),
            pl.BlockSpec((CONV_HALO, c), halo(rh_c, 0)),
            pl.BlockSpec((CONV_HALO, c), halo(rh_c, 1)),
            pl.BlockSpec((POOL_HALO, c), halo(rh_p, 3)),
            pl.BlockSpec((CONV_K, c), lambda i: (0, 0)),
            pl.BlockSpec((len(POOL_WINDOWS), gw, gw), lambda i: (0, 0, 0)),
            pl.BlockSpec((1, c), lambda i: (0, 0)),
        ],
        out_specs=[
            pl.BlockSpec((rows, 2 * c), lambda i: (i, 0)),
            pl.BlockSpec((None, CONV_HALO, c), state),
            pl.BlockSpec((None, POOL_HALO, c), state),
        ],
        out_shape=[
            jax.ShapeDtypeStruct((t_total, 2 * c), BF16),
            jax.ShapeDtypeStruct((n_seq, CONV_HALO, c), F32),
            jax.ShapeDtypeStruct((n_seq, POOL_HALO, c), F32),
        ],
        compiler_params=_params("arbitrary"),
        name="mix_prompt",
    )(z, z, z, z, z, z, z, conv_w, pool_w, pool_scale)


def _mix_sample_kernel(hc_ref, gc_ref, gb_ref, u_ref, cst_in_ref, pst_in_ref, cw_ref, pw_ref,
                       ps_ref, m_in_ref, m_ref, cst_ref, pst_ref, *, seq, past_len):
    del m_in_ref
    rows, c = hc_ref.shape
    n_seq = rows // seq
    gw = c // len(POOL_WINDOWS)

    def with_halo(halo3, x2):
        ext = jnp.concatenate([halo3, x2.reshape(n_seq, seq, c)], axis=1)
        return ext, ext.reshape(n_seq * ext.shape[1], c)

    def body_rows(flat, n_halo):
        return flat.reshape(n_seq, n_halo + seq, flat.shape[-1])[:, n_halo:].reshape(rows, flat.shape[-1])

    v = gc_ref[...] * hc_ref[...]
    v_ext3, v_ext = with_halo(cst_in_ref[...], v)
    y = body_rows(_conv3(v_ext, cw_ref[...]), CONV_HALO)
    m_ref[:, :c] = (gb_ref[...] * y).astype(BF16)
    cst_ref[...] = v_ext3[:, seq:]

    u = u_ref[...]
    u_ext3, u_ext = with_halo(pst_in_ref[...], u)
    pst_ref[...] = u_ext3[:, seq:]
    pos1 = past_len + jnp.bitwise_and(lax.broadcasted_iota(jnp.int32, (rows, 1), 0), seq - 1) + 1
    for gi, w in enumerate(POOL_WINDOWS):
        sl = slice(gi * gw, (gi + 1) * gw)
        win = body_rows(_window_sum(u_ext[:, sl], w), POOL_HALO)
        cnt = jnp.minimum(w, pos1).astype(F32)
        yp = _pool_group(win, cnt, u[:, sl], pw_ref[gi], ps_ref[:, sl])
        m_ref[:, c + gi * gw:c + (gi + 1) * gw] = yp.astype(BF16)


def _mix_sample(z, m, conv_state, pool_state, conv_w, pool_w, pool_scale, t_prompt, seq, past_len):
    c = conv_w.shape[1]
    n_seq = conv_state.shape[0]
    rows = n_seq * seq
    assert seq == SUBLANES and t_prompt % rows == 0
    rb = t_prompt // rows
    gw = c // len(POOL_WINDOWS)
    full3 = lambda i: (0, 0, 0)
    kern = functools.partial(_mix_sample_kernel, seq=seq, past_len=past_len)
    return pl.pallas_call(
        kern,
        grid=(1,),
        in_specs=[
            pl.BlockSpec((rows, c), lambda i: (rb, 0)),
            pl.BlockSpec((rows, c), lambda i: (rb, 1)),
            pl.BlockSpec((rows, c), lambda i: (rb, 2)),
            pl.BlockSpec((rows, c), lambda i: (rb, 3)),
            pl.BlockSpec((n_seq, CONV_HALO, c), full3),
            pl.BlockSpec((n_seq, POOL_HALO, c), full3),
            pl.BlockSpec((CONV_K, c), lambda i: (0, 0)),
            pl.BlockSpec((len(POOL_WINDOWS), gw, gw), full3),
            pl.BlockSpec((1, c), lambda i: (0, 0)),
            pl.BlockSpec(memory_space=pl.ANY),
        ],
        out_specs=[
            pl.BlockSpec((rows, 2 * c), lambda i: (rb, 0)),
            pl.BlockSpec((n_seq, CONV_HALO, c), full3),
            pl.BlockSpec((n_seq, POOL_HALO, c), full3),
        ],
        out_shape=[
            jax.ShapeDtypeStruct(m.shape, m.dtype),
            jax.ShapeDtypeStruct((n_seq, CONV_HALO, c), F32),
            jax.ShapeDtypeStruct((n_seq, POOL_HALO, c), F32),
        ],
        input_output_aliases={9: 0},
        compiler_params=_params("arbitrary"),
        name="mix_sample",
    )(z, z, z, z, conv_state, pool_state, conv_w, pool_w, pool_scale, m)


N_KEYS = 2 * BLOCK


def _t5_bucket(dist):
    max_exact = N_BUCKETS // 2
    n = jnp.maximum(dist, 0)
    ratio = jnp.log(jnp.maximum(n, 1).astype(F32) / max_exact) / math.log(MAX_DISTANCE / max_exact)
    large = jnp.minimum(max_exact + (ratio * (N_BUCKETS - max_exact)).astype(jnp.int32), N_BUCKETS - 1)
    return jnp.where(n < max_exact, n, large)


def _bias_prompt_kernel(rb_ref, o_ref):
    heads, n_keys, n_q = o_ref.shape
    s = lax.broadcasted_iota(jnp.int32, (n_keys, n_q), 0)
    q = lax.broadcasted_iota(jnp.int32, (n_keys, n_q), 1)
    bucket = _t5_bucket(BLOCK + q - s)
    for hh in range(heads):
        head = pl.program_id(0) * heads + hh
        acc = jnp.zeros((n_keys, n_q), F32)
        for b in range(N_BUCKETS):
            acc = jnp.where(bucket == b, rb_ref[b, head], acc)
        o_ref[hh] = acc


def _bias_sample_kernel(rb_ref, o_ref, *, seq, n_heads):
    half = pl.program_id(0)
    s = lax.broadcasted_iota(jnp.int32, o_ref.shape, 0)
    lane = lax.broadcasted_iota(jnp.int32, o_ref.shape, 1)
    bucket = _t5_bucket(BLOCK + jnp.bitwise_and(lane, seq - 1) - s)
    lane1 = lax.broadcasted_iota(jnp.int32, (1, LANES), 1)
    acc = jnp.zeros(o_ref.shape, F32)
    for b in range(N_BUCKETS):
        vec = jnp.zeros((1, LANES), F32)
        for slab in range(n_heads // 2):
            vec = jnp.where((lane1 >= slab * seq) & (lane1 < (slab + 1) * seq), rb_ref[b, 2 * slab + half], vec)
        acc = jnp.where(bucket == b, vec, acc)
    o_ref[...] = acc


def _bias_tables(rel_bias, seq_s):
    n_heads = rel_bias.shape[1]
    assert seq_s == SUBLANES and (n_heads // 2) * seq_s <= LANES
    heads_per_step = math.gcd(n_heads, 8)
    prompt = pl.pallas_call(
        _bias_prompt_kernel,
        grid=(n_heads // heads_per_step,),
        in_specs=[pl.BlockSpec(memory_space=pltpu.SMEM)],
        out_specs=pl.BlockSpec((heads_per_step, N_KEYS, BLOCK), lambda h: (h, 0, 0)),
        out_shape=jax.ShapeDtypeStruct((n_heads, N_KEYS, BLOCK), F32),
        compiler_params=_params("arbitrary"),
        name="rel_bias_prompt",
    )(rel_bias)
    sample = pl.pallas_call(
        functools.partial(_bias_sample_kernel, seq=seq_s, n_heads=n_heads),
        grid=(2,),
        in_specs=[pl.BlockSpec(memory_space=pltpu.SMEM)],
        out_specs=pl.BlockSpec((None, N_KEYS, LANES), lambda h: (h, 0, 0)),
        out_shape=jax.ShapeDtypeStruct((2, N_KEYS, LANES), F32),
        compiler_params=_params("arbitrary"),
        name="rel_bias_sample",
    )(rel_bias)
    return prompt, sample


def _stage_kv(k_all, v_all, kpad_ref, vtpad_ref):
    n_keys, kvw = k_all.shape
    lane = lax.broadcasted_iota(jnp.int32, (n_keys, LANES), 1)
    row = lax.broadcasted_iota(jnp.int32, (LANES, n_keys), 0)
    for slab in range(kvw // LANES):
        ks = k_all[:, slab * LANES:(slab + 1) * LANES]
        vt = v_all[:, slab * LANES:(slab + 1) * LANES].T
        for own in range(2):
            kv = 2 * slab + own
            k_own = jnp.where((lane >= own * HEAD_DIM) & (lane < (own + 1) * HEAD_DIM), ks, 0.0)
            v_own = jnp.where((row >= own * HEAD_DIM) & (row < (own + 1) * HEAD_DIM), vt, 0.0)
            kpad_ref[kv, own] = k_own.astype(BF16)
            kpad_ref[kv, 1 - own] = pltpu.roll(k_own, HEAD_DIM, 1).astype(BF16)
            vtpad_ref[kv, own] = v_own.astype(BF16)
            vtpad_ref[kv, 1 - own] = pltpu.roll(v_own, HEAD_DIM, 0).astype(BF16)


_TRANS_B = (((1,), (1,)), ((), ()))


def _attn_prompt_kernel(*refs, blocks_per_seq, n_blocks):
    step = pl.program_id(0)

    @pl.when(step >= n_blocks)
    def _():
        o_ref = refs[7]
        o_ref[...] = jnp.zeros(o_ref.shape, o_ref.dtype)

    @pl.when(step < n_blocks)
    def _():
        _attn_prompt_block(lax.rem(step, blocks_per_seq) == 0, *refs)


def _attn_prompt_block(first, sinks_ref, q_ref, kc_ref, kp_ref, vc_ref, vp_ref, bias_ref, o_ref,
                       kpad_ref, vtpad_ref, s_ref, e_ref, m_ref, den_ref):
    n_heads = bias_ref.shape[0]
    group = n_heads // kpad_ref.shape[0]
    _stage_kv(jnp.concatenate([kp_ref[...], kc_ref[...]], axis=0),
              jnp.concatenate([vp_ref[...], vc_ref[...]], axis=0), kpad_ref, vtpad_ref)
    key = lax.broadcasted_iota(jnp.int32, (N_KEYS, BLOCK), 0)
    qry = lax.broadcasted_iota(jnp.int32, (N_KEYS, BLOCK), 1)
    valid = (key >= qry) & (key <= qry + WINDOW) & ((key >= BLOCK) | jnp.logical_not(first))

    for p in range(n_heads // 2):
        q_slab = (q_ref[:, p * LANES:(p + 1) * LANES] * HEAD_DIM ** -0.5).astype(BF16)
        for half in range(2):
            head = 2 * p + half
            s = lax.dot_general(kpad_ref[head // group, half], q_slab, _TRANS_B, preferred_element_type=F32)
            s = jnp.where(valid, s + bias_ref[head], NEG)
            s_ref[head] = s
            m_ref[head:head + 1, :] = jnp.maximum(jnp.max(s, axis=0, keepdims=True), sinks_ref[head])
    for head in range(n_heads):
        m = m_ref[head:head + 1, :]
        e = jnp.exp(s_ref[head] - m)
        den_ref[head:head + 1, :] = jnp.sum(e, axis=0, keepdims=True) + jnp.exp(sinks_ref[head] - m)
        e_ref[head] = e.astype(BF16)
    first_head = lax.broadcasted_iota(jnp.int32, (LANES, BLOCK), 0) < HEAD_DIM
    for p in range(n_heads // 2):
        kv = 2 * p // group
        acc = (jnp.dot(vtpad_ref[kv, 0], e_ref[2 * p], preferred_element_type=F32)
               + jnp.dot(vtpad_ref[kv, 1], e_ref[2 * p + 1], preferred_element_type=F32))
        inv = jnp.where(first_head, 1.0 / den_ref[2 * p:2 * p + 1, :], 1.0 / den_ref[2 * p + 1:2 * p + 2, :])
        o_ref[:, p * LANES:(p + 1) * LANES] = (acc * inv).T.astype(o_ref.dtype)


def _attn_prompt(qkv, bias, sinks, n_seq, seq, n_heads, n_kv, t_total):
    d = n_heads * HEAD_DIM
    kvw = n_kv * HEAD_DIM
    assert kvw % LANES == 0 and d % kvw == 0 and seq % BLOCK == 0
    nb = seq // BLOCK
    kcol, vcol = d // kvw, d // kvw + 1
    n_blocks = n_seq * nb
    n_tail = pl.cdiv(t_total - n_seq * seq, BLOCK)
    cur = lambda col: (lambda s: (s, col))
    prev = lambda col: (lambda s: (jnp.maximum(s - 1, 0), col))
    return pl.pallas_call(
        functools.partial(_attn_prompt_kernel, blocks_per_seq=nb, n_blocks=n_blocks),
        grid=(n_blocks + n_tail,),
        in_specs=[
            pl.BlockSpec(memory_space=pltpu.SMEM),
            pl.BlockSpec((BLOCK, d), cur(0)),
            pl.BlockSpec((BLOCK, kvw), cur(kcol)),
            pl.BlockSpec((BLOCK, kvw), prev(kcol)),
            pl.BlockSpec((BLOCK, kvw), cur(vcol)),
            pl.BlockSpec((BLOCK, kvw), prev(vcol)),
            pl.BlockSpec((n_heads, N_KEYS, BLOCK), lambda s: (0, 0, 0), pipeline_mode=pl.Buffered(1)),
        ],
        out_specs=pl.BlockSpec((BLOCK, d), cur(0)),
        out_shape=jax.ShapeDtypeStruct((t_total, d), BF16),
        scratch_shapes=[
            pltpu.VMEM((n_kv, 2, N_KEYS, LANES), BF16),
            pltpu.VMEM((n_kv, 2, LANES, N_KEYS), BF16),
            pltpu.VMEM((n_heads, N_KEYS, BLOCK), F32),
            pltpu.VMEM((n_heads, N_KEYS, BLOCK), BF16),
            pltpu.VMEM((n_heads, BLOCK), F32),
            pltpu.VMEM((n_heads, BLOCK), F32),
        ],
        compiler_params=_params("arbitrary"),
        name="attn_prompt",
    )(sinks, qkv, qkv, qkv, qkv, qkv, bias)


def _attn_sample_kernel(sinks_ref, q_ref, kn_ref, vn_ref, kc_ref, vc_ref, bias_ref, att_in_ref, o_ref,
                        kpad_ref, vtpad_ref):
    del att_in_ref
    seq, kvw = kn_ref.shape
    n_pairs = q_ref.shape[1] // LANES
    n_kv = kvw // HEAD_DIM
    kv_lanes = n_pairs // n_kv * seq
    pad = jnp.zeros((N_KEYS - kc_ref.shape[0] - seq, kvw), F32)
    _stage_kv(jnp.concatenate([kc_ref[...], kn_ref[...], pad], axis=0),
              jnp.concatenate([vc_ref[...], vn_ref[...], pad], axis=0), kpad_ref, vtpad_ref)

    rows = [q_ref[:, p * LANES:(p + 1) * LANES] for p in range(n_pairs)]
    if n_pairs * seq < LANES:
        rows.append(jnp.zeros((LANES - n_pairs * seq, LANES), F32))
    qs = (jnp.concatenate(rows, axis=0) * HEAD_DIM ** -0.5).astype(BF16)

    key = lax.broadcasted_iota(jnp.int32, (N_KEYS, LANES), 0)
    lane = lax.broadcasted_iota(jnp.int32, (N_KEYS, LANES), 1)
    lane1 = lax.broadcasted_iota(jnp.int32, (1, LANES), 1)
    qry = jnp.bitwise_and(lane, seq - 1)
    valid = (key >= qry) & (key <= qry + WINDOW)

    probs, dens = [], []
    for half in range(2):
        s = jnp.zeros((N_KEYS, LANES), F32)
        for kv in range(n_kv):
            s_kv = lax.dot_general(kpad_ref[kv, half], qs, _TRANS_B, preferred_element_type=F32)
            s = jnp.where((lane >= kv * kv_lanes) & (lane < (kv + 1) * kv_lanes), s_kv, s)
        sink = jnp.zeros((1, LANES), F32)
        for p in range(n_pairs):
            sink = jnp.where((lane1 >= p * seq) & (lane1 < (p + 1) * seq), sinks_ref[2 * p + half], sink)
        s = jnp.where(valid, s + bias_ref[half], NEG)
        m = jnp.maximum(jnp.max(s, axis=0, keepdims=True), sink)
        e = jnp.exp(s - m)
        dens.append(jnp.sum(e, axis=0, keepdims=True) + jnp.exp(sink - m))
        probs.append(e.astype(BF16))

    row_t = lax.broadcasted_iota(jnp.int32, (LANES, LANES), 0)
    lane_t = lax.broadcasted_iota(jnp.int32, (LANES, LANES), 1)
    out_t = jnp.zeros((LANES, LANES), F32)
    for kv in range(n_kv):
        acc = (jnp.dot(vtpad_ref[kv, 0], probs[0], preferred_element_type=F32)
               + jnp.dot(vtpad_ref[kv, 1], probs[1], preferred_element_type=F32))
        out_t = jnp.where((lane_t >= kv * kv_lanes) & (lane_t < (kv + 1) * kv_lanes), acc, out_t)
    inv = jnp.where(row_t < HEAD_DIM, 1.0 / dens[0], 1.0 / dens[1])
    out = (out_t * inv).T
    for p in range(n_pairs):
        o_ref[:, p * LANES:(p + 1) * LANES] = out[p * seq:(p + 1) * seq].astype(o_ref.dtype)


def _attn_sample(qkv, att, cache_k, cache_v, bias, sinks, t_prompt, seq, n_heads):
    n_seq, kv_buf, kvw = cache_k.shape
    d = n_heads * HEAD_DIM
    assert kv_buf == BLOCK and seq == SUBLANES and t_prompt % seq == 0
    rb = t_prompt // seq
    kcol, vcol = d // kvw, d // kvw + 1
    return pl.pallas_call(
        _attn_sample_kernel,
        grid=(n_seq,),
        in_specs=[
            pl.BlockSpec(memory_space=pltpu.SMEM),
            pl.BlockSpec((seq, d), lambda n: (rb + n, 0)),
            pl.BlockSpec((seq, kvw), lambda n: (rb + n, kcol)),
            pl.BlockSpec((seq, kvw), lambda n: (rb + n, vcol)),
            pl.BlockSpec((None, kv_buf, kvw), lambda n: (n, 0, 0)),
            pl.BlockSpec((None, kv_buf, kvw), lambda n: (n, 0, 0)),
            pl.BlockSpec((2, N_KEYS, LANES), lambda n: (0, 0, 0)),
            pl.BlockSpec(memory_space=pl.ANY),
        ],
        out_specs=pl.BlockSpec((seq, d), lambda n: (rb + n, 0)),
        out_shape=jax.ShapeDtypeStruct(att.shape, att.dtype),
        scratch_shapes=[
            pltpu.VMEM((kvw // HEAD_DIM, 2, N_KEYS, LANES), BF16),
            pltpu.VMEM((kvw // HEAD_DIM, 2, LANES, N_KEYS), BF16),
        ],
        input_output_aliases={7: 0},
        compiler_params=_params("arbitrary"),
        name="attn_sample",
    )(sinks, qkv, qkv, qkv, cache_k, cache_v, bias, att)


def kernel(x_prompt, x_sample, state_conv, state_pool, cache_k, cache_v, norm_g, w_ffn_in,
           w_ffn_out, w_mix_in, conv_w, pool_w, pool_scale, w_mix_out, w_qkv, w_o,
           attn_sinks, rel_bias):
    n_p, seq_p, d = x_prompt.shape
    n_s, seq_s, _ = x_sample.shape
    depth = norm_g.shape[0]
    n_heads = d // HEAD_DIM
    n_kv = cache_k.shape[3]
    kv_buf = cache_k.shape[2]
    kvw = n_kv * HEAD_DIM
    conv_ctx = state_conv.shape[2]
    pool_ctx = state_pool.shape[2]
    t_p, t_s = n_p * seq_p, n_s * seq_s
    t = t_p + t_s
    tm = _divisor_tile(t, ROW_TILE, 16)
    tm_proj = _divisor_tile(t, PROJ_ROW_TILE, 16)

    tm_ffn = _divisor_tile(t, FFN_ROW_TILE, 16)
    x, h = _merge(x_prompt.reshape(t_p, d), x_sample.reshape(t_s, d), norm_g, 0, FFN1_PRE)
    bias_p, bias_s = _bias_tables(rel_bias, seq_s)
    out_tile = _divisor_tile(d, 512, LANES)
    w_mix_out = _column_tiles_bf16(w_mix_out, out_tile)
    w_o = _column_tiles_bf16(w_o, out_tile)

    conv_p, pool_p, k_p, v_p, conv_s, pool_s, k_s, v_s = ([] for _ in range(8))
    for i in range(depth):
        x, h = _ffn(h, x, norm_g, w_ffn_in, w_ffn_out, i, 0, tm_ffn, nxt=(i, MIX_PRE))
        j = i // 2
        if i % 2 == 0:
            z = _proj(h, w_mix_in, j, tm_proj)
            m, cst, pst = _mix_prompt(z, conv_w[j], pool_w[j], pool_scale[j][None], n_p, seq_p, t)
            conv_p.append(cst[:, CONV_HALO - conv_ctx:])
            pool_p.append(pst[:, POOL_HALO - pool_ctx:])
            cst_in = jnp.pad(state_conv[j], ((0, 0), (CONV_HALO - conv_ctx, 0), (0, 0)))
            pst_in = jnp.pad(state_pool[j], ((0, 0), (POOL_HALO - pool_ctx, 0), (0, 0)))
            m, cst, pst = _mix_sample(z, m, cst_in, pst_in, conv_w[j], pool_w[j], pool_scale[j][None],
                                      t_p, seq_s, PAST_LEN)
            conv_s.append(cst[:, CONV_HALO - conv_ctx:])
            pool_s.append(pst[:, POOL_HALO - pool_ctx:])
            x, h = _out(m, w_mix_out, j, x, norm_g, i, tm)
        else:
            qkv = _proj(h, w_qkv, j, tm_proj)
            att = _attn_prompt(qkv, bias_p, attn_sinks[j], n_p, seq_p, n_heads, n_kv, t)
            ck = cache_k[j].reshape(n_s, kv_buf, kvw)
            cv = cache_v[j].reshape(n_s, kv_buf, kvw)
            att = _attn_sample(qkv, att, ck, cv, bias_s, attn_sinks[j], t_p, seq_s, n_heads)
            k_new, v_new = qkv[:, d:d + kvw], qkv[:, d + kvw:]
            kv_p = lambda a: a[:t_p].reshape(n_p, seq_p, n_kv, HEAD_DIM)[:, seq_p - kv_buf:]
            k_p.append(kv_p(k_new))
            v_p.append(kv_p(v_new))
            kv_s = lambda c, a: jnp.concatenate(
                [c, a[t_p:].reshape(n_s, seq_s, kvw)], axis=1)[:, seq_s:].reshape(n_s, kv_buf, n_kv, HEAD_DIM)
            k_s.append(kv_s(ck, k_new))
            v_s.append(kv_s(cv, v_new))
            x, h = _out(att, w_o, j, x, norm_g, i, tm)
        if i + 1 < depth:
            x, h = _ffn(h, x, norm_g, w_ffn_in, w_ffn_out, i, 1, tm_ffn, nxt=(i + 1, FFN1_PRE))
        else:
            y_p, y_s = _ffn(h, x, norm_g, w_ffn_in, w_ffn_out, i, 1, tm_ffn, head_rows=t_p)

    y_p = y_p.reshape(n_p, seq_p, d)
    y_s = y_s.reshape(n_s, seq_s, d)
    st = jnp.stack
    return (y_p, y_s, st(conv_p), st(pool_p), st(k_p), st(v_p), st(conv_s), st(pool_s), st(k_s), st(v_s))
```

```python
import functools
import math

import jax
import jax.numpy as jnp
from jax import lax
from jax.experimental import pallas as pl
from jax.experimental.pallas import tpu as pltpu

F32 = jnp.float32
BF16 = jnp.bfloat16

EPS = 1e-6
NEG = -1e30
HEAD_DIM = 64
WINDOW = 128
BLOCK = WINDOW
CONV_K = 3
POOL_WINDOWS = (2, 4, 8, 16)
N_BUCKETS = 32
MAX_DISTANCE = 128
PAST_LEN = 16384

LANES = 128
SUBLANES = 8
CONV_HALO = SUBLANES
POOL_HALO = 2 * SUBLANES
FFN_TILINGS = ((1408, 2 * LANES), (1056, 2 * LANES))
VMEM_LIMIT_BYTES = 58 * 1024 * 1024
ROW_TILE = 1056
PROJ_ROW_TILE = 2112


def _divisor_tile(n, target, mult):
    best = None
    for t in range(mult, min(n, target) + 1, mult):
        if n % t == 0:
            best = t
    assert best is not None, (n, target, mult)
    return best


def _params(*sem):
    return pltpu.CompilerParams(dimension_semantics=sem, vmem_limit_bytes=VMEM_LIMIT_BYTES)


def _rms(xf, g):
    ms = jnp.mean(xf * xf, axis=-1, keepdims=True)
    return xf * lax.rsqrt(ms + EPS) * g


def _for_row_chunks(n_rows, body):
    rc = _divisor_tile(n_rows, 272, 16)

    def step(c, carry):
        body(pl.ds(pl.multiple_of(c * rc, rc), rc))
        return carry

    lax.fori_loop(0, n_rows // rc, step, 0)


def _merge_kernel(xp_ref, xs_ref, g_ref, x_ref, h_ref, *, n_prompt_blocks, row):
    x = jnp.where(pl.program_id(0) < n_prompt_blocks, xp_ref[...], xs_ref[...])
    x_ref[...] = x
    h_ref[...] = _rms(x, g_ref[row:row + 1, :]).astype(BF16)


def _merge(x_prompt, x_sample, norm_g, layer, row):
    (t_p, d), t_s = x_prompt.shape, x_sample.shape[0]
    assert t_p % t_s == 0 and t_s % 16 == 0
    nbp = t_p // t_s
    row_block = pl.BlockSpec((t_s, d), lambda i: (i, 0))
    return pl.pallas_call(
        functools.partial(_merge_kernel, n_prompt_blocks=nbp, row=row),
        grid=(nbp + 1,),
        in_specs=[
            pl.BlockSpec((t_s, d), lambda i: (jnp.minimum(i, nbp - 1), 0)),
            pl.BlockSpec((t_s, d), lambda i: (0, 0)),
            pl.BlockSpec((None,) + norm_g.shape[1:], lambda i: (layer, 0, 0)),
        ],
        out_specs=[row_block, row_block],
        out_shape=[jax.ShapeDtypeStruct((t_p + t_s, d), F32), jax.ShapeDtypeStruct((t_p + t_s, d), BF16)],
        compiler_params=_params("parallel"),
        name="merge_rows",
    )(x_prompt, x_sample, norm_g)


def _ffn_kernel(h_ref, xr_ref, g_ref, gn_ref, wg_ref, *rest, d_ff, ff_tile, nb, nc, post, nxt, n_head_chunks):
    assert nxt is None or n_head_chunks is None
    pieces = ff_tile // LANES
    wu_refs, wo_refs, rest = rest[:pieces], rest[pieces:2 * pieces], rest[2 * pieces:]
    xo_ref, rest = rest[0], rest[1:]
    o2_ref, acc_ref = rest if len(rest) == 2 else (None,) + rest
    i, j = pl.program_id(0), pl.program_id(1)
    tm, d = h_ref.shape
    rc = xr_ref.shape[0]
    cur = lax.rem(i, 2)
    prv = 1 - cur

    @pl.when((i == 0) & (j == 0))
    def _():
        def zero(rows):
            acc_ref[0, rows, :] = jnp.zeros((rows.size, d), F32)
            acc_ref[1, rows, :] = jnp.zeros((rows.size, d), F32)
        _for_row_chunks(tm, zero)

    n_full = d_ff // ff_tile
    end_pieces = (d_ff - n_full * ff_tile) // LANES
    wc = _divisor_tile(d, 512, LANES)
    row_halves = [slice(0, tm // 2), slice(tm // 2, tm)]

    def swiglu(row_sets, w, wo):
        width = wo.shape[0]
        hidden = []
        for rows in row_sets:
            r = jnp.dot(h_ref[rows, :], w, preferred_element_type=F32)
            gate, up = r[:, :width], r[:, width:]
            hidden.append(((gate * jax.nn.sigmoid(gate)) * up).astype(BF16))
        for rows, a in zip(row_sets, hidden):
            for c in range(d // wc):
                acc_ref[cur, rows, c * wc:(c + 1) * wc] += jnp.dot(a, wo[:, c * wc:(c + 1) * wc],
                                                                   preferred_element_type=F32)

    def tile(n_pieces=pieces):
        w = jnp.concatenate([wg_ref[:, :n_pieces * LANES].astype(BF16)]
                            + [r[...].astype(BF16) for r in wu_refs[:n_pieces]], axis=1)
        wo = jnp.concatenate([r[...].astype(BF16) for r in wo_refs[:n_pieces]], axis=0)
        swiglu(row_halves, w, wo)

    def end_tile():
        tile(end_pieces)

    def finish_chunk(then=None):
        rows = pl.ds(pl.multiple_of(j * rc, rc), rc)
        y = xr_ref[...] + 0.5 * _rms(acc_ref[prv, rows, :], g_ref[post:post + 1, :])
        acc_ref[prv, rows, :] = jnp.zeros((rc, d), F32)
        if n_head_chunks is None:
            xo_ref[...] = y
            if nxt is not None:
                o2_ref[...] = _rms(y, gn_ref[nxt:nxt + 1, :]).astype(BF16)
        if then is not None:
            then()
        if n_head_chunks is not None:
            in_head = (i - 1) * nc + j < n_head_chunks

            @pl.when(in_head)
            def _():
                xo_ref[...] = y

            @pl.when(jnp.logical_not(in_head))
            def _():
                o2_ref[...] = y

    real = i < nb

    @pl.when(real & (j < nc) & (j < n_full))
    def _():
        finish_chunk(tile)

    @pl.when(real & (j >= nc) & (j < n_full))
    def _():
        tile()

    if end_pieces:
        @pl.when(real & (j == n_full))
        def _():
            if n_full < nc:
                finish_chunk(end_tile)
            else:
                end_tile()

    @pl.when(jnp.logical_not(real) & (j < nc))
    def _():
        finish_chunk()


def _finish_rows(tm, n_steps):
    return min(c for c in range(16, tm + 1, 16) if tm % c == 0 and tm // c <= n_steps)


def _ffn_tiling(t, d_ff, align_rows=()):
    for tm_target, ff_tile in FFN_TILINGS:
        tm = _divisor_tile(t, tm_target, 16)
        rc = _finish_rows(tm, pl.cdiv(d_ff, ff_tile))
        if all(r % rc == 0 for r in align_rows):
            return tm, ff_tile
    raise ValueError((t, d_ff, align_rows))


def _ffn(h, x, norm_g, w_in, w_out, layer, which, tm, ff_tile, nxt=None, head_rows=None):
    t, d = x.shape
    d_ff = w_out.shape[2]
    pieces = ff_tile // LANES
    assert ff_tile % LANES == 0 and d_ff % LANES == 0 and w_in.shape[2:] == (d, 2 * d_ff) and t % tm == 0
    n_pieces = d_ff // LANES
    nj = pl.cdiv(d_ff, ff_tile)
    nb = t // tm
    rc = _finish_rows(tm, nj)
    nc = tm // rc
    nxt_layer, nxt_row = nxt if nxt is not None else (layer, None)
    n_head = None
    if head_rows is not None:
        assert nxt is None and head_rows % rc == 0 and (t - head_rows) % rc == 0
        n_head = head_rows // rc
    kern = functools.partial(_ffn_kernel, d_ff=d_ff, ff_tile=ff_tile, nb=nb, nc=nc, post=4 * which + 1,
                             nxt=nxt_row, n_head_chunks=n_head)
    jw = lambda i, j: jnp.where(i < nb, j, nj - 1)
    w_in_spec = lambda width, col: pl.BlockSpec(
        (None, None, d, width), lambda i, j: (layer, which, 0, col(jw(i, j))))
    w_out_spec = lambda row: pl.BlockSpec(
        (None, None, LANES, d), lambda i, j: (layer, which, row(jw(i, j)), 0))
    up_piece = lambda q: (lambda j: jnp.minimum(n_pieces + pieces * j + q, 2 * n_pieces - 1))
    down_piece = lambda q: (lambda j: jnp.minimum(pieces * j + q, n_pieces - 1))
    chunk = lambda i, j: (jnp.maximum((i - 1) * nc + jnp.minimum(j, nc - 1), 0), 0)
    g_spec = lambda l: pl.BlockSpec((None,) + norm_g.shape[1:], lambda i, j: (l, 0, 0))
    if nxt is not None:
        out_specs = [pl.BlockSpec((rc, d), chunk), pl.BlockSpec((rc, d), chunk)]
        out_shape = [jax.ShapeDtypeStruct((t, d), F32), jax.ShapeDtypeStruct((t, d), BF16)]
    elif n_head is not None:
        out_specs = [pl.BlockSpec((rc, d), lambda i, j: (jnp.minimum(chunk(i, j)[0], n_head - 1), 0)),
                     pl.BlockSpec((rc, d), lambda i, j: (jnp.maximum(chunk(i, j)[0] - n_head, 0), 0))]
        out_shape = [jax.ShapeDtypeStruct((head_rows, d), F32), jax.ShapeDtypeStruct((t - head_rows, d), F32)]
    else:
        out_specs = [pl.BlockSpec((rc, d), chunk)]
        out_shape = [jax.ShapeDtypeStruct((t, d), F32)]
    res = pl.pallas_call(
        kern,
        grid=(nb + 1, nj),
        in_specs=[
            pl.BlockSpec((tm, d), lambda i, j: (jnp.minimum(i, nb - 1), 0)),
            pl.BlockSpec((rc, d), chunk),
            g_spec(layer),
            g_spec(nxt_layer),
            w_in_spec(ff_tile, lambda j: j),
            *[w_in_spec(LANES, up_piece(q)) for q in range(pieces)],
            *[w_out_spec(down_piece(q)) for q in range(pieces)],
        ],
        out_specs=out_specs,
        out_shape=out_shape,
        scratch_shapes=[pltpu.VMEM((2, tm, d), F32)],
        compiler_params=_params("arbitrary", "arbitrary"),
        name="ffn",
    )(h, x, norm_g, norm_g, w_in, *([w_in] * pieces), *([w_out] * pieces))
    return (res[0], res[1]) if len(res) == 2 else (res[0], None)


FFN1_PRE, MIX_PRE, MIX_POST, FFN2_PRE = 0, 2, 3, 4


def _proj_kernel(h_ref, w_ref, o_ref):
    o_ref[...] = jnp.dot(h_ref[...], w_ref[...].astype(BF16), preferred_element_type=F32)


def _proj(h, w, widx, tm):
    t, d = h.shape
    n = w.shape[2]
    tn = _divisor_tile(n, 512, LANES)
    return pl.pallas_call(
        _proj_kernel,
        grid=(t // tm, n // tn),
        in_specs=[
            pl.BlockSpec((tm, d), lambda i, j: (i, 0)),
            pl.BlockSpec((None, d, tn), lambda i, j: (widx, 0, j)),
        ],
        out_specs=pl.BlockSpec((tm, tn), lambda i, j: (i, j)),
        out_shape=jax.ShapeDtypeStruct((t, n), F32),
        compiler_params=_params("parallel", "arbitrary"),
        name="proj",
    )(h, w)


def _column_tiles_kernel(w_ref, o_ref):
    o_ref[...] = w_ref[...].astype(BF16)


def _column_tiles_bf16(w, tn):
    n, kdim, d = w.shape
    return pl.pallas_call(
        _column_tiles_kernel,
        grid=(n, d // tn),
        in_specs=[pl.BlockSpec((None, kdim, tn), lambda a, j: (a, 0, j))],
        out_specs=pl.BlockSpec((None, None, kdim, tn), lambda a, j: (a, j, 0, 0)),
        out_shape=jax.ShapeDtypeStruct((n, d // tn, kdim, tn), BF16),
        compiler_params=_params("parallel", "parallel"),
        name="column_tiles_bf16",
    )(w)


def _out_kernel(m_ref, w_ref, xr_ref, g_ref, xo_ref, ho_ref, acc_ref, *, nb, nc):
    i, j = pl.program_id(0), pl.program_id(1)
    ns, tm, tn = acc_ref.shape[1:]
    rc = xr_ref.shape[0]
    cur = lax.rem(i, 2)
    prv = 1 - cur

    @pl.when((i == 0) & (j == 0))
    def _():
        acc_ref[1] = jnp.zeros(acc_ref.shape[1:], F32)

    def tile():
        acc_ref[cur, j] = jnp.dot(m_ref[...], w_ref[j], preferred_element_type=F32)

    def finish_chunk():
        sub = _divisor_tile(rc, 176, 16)
        for r in range(rc // sub):
            rows = pl.ds(pl.multiple_of(j * rc + r * sub, 16), sub)
            o = jnp.concatenate([acc_ref[prv, s, rows, :] for s in range(ns)], axis=1)
            y = xr_ref[r * sub:(r + 1) * sub, :] + _rms(o, g_ref[MIX_POST:MIX_POST + 1, :])
            xo_ref[r * sub:(r + 1) * sub, :] = y
            ho_ref[r * sub:(r + 1) * sub, :] = _rms(y, g_ref[FFN2_PRE:FFN2_PRE + 1, :]).astype(BF16)

    real = i < nb

    @pl.when(real & (j < nc))
    def _():
        finish_chunk()
        tile()

    @pl.when(real & (j >= nc))
    def _():
        tile()

    @pl.when(jnp.logical_not(real) & (j < nc))
    def _():
        finish_chunk()


def _out(m, w, widx, x, norm_g, layer, tm):
    t, d = x.shape
    ns, kdim, tn = w.shape[1:]
    assert ns * tn == d
    nb = t // tm
    rc = min(c for c in range(16, tm + 1, 16) if tm % c == 0 and tm // c <= ns)
    nc = tm // rc
    chunk = pl.BlockSpec((rc, d), lambda i, j: (jnp.maximum((i - 1) * nc + jnp.minimum(j, nc - 1), 0), 0))
    return pl.pallas_call(
        functools.partial(_out_kernel, nb=nb, nc=nc),
        grid=(nb + 1, ns),
        in_specs=[
            pl.BlockSpec((tm, kdim), lambda i, j: (jnp.minimum(i, nb - 1), 0)),
            pl.BlockSpec((None, ns, kdim, tn), lambda i, j: (widx, 0, 0, 0), pipeline_mode=pl.Buffered(1)),
            chunk,
            pl.BlockSpec((None,) + norm_g.shape[1:], lambda i, j: (layer, 0, 0)),
        ],
        out_specs=[chunk, chunk],
        out_shape=[jax.ShapeDtypeStruct((t, d), F32), jax.ShapeDtypeStruct((t, d), BF16)],
        scratch_shapes=[pltpu.VMEM((2, ns, tm, tn), F32)],
        compiler_params=_params("arbitrary", "arbitrary"),
        name="outproj",
    )(m, w, x, norm_g)


def _conv3(ext, cw):
    return cw[0:1] * pltpu.roll(ext, 2, 0) + cw[1:2] * pltpu.roll(ext, 1, 0) + cw[2:3] * ext


def _window_sum(ext, w):
    s, k = ext, 1
    while k < w:
        s = s + pltpu.roll(s, k, 0)
        k *= 2
    return s


def _pool_group(win, cnt, ug, pw, scale):
    dlt = win / cnt - ug
    return jnp.dot(dlt.astype(BF16), pw.astype(BF16), preferred_element_type=F32) * scale


def _mix_prompt_kernel(hc_ref, gc_ref, gb_ref, u_ref, hch_ref, gch_ref, uh_ref, cw_ref, pw_ref,
                       ps_ref, m_ref, cst_ref, pst_ref, *, chunks_per_seq, n_chunks):
    rows, c = hc_ref.shape
    gw = c // len(POOL_WINDOWS)
    ci = pl.program_id(0) % chunks_per_seq
    first = ci == 0

    @pl.when(pl.program_id(0) >= n_chunks)
    def _():
        m_ref[...] = jnp.zeros(m_ref.shape, BF16)

    @pl.when(pl.program_id(0) < n_chunks)
    def _():
        v = gc_ref[...] * hc_ref[...]
        v_halo = jnp.where(first, 0.0, gch_ref[...] * hch_ref[...])
        y = _conv3(jnp.concatenate([v_halo, v], axis=0), cw_ref[...])[CONV_HALO:]
        m_ref[:, :c] = (gb_ref[...] * y).astype(BF16)
        cst_ref[...] = v[rows - CONV_HALO:]

        u = u_ref[...]
        u_ext = jnp.concatenate([jnp.where(first, 0.0, uh_ref[...]), u], axis=0)
        pst_ref[...] = u_ext[rows:]
        pos1 = ci * rows + lax.broadcasted_iota(jnp.int32, (rows, 1), 0) + 1
        for gi, w in enumerate(POOL_WINDOWS):
            sl = slice(gi * gw, (gi + 1) * gw)
            win = _window_sum(u_ext[:, sl], w)[POOL_HALO:]
            cnt = jnp.minimum(w, pos1).astype(F32)
            yp = _pool_group(win, cnt, u[:, sl], pw_ref[gi], ps_ref[:, sl])
            m_ref[:, c + gi * gw:c + (gi + 1) * gw] = yp.astype(BF16)


def _mix_prompt(z, conv_w, pool_w, pool_scale, n_seq, seq, t_total):
    c = conv_w.shape[1]
    assert z.shape[1] == 4 * c
    t_tail = t_total - n_seq * seq
    rows = max(r for r in range(POOL_HALO, 257, POOL_HALO) if seq % r == 0 and t_tail % r == 0)
    cps = seq // rows
    n_chunks = n_seq * cps
    rh_c, rh_p = rows // CONV_HALO, rows // POOL_HALO

    def halo(ratio, col):
        return lambda i: (jnp.maximum(i * ratio - 1, 0), col)

    kern = functools.partial(_mix_prompt_kernel, chunks_per_seq=cps, n_chunks=n_chunks)
    gw = c // len(POOL_WINDOWS)
    state = lambda i: (jnp.minimum(i // cps, n_seq - 1), 0, 0)
    return pl.pallas_call(
        kern,
        grid=(n_chunks + t_tail // rows,),
        in_specs=[
            pl.BlockSpec((rows, c), lambda i: (i, 0)),
            pl.BlockSpec((rows, c), lambda i: (i, 1)),
            pl.BlockSpec((rows, c), lambda i: (i, 2)),
            pl.BlockSpec((rows, c), lambda i: (i, 3)),
            pl.BlockSpec((CONV_HALO, c), halo(rh_c, 0)),
            pl.BlockSpec((CONV_HALO, c), halo(rh_c, 1)),
            pl.BlockSpec((POOL_HALO, c), halo(rh_p, 3)),
            pl.BlockSpec((CONV_K, c), lambda i: (0, 0)),
            pl.BlockSpec((len(POOL_WINDOWS), gw, gw), lambda i: (0, 0, 0)),
            pl.BlockSpec((1, c), lambda i: (0, 0)),
        ],
        out_specs=[
            pl.BlockSpec((rows, 2 * c), lambda i: (i, 0)),
            pl.BlockSpec((None, CONV_HALO, c), state),
            pl.BlockSpec((None, POOL_HALO, c), state),
        ],
        out_shape=[
            jax.ShapeDtypeStruct((t_total, 2 * c), BF16),
            jax.ShapeDtypeStruct((n_seq, CONV_HALO, c), F32),
            jax.ShapeDtypeStruct((n_seq, POOL_HALO, c), F32),
        ],
        compiler_params=_params("arbitrary"),
        name="mix_prompt",
    )(z, z, z, z, z, z, z, conv_w, pool_w, pool_scale)


def _mix_sample_kernel(hc_ref, gc_ref, gb_ref, u_ref, cst_in_ref, pst_in_ref, cw_ref, pw_ref,
                       ps_ref, m_in_ref, m_ref, cst_ref, pst_ref, *, seq, past_len):
    del m_in_ref
    rows, c = hc_ref.shape
    n_seq = rows // seq
    gw = c // len(POOL_WINDOWS)

    def with_halo(halo3, x2):
        ext = jnp.concatenate([halo3, x2.reshape(n_seq, seq, c)], axis=1)
        return ext, ext.reshape(n_seq * ext.shape[1], c)

    def body_rows(flat, n_halo):
        return flat.reshape(n_seq, n_halo + seq, flat.shape[-1])[:, n_halo:].reshape(rows, flat.shape[-1])

    v = gc_ref[...] * hc_ref[...]
    v_ext3, v_ext = with_halo(cst_in_ref[...], v)
    y = body_rows(_conv3(v_ext, cw_ref[...]), CONV_HALO)
    m_ref[:, :c] = (gb_ref[...] * y).astype(BF16)
    cst_ref[...] = v_ext3[:, seq:]

    u = u_ref[...]
    u_ext3, u_ext = with_halo(pst_in_ref[...], u)
    pst_ref[...] = u_ext3[:, seq:]
    pos1 = past_len + jnp.bitwise_and(lax.broadcasted_iota(jnp.int32, (rows, 1), 0), seq - 1) + 1
    for gi, w in enumerate(POOL_WINDOWS):
        sl = slice(gi * gw, (gi + 1) * gw)
        win = body_rows(_window_sum(u_ext[:, sl], w), POOL_HALO)
        cnt = jnp.minimum(w, pos1).astype(F32)
        yp = _pool_group(win, cnt, u[:, sl], pw_ref[gi], ps_ref[:, sl])
        m_ref[:, c + gi * gw:c + (gi + 1) * gw] = yp.astype(BF16)


def _mix_sample(z, m, conv_state, pool_state, conv_w, pool_w, pool_scale, t_prompt, seq, past_len):
    c = conv_w.shape[1]
    n_seq = conv_state.shape[0]
    rows = n_seq * seq
    assert seq == SUBLANES and t_prompt % rows == 0
    rb = t_prompt // rows
    gw = c // len(POOL_WINDOWS)
    full3 = lambda i: (0, 0, 0)
    kern = functools.partial(_mix_sample_kernel, seq=seq, past_len=past_len)
    return pl.pallas_call(
        kern,
        grid=(1,),
        in_specs=[
            pl.BlockSpec((rows, c), lambda i: (rb, 0)),
            pl.BlockSpec((rows, c), lambda i: (rb, 1)),
            pl.BlockSpec((rows, c), lambda i: (rb, 2)),
            pl.BlockSpec((rows, c), lambda i: (rb, 3)),
            pl.BlockSpec((n_seq, CONV_HALO, c), full3),
            pl.BlockSpec((n_seq, POOL_HALO, c), full3),
            pl.BlockSpec((CONV_K, c), lambda i: (0, 0)),
            pl.BlockSpec((len(POOL_WINDOWS), gw, gw), full3),
            pl.BlockSpec((1, c), lambda i: (0, 0)),
            pl.BlockSpec(memory_space=pl.ANY),
        ],
        out_specs=[
            pl.BlockSpec((rows, 2 * c), lambda i: (rb, 0)),
            pl.BlockSpec((n_seq, CONV_HALO, c), full3),
            pl.BlockSpec((n_seq, POOL_HALO, c), full3),
        ],
        out_shape=[
            jax.ShapeDtypeStruct(m.shape, m.dtype),
            jax.ShapeDtypeStruct((n_seq, CONV_HALO, c), F32),
            jax.ShapeDtypeStruct((n_seq, POOL_HALO, c), F32),
        ],
        input_output_aliases={9: 0},
        compiler_params=_params("arbitrary"),
        name="mix_sample",
    )(z, z, z, z, conv_state, pool_state, conv_w, pool_w, pool_scale, m)


N_KEYS = 2 * BLOCK


def _t5_bucket(dist):
    max_exact = N_BUCKETS // 2
    n = jnp.maximum(dist, 0)
    ratio = jnp.log(jnp.maximum(n, 1).astype(F32) / max_exact) / math.log(MAX_DISTANCE / max_exact)
    large = jnp.minimum(max_exact + (ratio * (N_BUCKETS - max_exact)).astype(jnp.int32), N_BUCKETS - 1)
    return jnp.where(n < max_exact, n, large)


def _bias_prompt_kernel(rb_ref, o_ref):
    heads, n_keys, n_q = o_ref.shape
    s = lax.broadcasted_iota(jnp.int32, (n_keys, n_q), 0)
    q = lax.broadcasted_iota(jnp.int32, (n_keys, n_q), 1)
    bucket = _t5_bucket(BLOCK + q - s)
    for hh in range(heads):
        head = pl.program_id(0) * heads + hh
        acc = jnp.zeros((n_keys, n_q), F32)
        for b in range(N_BUCKETS):
            acc = jnp.where(bucket == b, rb_ref[b, head], acc)
        o_ref[hh] = acc


def _bias_sample_kernel(rb_ref, o_ref, *, seq, n_heads):
    half = pl.program_id(0)
    s = lax.broadcasted_iota(jnp.int32, o_ref.shape, 0)
    lane = lax.broadcasted_iota(jnp.int32, o_ref.shape, 1)
    bucket = _t5_bucket(BLOCK + jnp.bitwise_and(lane, seq - 1) - s)
    lane1 = lax.broadcasted_iota(jnp.int32, (1, LANES), 1)
    acc = jnp.zeros(o_ref.shape, F32)
    for b in range(N_BUCKETS):
        vec = jnp.zeros((1, LANES), F32)
        for slab in range(n_heads // 2):
            vec = jnp.where((lane1 >= slab * seq) & (lane1 < (slab + 1) * seq), rb_ref[b, 2 * slab + half], vec)
        acc = jnp.where(bucket == b, vec, acc)
    o_ref[...] = acc


def _bias_tables(rel_bias, seq_s):
    n_heads = rel_bias.shape[1]
    assert seq_s == SUBLANES and (n_heads // 2) * seq_s <= LANES
    heads_per_step = math.gcd(n_heads, 8)
    prompt = pl.pallas_call(
        _bias_prompt_kernel,
        grid=(n_heads // heads_per_step,),
        in_specs=[pl.BlockSpec(memory_space=pltpu.SMEM)],
        out_specs=pl.BlockSpec((heads_per_step, N_KEYS, BLOCK), lambda h: (h, 0, 0)),
        out_shape=jax.ShapeDtypeStruct((n_heads, N_KEYS, BLOCK), F32),
        compiler_params=_params("arbitrary"),
        name="rel_bias_prompt",
    )(rel_bias)
    sample = pl.pallas_call(
        functools.partial(_bias_sample_kernel, seq=seq_s, n_heads=n_heads),
        grid=(2,),
        in_specs=[pl.BlockSpec(memory_space=pltpu.SMEM)],
        out_specs=pl.BlockSpec((None, N_KEYS, LANES), lambda h: (h, 0, 0)),
        out_shape=jax.ShapeDtypeStruct((2, N_KEYS, LANES), F32),
        compiler_params=_params("arbitrary"),
        name="rel_bias_sample",
    )(rel_bias)
    return prompt, sample


def _stage_kv(k_all, v_all, kpad_ref, vtpad_ref):
    n_keys, kvw = k_all.shape
    lane = lax.broadcasted_iota(jnp.int32, (n_keys, LANES), 1)
    row = lax.broadcasted_iota(jnp.int32, (LANES, n_keys), 0)
    for slab in range(kvw // LANES):
        ks = k_all[:, slab * LANES:(slab + 1) * LANES]
        vt = v_all[:, slab * LANES:(slab + 1) * LANES].T
        for own in range(2):
            kv = 2 * slab + own
            k_own = jnp.where((lane >= own * HEAD_DIM) & (lane < (own + 1) * HEAD_DIM), ks, 0.0)
            v_own = jnp.where((row >= own * HEAD_DIM) & (row < (own + 1) * HEAD_DIM), vt, 0.0)
            kpad_ref[kv, own] = k_own.astype(BF16)
            kpad_ref[kv, 1 - own] = pltpu.roll(k_own, HEAD_DIM, 1).astype(BF16)
            vtpad_ref[kv, own] = v_own.astype(BF16)
            vtpad_ref[kv, 1 - own] = pltpu.roll(v_own, HEAD_DIM, 0).astype(BF16)


_TRANS_B = (((1,), (1,)), ((), ()))


def _attn_prompt_kernel(*refs, blocks_per_seq, n_blocks):
    step = pl.program_id(0)

    @pl.when(step >= n_blocks)
    def _():
        o_ref = refs[7]
        o_ref[...] = jnp.zeros(o_ref.shape, o_ref.dtype)

    @pl.when(step < n_blocks)
    def _():
        _attn_prompt_block(lax.rem(step, blocks_per_seq) == 0, *refs)


def _attn_prompt_block(first, sinks_ref, q_ref, kc_ref, kp_ref, vc_ref, vp_ref, bias_ref, o_ref,
                       kpad_ref, vtpad_ref, s_ref, e_ref, m_ref, den_ref):
    n_heads = bias_ref.shape[0]
    group = n_heads // kpad_ref.shape[0]
    _stage_kv(jnp.concatenate([kp_ref[...], kc_ref[...]], axis=0),
              jnp.concatenate([vp_ref[...], vc_ref[...]], axis=0), kpad_ref, vtpad_ref)
    key = lax.broadcasted_iota(jnp.int32, (N_KEYS, BLOCK), 0)
    qry = lax.broadcasted_iota(jnp.int32, (N_KEYS, BLOCK), 1)
    valid = (key >= qry) & (key <= qry + WINDOW) & ((key >= BLOCK) | jnp.logical_not(first))

    for p in range(n_heads // 2):
        kv = 2 * p // group
        own = kv % 2
        q_slab = q_ref[:, p * LANES:(p + 1) * LANES] * HEAD_DIM ** -0.5
        q_roll = pltpu.roll(q_slab, HEAD_DIM, 1)
        q_pair = jnp.concatenate([q_roll, q_slab] if own else [q_slab, q_roll], axis=0).astype(BF16)
        s_pair = lax.dot_general(kpad_ref[kv, own], q_pair, _TRANS_B, preferred_element_type=F32)
        for half in range(2):
            head = 2 * p + half
            s = s_pair[:, half * BLOCK:(half + 1) * BLOCK]
            s = jnp.where(valid, s + bias_ref[head], NEG)
            s_ref[head] = s
            m_ref[head:head + 1, :] = jnp.maximum(jnp.max(s, axis=0, keepdims=True), sinks_ref[head])
    for head in range(n_heads):
        m = m_ref[head:head + 1, :]
        e = jnp.exp(s_ref[head] - m)
        den_ref[head:head + 1, :] = jnp.sum(e, axis=0, keepdims=True) + jnp.exp(sinks_ref[head] - m)
        e_ref[head] = e.astype(BF16)
    first_head = lax.broadcasted_iota(jnp.int32, (LANES, BLOCK), 0) < HEAD_DIM
    for p in range(n_heads // 2):
        kv = 2 * p // group
        acc = (jnp.dot(vtpad_ref[kv, 0], e_ref[2 * p], preferred_element_type=F32)
               + jnp.dot(vtpad_ref[kv, 1], e_ref[2 * p + 1], preferred_element_type=F32))
        inv = jnp.where(first_head, 1.0 / den_ref[2 * p:2 * p + 1, :], 1.0 / den_ref[2 * p + 1:2 * p + 2, :])
        o_ref[:, p * LANES:(p + 1) * LANES] = (acc * inv).T.astype(o_ref.dtype)


def _attn_prompt(qkv, bias, sinks, n_seq, seq, n_heads, n_kv, t_total):
    d = n_heads * HEAD_DIM
    kvw = n_kv * HEAD_DIM
    assert kvw % LANES == 0 and d % kvw == 0 and seq % BLOCK == 0
    nb = seq // BLOCK
    kcol, vcol = d // kvw, d // kvw + 1
    n_blocks = n_seq * nb
    n_tail = pl.cdiv(t_total - n_seq * seq, BLOCK)
    cur = lambda col: (lambda s: (s, col))
    prev = lambda col: (lambda s: (jnp.maximum(s - 1, 0), col))
    return pl.pallas_call(
        functools.partial(_attn_prompt_kernel, blocks_per_seq=nb, n_blocks=n_blocks),
        grid=(n_blocks + n_tail,),
        in_specs=[
            pl.BlockSpec(memory_space=pltpu.SMEM),
            pl.BlockSpec((BLOCK, d), cur(0)),
            pl.BlockSpec((BLOCK, kvw), cur(kcol)),
            pl.BlockSpec((BLOCK, kvw), prev(kcol)),
            pl.BlockSpec((BLOCK, kvw), cur(vcol)),
            pl.BlockSpec((BLOCK, kvw), prev(vcol)),
            pl.BlockSpec((n_heads, N_KEYS, BLOCK), lambda s: (0, 0, 0), pipeline_mode=pl.Buffered(1)),
        ],
        out_specs=pl.BlockSpec((BLOCK, d), cur(0)),
        out_shape=jax.ShapeDtypeStruct((t_total, d), BF16),
        scratch_shapes=[
            pltpu.VMEM((n_kv, 2, N_KEYS, LANES), BF16),
            pltpu.VMEM((n_kv, 2, LANES, N_KEYS), BF16),
            pltpu.VMEM((n_heads, N_KEYS, BLOCK), F32),
            pltpu.VMEM((n_heads, N_KEYS, BLOCK), BF16),
            pltpu.VMEM((n_heads, BLOCK), F32),
            pltpu.VMEM((n_heads, BLOCK), F32),
        ],
        compiler_params=_params("arbitrary"),
        name="attn_prompt",
    )(sinks, qkv, qkv, qkv, qkv, qkv, bias)


def _attn_sample_kernel(sinks_ref, q_ref, kn_ref, vn_ref, kc_ref, vc_ref, bias_ref, att_in_ref, o_ref,
                        kpad_ref, vtpad_ref):
    del att_in_ref
    seq, kvw = kn_ref.shape
    n_pairs = q_ref.shape[1] // LANES
    n_kv = kvw // HEAD_DIM
    kv_lanes = n_pairs // n_kv * seq
    pad = jnp.zeros((N_KEYS - kc_ref.shape[0] - seq, kvw), F32)
    _stage_kv(jnp.concatenate([kc_ref[...], kn_ref[...], pad], axis=0),
              jnp.concatenate([vc_ref[...], vn_ref[...], pad], axis=0), kpad_ref, vtpad_ref)

    rows = [q_ref[:, p * LANES:(p + 1) * LANES] for p in range(n_pairs)]
    if n_pairs * seq < LANES:
        rows.append(jnp.zeros((LANES - n_pairs * seq, LANES), F32))
    qs = (jnp.concatenate(rows, axis=0) * HEAD_DIM ** -0.5).astype(BF16)

    key = lax.broadcasted_iota(jnp.int32, (N_KEYS, LANES), 0)
    lane = lax.broadcasted_iota(jnp.int32, (N_KEYS, LANES), 1)
    lane1 = lax.broadcasted_iota(jnp.int32, (1, LANES), 1)
    qry = jnp.bitwise_and(lane, seq - 1)
    valid = (key >= qry) & (key <= qry + WINDOW)

    probs, dens = [], []
    for half in range(2):
        s = jnp.zeros((N_KEYS, LANES), F32)
        for kv in range(n_kv):
            s_kv = lax.dot_general(kpad_ref[kv, half], qs, _TRANS_B, preferred_element_type=F32)
            s = jnp.where((lane >= kv * kv_lanes) & (lane < (kv + 1) * kv_lanes), s_kv, s)
        sink = jnp.zeros((1, LANES), F32)
        for p in range(n_pairs):
            sink = jnp.where((lane1 >= p * seq) & (lane1 < (p + 1) * seq), sinks_ref[2 * p + half], sink)
        s = jnp.where(valid, s + bias_ref[half], NEG)
        m = jnp.maximum(jnp.max(s, axis=0, keepdims=True), sink)
        e = jnp.exp(s - m)
        dens.append(jnp.sum(e, axis=0, keepdims=True) + jnp.exp(sink - m))
        probs.append(e.astype(BF16))

    row_t = lax.broadcasted_iota(jnp.int32, (LANES, LANES), 0)
    lane_t = lax.broadcasted_iota(jnp.int32, (LANES, LANES), 1)
    out_t = jnp.zeros((LANES, LANES), F32)
    for kv in range(n_kv):
        acc = (jnp.dot(vtpad_ref[kv, 0], probs[0], preferred_element_type=F32)
               + jnp.dot(vtpad_ref[kv, 1], probs[1], preferred_element_type=F32))
        out_t = jnp.where((lane_t >= kv * kv_lanes) & (lane_t < (kv + 1) * kv_lanes), acc, out_t)
    inv = jnp.where(row_t < HEAD_DIM, 1.0 / dens[0], 1.0 / dens[1])
    out = (out_t * inv).T
    for p in range(n_pairs):
        o_ref[:, p * LANES:(p + 1) * LANES] = out[p * seq:(p + 1) * seq].astype(o_ref.dtype)


def _attn_sample(qkv, att, cache_k, cache_v, bias, sinks, t_prompt, seq, n_heads):
    n_seq, kv_buf, kvw = cache_k.shape
    d = n_heads * HEAD_DIM
    assert kv_buf == BLOCK and seq == SUBLANES and t_prompt % seq == 0
    rb = t_prompt // seq
    kcol, vcol = d // kvw, d // kvw + 1
    return pl.pallas_call(
        _attn_sample_kernel,
        grid=(n_seq,),
        in_specs=[
            pl.BlockSpec(memory_space=pltpu.SMEM),
            pl.BlockSpec((seq, d), lambda n: (rb + n, 0)),
            pl.BlockSpec((seq, kvw), lambda n: (rb + n, kcol)),
            pl.BlockSpec((seq, kvw), lambda n: (rb + n, vcol)),
            pl.BlockSpec((None, kv_buf, kvw), lambda n: (n, 0, 0)),
            pl.BlockSpec((None, kv_buf, kvw), lambda n: (n, 0, 0)),
            pl.BlockSpec((2, N_KEYS, LANES), lambda n: (0, 0, 0)),
            pl.BlockSpec(memory_space=pl.ANY),
        ],
        out_specs=pl.BlockSpec((seq, d), lambda n: (rb + n, 0)),
        out_shape=jax.ShapeDtypeStruct(att.shape, att.dtype),
        scratch_shapes=[
            pltpu.VMEM((kvw // HEAD_DIM, 2, N_KEYS, LANES), BF16),
            pltpu.VMEM((kvw // HEAD_DIM, 2, LANES, N_KEYS), BF16),
        ],
        input_output_aliases={7: 0},
        compiler_params=_params("arbitrary"),
        name="attn_sample",
    )(sinks, qkv, qkv, qkv, cache_k, cache_v, bias, att)


def kernel(x_prompt, x_sample, state_conv, state_pool, cache_k, cache_v, norm_g, w_ffn_in,
           w_ffn_out, w_mix_in, conv_w, pool_w, pool_scale, w_mix_out, w_qkv, w_o,
           attn_sinks, rel_bias):
    n_p, seq_p, d = x_prompt.shape
    n_s, seq_s, _ = x_sample.shape
    depth = norm_g.shape[0]
    n_heads = d // HEAD_DIM
    n_kv = cache_k.shape[3]
    kv_buf = cache_k.shape[2]
    kvw = n_kv * HEAD_DIM
    conv_ctx = state_conv.shape[2]
    pool_ctx = state_pool.shape[2]
    t_p, t_s = n_p * seq_p, n_s * seq_s
    t = t_p + t_s
    tm = _divisor_tile(t, ROW_TILE, 16)
    tm_proj = _divisor_tile(t, PROJ_ROW_TILE, 16)

    d_ff = w_ffn_out.shape[2]
    ffn_tiles = _ffn_tiling(t, d_ff)
    last_ffn_tiles = _ffn_tiling(t, d_ff, align_rows=(t_p, t_s))
    x, h = _merge(x_prompt.reshape(t_p, d), x_sample.reshape(t_s, d), norm_g, 0, FFN1_PRE)
    bias_p, bias_s = _bias_tables(rel_bias, seq_s)
    out_tile = _divisor_tile(d, 512, LANES)
    w_mix_out = _column_tiles_bf16(w_mix_out, out_tile)
    w_o = _column_tiles_bf16(w_o, out_tile)

    conv_p, pool_p, k_p, v_p, conv_s, pool_s, k_s, v_s = ([] for _ in range(8))
    for i in range(depth):
        x, h = _ffn(h, x, norm_g, w_ffn_in, w_ffn_out, i, 0, *ffn_tiles, nxt=(i, MIX_PRE))
        j = i // 2
        if i % 2 == 0:
            z = _proj(h, w_mix_in, j, tm_proj)
            m, cst, pst = _mix_prompt(z, conv_w[j], pool_w[j], pool_scale[j][None], n_p, seq_p, t)
            conv_p.append(cst[:, CONV_HALO - conv_ctx:])
            pool_p.append(pst[:, POOL_HALO - pool_ctx:])
            cst_in = jnp.pad(state_conv[j], ((0, 0), (CONV_HALO - conv_ctx, 0), (0, 0)))
            pst_in = jnp.pad(state_pool[j], ((0, 0), (POOL_HALO - pool_ctx, 0), (0, 0)))
            m, cst, pst = _mix_sample(z, m, cst_in, pst_in, conv_w[j], pool_w[j], pool_scale[j][None],
                                      t_p, seq_s, PAST_LEN)
            conv_s.append(cst[:, CONV_HALO - conv_ctx:])
            pool_s.append(pst[:, POOL_HALO - pool_ctx:])
            x, h = _out(m, w_mix_out, j, x, norm_g, i, tm)
        else:
            qkv = _proj(h, w_qkv, j, tm_proj)
            att = _attn_prompt(qkv, bias_p, attn_sinks[j], n_p, seq_p, n_heads, n_kv, t)
            ck = cache_k[j].reshape(n_s, kv_buf, kvw)
            cv = cache_v[j].reshape(n_s, kv_buf, kvw)
            att = _attn_sample(qkv, att, ck, cv, bias_s, attn_sinks[j], t_p, seq_s, n_heads)
            k_new, v_new = qkv[:, d:d + kvw], qkv[:, d + kvw:]
            kv_p = lambda a: a[:t_p].reshape(n_p, seq_p, n_kv, HEAD_DIM)[:, seq_p - kv_buf:]
            k_p.append(kv_p(k_new))
            v_p.append(kv_p(v_new))
            kv_s = lambda c, a: jnp.concatenate(
                [c, a[t_p:].reshape(n_s, seq_s, kvw)], axis=1)[:, seq_s:].reshape(n_s, kv_buf, n_kv, HEAD_DIM)
            k_s.append(kv_s(ck, k_new))
            v_s.append(kv_s(cv, v_new))
            x, h = _out(att, w_o, j, x, norm_g, i, tm)
        if i + 1 < depth:
            x, h = _ffn(h, x, norm_g, w_ffn_in, w_ffn_out, i, 1, *ffn_tiles, nxt=(i + 1, FFN1_PRE))
        else:
            y_p, y_s = _ffn(h, x, norm_g, w_ffn_in, w_ffn_out, i, 1, *last_ffn_tiles, head_rows=t_p)

    y_p = y_p.reshape(n_p, seq_p, d)
    y_s = y_s.reshape(n_s, seq_s, d)
    st = jnp.stack
    return (y_p, y_s, st(conv_p), st(pool_p), st(k_p), st(v_p), st(conv_s), st(pool_s), st(k_s), st(v_s))
```

```python
import functools
import math

import jax
import jax.numpy as jnp
from jax import lax
from jax.experimental import pallas as pl
from jax.experimental.pallas import tpu as pltpu

F32 = jnp.float32
BF16 = jnp.bfloat16

EPS = 1e-6
NEG = -1e30
HEAD_DIM = 64
WINDOW = 128
BLOCK = WINDOW
CONV_K = 3
POOL_WINDOWS = (2, 4, 8, 16)
N_BUCKETS = 32
MAX_DISTANCE = 128
PAST_LEN = 16384

LANES = 128
SUBLANES = 8
CONV_HALO = SUBLANES
POOL_HALO = 2 * SUBLANES
FFN_TILINGS = ((1408, 2 * LANES), (1056, 2 * LANES))
FINISH_ROWS_MAX = 64
VMEM_LIMIT_BYTES = 58 * 1024 * 1024
ROW_TILE = 1056
PROJ_ROW_TILE = 2112


def _divisor_tile(n, target, mult):
    best = None
    for t in range(mult, min(n, target) + 1, mult):
        if n % t == 0:
            best = t
    assert best is not None, (n, target, mult)
    return best


def _params(*sem):
    return pltpu.CompilerParams(dimension_semantics=sem, vmem_limit_bytes=VMEM_LIMIT_BYTES)


def _rms(xf, g):
    ms = jnp.mean(xf * xf, axis=-1, keepdims=True)
    return xf * lax.rsqrt(ms + EPS) * g


def _for_row_chunks(n_rows, body):
    rc = _divisor_tile(n_rows, 272, 16)

    def step(c, carry):
        body(pl.ds(pl.multiple_of(c * rc, rc), rc))
        return carry

    lax.fori_loop(0, n_rows // rc, step, 0)


def _merge_kernel(xp_ref, xs_ref, g_ref, x_ref, h_ref, *, n_prompt_blocks, row):
    x = jnp.where(pl.program_id(0) < n_prompt_blocks, xp_ref[...], xs_ref[...])
    x_ref[...] = x
    h_ref[...] = _rms(x, g_ref[row:row + 1, :]).astype(BF16)


def _merge(x_prompt, x_sample, norm_g, layer, row):
    (t_p, d), t_s = x_prompt.shape, x_sample.shape[0]
    assert t_p % t_s == 0 and t_s % 16 == 0
    nbp = t_p // t_s
    row_block = pl.BlockSpec((t_s, d), lambda i: (i, 0))
    return pl.pallas_call(
        functools.partial(_merge_kernel, n_prompt_blocks=nbp, row=row),
        grid=(nbp + 1,),
        in_specs=[
            pl.BlockSpec((t_s, d), lambda i: (jnp.minimum(i, nbp - 1), 0)),
            pl.BlockSpec((t_s, d), lambda i: (0, 0)),
            pl.BlockSpec((None,) + norm_g.shape[1:], lambda i: (layer, 0, 0)),
        ],
        out_specs=[row_block, row_block],
        out_shape=[jax.ShapeDtypeStruct((t_p + t_s, d), F32), jax.ShapeDtypeStruct((t_p + t_s, d), BF16)],
        compiler_params=_params("parallel"),
        name="merge_rows",
    )(x_prompt, x_sample, norm_g)


def _ffn_step(step, nb, nj):
    real = step < nb * nj
    return jnp.where(real, step // nj, nb), jnp.where(real, step % nj, step - nb * nj)


def _ffn_kernel(h_ref, xr_ref, g_ref, gn_ref, wg_ref, *rest, d_ff, ff_tile, nb, nc, post, nxt, n_head_chunks):
    assert nxt is None or n_head_chunks is None
    pieces = ff_tile // LANES
    wu_refs, wo_refs, rest = rest[:pieces], rest[pieces:2 * pieces], rest[2 * pieces:]
    xo_ref, rest = rest[0], rest[1:]
    o2_ref, acc_ref = rest if len(rest) == 2 else (None,) + rest
    i, j = _ffn_step(pl.program_id(0), nb, pl.cdiv(d_ff, ff_tile))
    tm, d = h_ref.shape
    rc = xr_ref.shape[0]
    cur = lax.rem(i, 2)
    prv = 1 - cur

    @pl.when(pl.program_id(0) == 0)
    def _():
        def zero(rows):
            acc_ref[0, rows, :] = jnp.zeros((rows.size, d), F32)
            acc_ref[1, rows, :] = jnp.zeros((rows.size, d), F32)
        _for_row_chunks(tm, zero)

    n_full = d_ff // ff_tile
    end_pieces = (d_ff - n_full * ff_tile) // LANES
    wc = _divisor_tile(d, 512, LANES)
    row_halves = [slice(0, tm // 2), slice(tm // 2, tm)]

    def swiglu(row_sets, w, wo):
        width = wo.shape[0]
        hidden = []
        for rows in row_sets:
            r = jnp.dot(h_ref[rows, :], w, preferred_element_type=F32)
            gate, up = r[:, :width], r[:, width:]
            hidden.append(((gate * jax.nn.sigmoid(gate)) * up).astype(BF16))
        for rows, a in zip(row_sets, hidden):
            for c in range(d // wc):
                acc_ref[cur, rows, c * wc:(c + 1) * wc] += jnp.dot(a, wo[:, c * wc:(c + 1) * wc],
                                                                   preferred_element_type=F32)

    def tile(n_pieces=pieces):
        w = jnp.concatenate([wg_ref[:, :n_pieces * LANES].astype(BF16)]
                            + [r[...].astype(BF16) for r in wu_refs[:n_pieces]], axis=1)
        wo = jnp.concatenate([r[...].astype(BF16) for r in wo_refs[:n_pieces]], axis=0)
        swiglu(row_halves, w, wo)

    def end_tile():
        tile(end_pieces)

    def finish_chunk(then=None):
        rows = pl.ds(pl.multiple_of(j * rc, rc), rc)
        y = xr_ref[...] + 0.5 * _rms(acc_ref[prv, rows, :], g_ref[post:post + 1, :])
        acc_ref[prv, rows, :] = jnp.zeros((rc, d), F32)
        if n_head_chunks is None:
            xo_ref[...] = y
            if nxt is not None:
                o2_ref[...] = _rms(y, gn_ref[nxt:nxt + 1, :]).astype(BF16)
        if then is not None:
            then()
        if n_head_chunks is not None:
            in_head = (i - 1) * nc + j < n_head_chunks

            @pl.when(in_head)
            def _():
                xo_ref[...] = y

            @pl.when(jnp.logical_not(in_head))
            def _():
                o2_ref[...] = y

    real = i < nb

    @pl.when(real & (j < nc) & (j < n_full))
    def _():
        finish_chunk(tile)

    @pl.when(real & (j >= nc) & (j < n_full))
    def _():
        tile()

    if end_pieces:
        @pl.when(real & (j == n_full))
        def _():
            if n_full < nc:
                finish_chunk(end_tile)
            else:
                end_tile()

    @pl.when(jnp.logical_not(real) & (j < nc))
    def _():
        finish_chunk()


def _finish_rows(tm, n_steps, align_rows=()):
    fits = [c for c in range(16, FINISH_ROWS_MAX + 1, 16)
            if tm % c == 0 and tm // c <= n_steps and all(r % c == 0 for r in align_rows)]
    return max(fits) if fits else None


def _ffn_tiling(t, d_ff, align_rows=()):
    for tm_target, ff_tile in FFN_TILINGS:
        tm = _divisor_tile(t, tm_target, 16)
        if _finish_rows(tm, pl.cdiv(d_ff, ff_tile), align_rows) is not None:
            return tm, ff_tile
    raise ValueError((t, d_ff, align_rows))


def _ffn(h, x, norm_g, w_in, w_out, layer, which, tm, ff_tile, nxt=None, head_rows=None):
    t, d = x.shape
    d_ff = w_out.shape[2]
    pieces = ff_tile // LANES
    assert ff_tile % LANES == 0 and d_ff % LANES == 0 and w_in.shape[2:] == (d, 2 * d_ff) and t % tm == 0
    n_pieces = d_ff // LANES
    nj = pl.cdiv(d_ff, ff_tile)
    nb = t // tm
    rc = _finish_rows(tm, nj, () if head_rows is None else (head_rows, t - head_rows))
    nc = tm // rc
    nxt_layer, nxt_row = nxt if nxt is not None else (layer, None)
    n_head = None
    if head_rows is not None:
        assert nxt is None
        n_head = head_rows // rc
    kern = functools.partial(_ffn_kernel, d_ff=d_ff, ff_tile=ff_tile, nb=nb, nc=nc, post=4 * which + 1,
                             nxt=nxt_row, n_head_chunks=n_head)
    spec = lambda shape, ij_map: pl.BlockSpec(shape, lambda step: ij_map(*_ffn_step(step, nb, nj)))
    jw = lambda i, j: jnp.where(i < nb, j, nj - 1)
    w_in_spec = lambda width, col: spec((None, None, d, width), lambda i, j: (layer, which, 0, col(jw(i, j))))
    w_out_spec = lambda row: spec((None, None, LANES, d), lambda i, j: (layer, which, row(jw(i, j)), 0))
    up_piece = lambda q: (lambda j: jnp.minimum(n_pieces + pieces * j + q, 2 * n_pieces - 1))
    down_piece = lambda q: (lambda j: jnp.minimum(pieces * j + q, n_pieces - 1))
    chunk = lambda i, j: (jnp.maximum((i - 1) * nc + jnp.minimum(j, nc - 1), 0), 0)
    g_spec = lambda l: spec((None,) + norm_g.shape[1:], lambda i, j: (l, 0, 0))
    if nxt is not None:
        out_specs = [spec((rc, d), chunk), spec((rc, d), chunk)]
        out_shape = [jax.ShapeDtypeStruct((t, d), F32), jax.ShapeDtypeStruct((t, d), BF16)]
    elif n_head is not None:
        out_specs = [spec((rc, d), lambda i, j: (jnp.minimum(chunk(i, j)[0], n_head - 1), 0)),
                     spec((rc, d), lambda i, j: (jnp.maximum(chunk(i, j)[0] - n_head, 0), 0))]
        out_shape = [jax.ShapeDtypeStruct((head_rows, d), F32), jax.ShapeDtypeStruct((t - head_rows, d), F32)]
    else:
        out_specs = [spec((rc, d), chunk)]
        out_shape = [jax.ShapeDtypeStruct((t, d), F32)]
    res = pl.pallas_call(
        kern,
        grid=(nb * nj + nc,),
        in_specs=[
            spec((tm, d), lambda i, j: (jnp.minimum(i, nb - 1), 0)),
            spec((rc, d), chunk),
            g_spec(layer),
            g_spec(nxt_layer),
            w_in_spec(ff_tile, lambda j: j),
            *[w_in_spec(LANES, up_piece(q)) for q in range(pieces)],
            *[w_out_spec(down_piece(q)) for q in range(pieces)],
        ],
        out_specs=out_specs,
        out_shape=out_shape,
        scratch_shapes=[pltpu.VMEM((2, tm, d), F32)],
        compiler_params=_params("arbitrary"),
        name="ffn",
    )(h, x, norm_g, norm_g, w_in, *([w_in] * pieces), *([w_out] * pieces))
    return (res[0], res[1]) if len(res) == 2 else (res[0], None)


FFN1_PRE, MIX_PRE, MIX_POST, FFN2_PRE = 0, 2, 3, 4


def _proj_kernel(h_ref, w_ref, o_ref):
    o_ref[...] = jnp.dot(h_ref[...], w_ref[...].astype(BF16), preferred_element_type=F32)


def _proj(h, w, widx, tm):
    t, d = h.shape
    n = w.shape[2]
    tn = _divisor_tile(n, 512, LANES)
    return pl.pallas_call(
        _proj_kernel,
        grid=(t // tm, n // tn),
        in_specs=[
            pl.BlockSpec((tm, d), lambda i, j: (i, 0)),
            pl.BlockSpec((None, d, tn), lambda i, j: (widx, 0, j)),
        ],
        out_specs=pl.BlockSpec((tm, tn), lambda i, j: (i, j)),
        out_shape=jax.ShapeDtypeStruct((t, n), F32),
        compiler_params=_params("parallel", "arbitrary"),
        name="proj",
    )(h, w)


def _column_tiles_kernel(w_ref, o_ref):
    o_ref[...] = w_ref[...].astype(BF16)


def _column_tiles_bf16(w, tn):
    n, kdim, d = w.shape
    return pl.pallas_call(
        _column_tiles_kernel,
        grid=(n, d // tn),
        in_specs=[pl.BlockSpec((None, kdim, tn), lambda a, j: (a, 0, j))],
        out_specs=pl.BlockSpec((None, None, kdim, tn), lambda a, j: (a, j, 0, 0)),
        out_shape=jax.ShapeDtypeStruct((n, d // tn, kdim, tn), BF16),
        compiler_params=_params("parallel", "parallel"),
        name="column_tiles_bf16",
    )(w)


def _out_kernel(m_ref, w_ref, xr_ref, g_ref, xo_ref, ho_ref, acc_ref, *, nb, nc):
    i, j = pl.program_id(0), pl.program_id(1)
    ns, tm, tn = acc_ref.shape[1:]
    rc = xr_ref.shape[0]
    cur = lax.rem(i, 2)
    prv = 1 - cur

    @pl.when((i == 0) & (j == 0))
    def _():
        acc_ref[1] = jnp.zeros(acc_ref.shape[1:], F32)

    def tile():
        acc_ref[cur, j] = jnp.dot(m_ref[...], w_ref[j], preferred_element_type=F32)

    def finish_chunk():
        sub = _divisor_tile(rc, 176, 16)
        for r in range(rc // sub):
            rows = pl.ds(pl.multiple_of(j * rc + r * sub, 16), sub)
            o = jnp.concatenate([acc_ref[prv, s, rows, :] for s in range(ns)], axis=1)
            y = xr_ref[r * sub:(r + 1) * sub, :] + _rms(o, g_ref[MIX_POST:MIX_POST + 1, :])
            xo_ref[r * sub:(r + 1) * sub, :] = y
            ho_ref[r * sub:(r + 1) * sub, :] = _rms(y, g_ref[FFN2_PRE:FFN2_PRE + 1, :]).astype(BF16)

    real = i < nb

    @pl.when(real & (j < nc))
    def _():
        finish_chunk()
        tile()

    @pl.when(real & (j >= nc))
    def _():
        tile()

    @pl.when(jnp.logical_not(real) & (j < nc))
    def _():
        finish_chunk()


def _out(m, w, widx, x, norm_g, layer, tm):
    t, d = x.shape
    ns, kdim, tn = w.shape[1:]
    assert ns * tn == d
    nb = t // tm
    rc = min(c for c in range(16, tm + 1, 16) if tm % c == 0 and tm // c <= ns)
    nc = tm // rc
    chunk = pl.BlockSpec((rc, d), lambda i, j: (jnp.maximum((i - 1) * nc + jnp.minimum(j, nc - 1), 0), 0))
    return pl.pallas_call(
        functools.partial(_out_kernel, nb=nb, nc=nc),
        grid=(nb + 1, ns),
        in_specs=[
            pl.BlockSpec((tm, kdim), lambda i, j: (jnp.minimum(i, nb - 1), 0)),
            pl.BlockSpec((None, ns, kdim, tn), lambda i, j: (widx, 0, 0, 0), pipeline_mode=pl.Buffered(1)),
            chunk,
            pl.BlockSpec((None,) + norm_g.shape[1:], lambda i, j: (layer, 0, 0)),
        ],
        out_specs=[chunk, chunk],
        out_shape=[jax.ShapeDtypeStruct((t, d), F32), jax.ShapeDtypeStruct((t, d), BF16)],
        scratch_shapes=[pltpu.VMEM((2, ns, tm, tn), F32)],
        compiler_params=_params("arbitrary", "arbitrary"),
        name="outproj",
    )(m, w, x, norm_g)


def _conv3(ext, cw):
    return cw[0:1] * pltpu.roll(ext, 2, 0) + cw[1:2] * pltpu.roll(ext, 1, 0) + cw[2:3] * ext


def _window_sum(ext, w):
    s, k = ext, 1
    while k < w:
        s = s + pltpu.roll(s, k, 0)
        k *= 2
    return s


def _pool_group(win, cnt, ug, pw, scale):
    dlt = win / cnt - ug
    return jnp.dot(dlt.astype(BF16), pw.astype(BF16), preferred_element_type=F32) * scale


def _mix_prompt_kernel(hc_ref, gc_ref, gb_ref, u_ref, hch_ref, gch_ref, uh_ref, cw_ref, pw_ref,
                       ps_ref, m_ref, cst_ref, pst_ref, *, chunks_per_seq, n_chunks):
    rows, c = hc_ref.shape
    gw = c // len(POOL_WINDOWS)
    ci = pl.program_id(0) % chunks_per_seq
    first = ci == 0

    @pl.when(pl.program_id(0) >= n_chunks)
    def _():
        m_ref[...] = jnp.zeros(m_ref.shape, BF16)

    @pl.when(pl.program_id(0) < n_chunks)
    def _():
        v = gc_ref[...] * hc_ref[...]
        v_halo = jnp.where(first, 0.0, gch_ref[...] * hch_ref[...])
        y = _conv3(jnp.concatenate([v_halo, v], axis=0), cw_ref[...])[CONV_HALO:]
        m_ref[:, :c] = (gb_ref[...] * y).astype(BF16)
        cst_ref[...] = v[rows - CONV_HALO:]

        u = u_ref[...]
        u_ext = jnp.concatenate([jnp.where(first, 0.0, uh_ref[...]), u], axis=0)
        pst_ref[...] = u_ext[rows:]
        pos1 = ci * rows + lax.broadcasted_iota(jnp.int32, (rows, 1), 0) + 1
        for gi, w in enumerate(POOL_WINDOWS):
            sl = slice(gi * gw, (gi + 1) * gw)
            win = _window_sum(u_ext[:, sl], w)[POOL_HALO:]
            cnt = jnp.minimum(w, pos1).astype(F32)
            yp = _pool_group(win, cnt, u[:, sl], pw_ref[gi], ps_ref[:, sl])
            m_ref[:, c + gi * gw:c + (gi + 1) * gw] = yp.astype(BF16)


def _mix_prompt(z, conv_w, pool_w, pool_scale, n_seq, seq, t_total):
    c = conv_w.shape[1]
    assert z.shape[1] == 4 * c
    t_tail = t_total - n_seq * seq
    rows = max(r for r in range(POOL_HALO, 257, POOL_HALO) if seq % r == 0 and t_tail % r == 0)
    cps = seq // rows
    n_chunks = n_seq * cps
    rh_c, rh_p = rows // CONV_HALO, rows // POOL_HALO

    def halo(ratio, col):
        return lambda i: (jnp.maximum(i * ratio - 1, 0), col)

    kern = functools.partial(_mix_prompt_kernel, chunks_per_seq=cps, n_chunks=n_chunks)
    gw = c // len(POOL_WINDOWS)
    state = lambda i: (jnp.minimum(i // cps, n_seq - 1), 0, 0)
    return pl.pallas_call(
        kern,
        grid=(n_chunks + t_tail // rows,),
        in_specs=[
            pl.BlockSpec((rows, c), lambda i: (i, 0)),
            pl.BlockSpec((rows, c), lambda i: (i, 1)),
            pl.BlockSpec((rows, c), lambda i: (i, 2)),
            pl.BlockSpec((rows, c), lambda i: (i, 3)),
            pl.BlockSpec((CONV_HALO, c), halo(rh_c, 0)),
            pl.BlockSpec((CONV_HALO, c), halo(rh_c, 1)),
            pl.BlockSpec((POOL_HALO, c), halo(rh_p, 3)),
            pl.BlockSpec((CONV_K, c), lambda i: (0, 0)),
            pl.BlockSpec((len(POOL_WINDOWS), gw, gw), lambda i: (0, 0, 0)),
            pl.BlockSpec((1, c), lambda i: (0, 0)),
        ],
        out_specs=[
            pl.BlockSpec((rows, 2 * c), lambda i: (i, 0)),
            pl.BlockSpec((None, CONV_HALO, c), state),
            pl.BlockSpec((None, POOL_HALO, c), state),
        ],
        out_shape=[
            jax.ShapeDtypeStruct((t_total, 2 * c), BF16),
            jax.ShapeDtypeStruct((n_seq, CONV_HALO, c), F32),
            jax.ShapeDtypeStruct((n_seq, POOL_HALO, c), F32),
        ],
        compiler_params=_params("arbitrary"),
        name="mix_prompt",
    )(z, z, z, z, z, z, z, conv_w, pool_w, pool_scale)


def _mix_sample_kernel(hc_ref, gc_ref, gb_ref, u_ref, cst_in_ref, pst_in_ref, cw_ref, pw_ref,
                       ps_ref, m_in_ref, m_ref, cst_ref, pst_ref, *, seq, past_len):
    del m_in_ref
    rows, c = hc_ref.shape
    n_seq = rows // seq
    gw = c // len(POOL_WINDOWS)

    def with_halo(halo3, x2):
        ext = jnp.concatenate([halo3, x2.reshape(n_seq, seq, c)], axis=1)
        return ext, ext.reshape(n_seq * ext.shape[1], c)

    def body_rows(flat, n_halo):
        return flat.reshape(n_seq, n_halo + seq, flat.shape[-1])[:, n_halo:].reshape(rows, flat.shape[-1])

    v = gc_ref[...] * hc_ref[...]
    v_ext3, v_ext = with_halo(cst_in_ref[...], v)
    y = body_rows(_conv3(v_ext, cw_ref[...]), CONV_HALO)
    m_ref[:, :c] = (gb_ref[...] * y).astype(BF16)
    cst_ref[...] = v_ext3[:, seq:]

    u = u_ref[...]
    u_ext3, u_ext = with_halo(pst_in_ref[...], u)
    pst_ref[...] = u_ext3[:, seq:]
    pos1 = past_len + jnp.bitwise_and(lax.broadcasted_iota(jnp.int32, (rows, 1), 0), seq - 1) + 1
    for gi, w in enumerate(POOL_WINDOWS):
        sl = slice(gi * gw, (gi + 1) * gw)
        win = body_rows(_window_sum(u_ext[:, sl], w), POOL_HALO)
        cnt = jnp.minimum(w, pos1).astype(F32)
        yp = _pool_group(win, cnt, u[:, sl], pw_ref[gi], ps_ref[:, sl])
        m_ref[:, c + gi * gw:c + (gi + 1) * gw] = yp.astype(BF16)


def _mix_sample(z, m, conv_state, pool_state, conv_w, pool_w, pool_scale, t_prompt, seq, past_len):
    c = conv_w.shape[1]
    n_seq = conv_state.shape[0]
    rows = n_seq * seq
    assert seq == SUBLANES and t_prompt % rows == 0
    rb = t_prompt // rows
    gw = c // len(POOL_WINDOWS)
    full3 = lambda i: (0, 0, 0)
    kern = functools.partial(_mix_sample_kernel, seq=seq, past_len=past_len)
    return pl.pallas_call(
        kern,
        grid=(1,),
        in_specs=[
            pl.BlockSpec((rows, c), lambda i: (rb, 0)),
            pl.BlockSpec((rows, c), lambda i: (rb, 1)),
            pl.BlockSpec((rows, c), lambda i: (rb, 2)),
            pl.BlockSpec((rows, c), lambda i: (rb, 3)),
            pl.BlockSpec((n_seq, CONV_HALO, c), full3),
            pl.BlockSpec((n_seq, POOL_HALO, c), full3),
            pl.BlockSpec((CONV_K, c), lambda i: (0, 0)),
            pl.BlockSpec((len(POOL_WINDOWS), gw, gw), full3),
            pl.BlockSpec((1, c), lambda i: (0, 0)),
            pl.BlockSpec(memory_space=pl.ANY),
        ],
        out_specs=[
            pl.BlockSpec((rows, 2 * c), lambda i: (rb, 0)),
            pl.BlockSpec((n_seq, CONV_HALO, c), full3),
            pl.BlockSpec((n_seq, POOL_HALO, c), full3),
        ],
        out_shape=[
            jax.ShapeDtypeStruct(m.shape, m.dtype),
            jax.ShapeDtypeStruct((n_seq, CONV_HALO, c), F32),
            jax.ShapeDtypeStruct((n_seq, POOL_HALO, c), F32),
        ],
        input_output_aliases={9: 0},
        compiler_params=_params("arbitrary"),
        name="mix_sample",
    )(z, z, z, z, conv_state, pool_state, conv_w, pool_w, pool_scale, m)


N_KEYS = 2 * BLOCK


def _t5_bucket(dist):
    max_exact = N_BUCKETS // 2
    n = jnp.maximum(dist, 0)
    ratio = jnp.log(jnp.maximum(n, 1).astype(F32) / max_exact) / math.log(MAX_DISTANCE / max_exact)
    large = jnp.minimum(max_exact + (ratio * (N_BUCKETS - max_exact)).astype(jnp.int32), N_BUCKETS - 1)
    return jnp.where(n < max_exact, n, large)


def _bias_prompt_kernel(rb_ref, o_ref):
    heads, n_keys, n_q = o_ref.shape
    s = lax.broadcasted_iota(jnp.int32, (n_keys, n_q), 0)
    q = lax.broadcasted_iota(jnp.int32, (n_keys, n_q), 1)
    bucket = _t5_bucket(BLOCK + q - s)
    for hh in range(heads):
        head = pl.program_id(0) * heads + hh
        acc = jnp.zeros((n_keys, n_q), F32)
        for b in range(N_BUCKETS):
            acc = jnp.where(bucket == b, rb_ref[b, head], acc)
        o_ref[hh] = acc


def _bias_sample_kernel(rb_ref, o_ref, *, seq, n_heads):
    half = pl.program_id(0)
    s = lax.broadcasted_iota(jnp.int32, o_ref.shape, 0)
    lane = lax.broadcasted_iota(jnp.int32, o_ref.shape, 1)
    bucket = _t5_bucket(BLOCK + jnp.bitwise_and(lane, seq - 1) - s)
    lane1 = lax.broadcasted_iota(jnp.int32, (1, LANES), 1)
    acc = jnp.zeros(o_ref.shape, F32)
    for b in range(N_BUCKETS):
        vec = jnp.zeros((1, LANES), F32)
        for slab in range(n_heads // 2):
            vec = jnp.where((lane1 >= slab * seq) & (lane1 < (slab + 1) * seq), rb_ref[b, 2 * slab + half], vec)
        acc = jnp.where(bucket == b, vec, acc)
    o_ref[...] = acc


def _bias_tables(rel_bias, seq_s):
    n_heads = rel_bias.shape[1]
    assert seq_s == SUBLANES and (n_heads // 2) * seq_s <= LANES
    heads_per_step = math.gcd(n_heads, 8)
    prompt = pl.pallas_call(
        _bias_prompt_kernel,
        grid=(n_heads // heads_per_step,),
        in_specs=[pl.BlockSpec(memory_space=pltpu.SMEM)],
        out_specs=pl.BlockSpec((heads_per_step, N_KEYS, BLOCK), lambda h: (h, 0, 0)),
        out_shape=jax.ShapeDtypeStruct((n_heads, N_KEYS, BLOCK), F32),
        compiler_params=_params("arbitrary"),
        name="rel_bias_prompt",
    )(rel_bias)
    sample = pl.pallas_call(
        functools.partial(_bias_sample_kernel, seq=seq_s, n_heads=n_heads),
        grid=(2,),
        in_specs=[pl.BlockSpec(memory_space=pltpu.SMEM)],
        out_specs=pl.BlockSpec((None, N_KEYS, LANES), lambda h: (h, 0, 0)),
        out_shape=jax.ShapeDtypeStruct((2, N_KEYS, LANES), F32),
        compiler_params=_params("arbitrary"),
        name="rel_bias_sample",
    )(rel_bias)
    return prompt, sample


def _stage_kv(k_all, v_all, kpad_ref, vtpad_ref):
    n_keys, kvw = k_all.shape
    lane = lax.broadcasted_iota(jnp.int32, (n_keys, LANES), 1)
    row = lax.broadcasted_iota(jnp.int32, (LANES, n_keys), 0)
    for slab in range(kvw // LANES):
        ks = k_all[:, slab * LANES:(slab + 1) * LANES]
        vt = v_all[:, slab * LANES:(slab + 1) * LANES].T
        for own in range(2):
            kv = 2 * slab + own
            k_own = jnp.where((lane >= own * HEAD_DIM) & (lane < (own + 1) * HEAD_DIM), ks, 0.0)
            v_own = jnp.where((row >= own * HEAD_DIM) & (row < (own + 1) * HEAD_DIM), vt, 0.0)
            kpad_ref[kv, own] = k_own.astype(BF16)
            kpad_ref[kv, 1 - own] = pltpu.roll(k_own, HEAD_DIM, 1).astype(BF16)
            vtpad_ref[kv, own] = v_own.astype(BF16)
            vtpad_ref[kv, 1 - own] = pltpu.roll(v_own, HEAD_DIM, 0).astype(BF16)


_TRANS_B = (((1,), (1,)), ((), ()))


def _attn_prompt_kernel(*refs, blocks_per_seq, n_blocks):
    step = pl.program_id(0)

    @pl.when(step >= n_blocks)
    def _():
        o_ref = refs[7]
        o_ref[...] = jnp.zeros(o_ref.shape, o_ref.dtype)

    @pl.when(step < n_blocks)
    def _():
        _attn_prompt_block(lax.rem(step, blocks_per_seq) == 0, *refs)


def _attn_prompt_block(first, sinks_ref, q_ref, kc_ref, kp_ref, vc_ref, vp_ref, bias_ref, o_ref,
                       kpad_ref, vtpad_ref, s_ref, e_ref, m_ref, den_ref):
    n_heads = bias_ref.shape[0]
    group = n_heads // kpad_ref.shape[0]
    _stage_kv(jnp.concatenate([kp_ref[...], kc_ref[...]], axis=0),
              jnp.concatenate([vp_ref[...], vc_ref[...]], axis=0), kpad_ref, vtpad_ref)
    key = lax.broadcasted_iota(jnp.int32, (N_KEYS, BLOCK), 0)
    qry = lax.broadcasted_iota(jnp.int32, (N_KEYS, BLOCK), 1)
    valid = (key >= qry) & (key <= qry + WINDOW) & ((key >= BLOCK) | jnp.logical_not(first))

    for p in range(n_heads // 2):
        q_slab = (q_ref[:, p * LANES:(p + 1) * LANES] * HEAD_DIM ** -0.5).astype(BF16)
        for half in range(2):
            head = 2 * p + half
            s = lax.dot_general(kpad_ref[head // group, half], q_slab, _TRANS_B, preferred_element_type=F32)
            s = jnp.where(valid, s + bias_ref[head], NEG)
            s_ref[head] = s
            m_ref[head:head + 1, :] = jnp.maximum(jnp.max(s, axis=0, keepdims=True), sinks_ref[head])
    for head in range(n_heads):
        m = m_ref[head:head + 1, :]
        e = jnp.exp(s_ref[head] - m)
        den_ref[head:head + 1, :] = jnp.sum(e, axis=0, keepdims=True) + jnp.exp(sinks_ref[head] - m)
        e_ref[head] = e.astype(BF16)
    first_head = lax.broadcasted_iota(jnp.int32, (LANES, BLOCK), 0) < HEAD_DIM
    for p in range(n_heads // 2):
        kv = 2 * p // group
        acc = (jnp.dot(vtpad_ref[kv, 0], e_ref[2 * p], preferred_element_type=F32)
               + jnp.dot(vtpad_ref[kv, 1], e_ref[2 * p + 1], preferred_element_type=F32))
        inv = jnp.where(first_head, 1.0 / den_ref[2 * p:2 * p + 1, :], 1.0 / den_ref[2 * p + 1:2 * p + 2, :])
        o_ref[:, p * LANES:(p + 1) * LANES] = (acc * inv).T.astype(o_ref.dtype)


def _attn_prompt(qkv, bias, sinks, n_seq, seq, n_heads, n_kv, t_total):
    d = n_heads * HEAD_DIM
    kvw = n_kv * HEAD_DIM
    assert kvw % LANES == 0 and d % kvw == 0 and seq % BLOCK == 0
    nb = seq // BLOCK
    kcol, vcol = d // kvw, d // kvw + 1
    n_blocks = n_seq * nb
    n_tail = pl.cdiv(t_total - n_seq * seq, BLOCK)
    cur = lambda col: (lambda s: (s, col))
    prev = lambda col: (lambda s: (jnp.maximum(s - 1, 0), col))
    return pl.pallas_call(
        functools.partial(_attn_prompt_kernel, blocks_per_seq=nb, n_blocks=n_blocks),
        grid=(n_blocks + n_tail,),
        in_specs=[
            pl.BlockSpec(memory_space=pltpu.SMEM),
            pl.BlockSpec((BLOCK, d), cur(0)),
            pl.BlockSpec((BLOCK, kvw), cur(kcol)),
            pl.BlockSpec((BLOCK, kvw), prev(kcol)),
            pl.BlockSpec((BLOCK, kvw), cur(vcol)),
            pl.BlockSpec((BLOCK, kvw), prev(vcol)),
            pl.BlockSpec((n_heads, N_KEYS, BLOCK), lambda s: (0, 0, 0), pipeline_mode=pl.Buffered(1)),
        ],
        out_specs=pl.BlockSpec((BLOCK, d), cur(0)),
        out_shape=jax.ShapeDtypeStruct((t_total, d), BF16),
        scratch_shapes=[
            pltpu.VMEM((n_kv, 2, N_KEYS, LANES), BF16),
            pltpu.VMEM((n_kv, 2, LANES, N_KEYS), BF16),
            pltpu.VMEM((n_heads, N_KEYS, BLOCK), F32),
            pltpu.VMEM((n_heads, N_KEYS, BLOCK), BF16),
            pltpu.VMEM((n_heads, BLOCK), F32),
            pltpu.VMEM((n_heads, BLOCK), F32),
        ],
        compiler_params=_params("arbitrary"),
        name="attn_prompt",
    )(sinks, qkv, qkv, qkv, qkv, qkv, bias)


def _attn_sample_kernel(sinks_ref, q_ref, kn_ref, vn_ref, kc_ref, vc_ref, bias_ref, att_in_ref, o_ref,
                        kpad_ref, vtpad_ref):
    del att_in_ref
    seq, kvw = kn_ref.shape
    n_pairs = q_ref.shape[1] // LANES
    n_kv = kvw // HEAD_DIM
    kv_lanes = n_pairs // n_kv * seq
    pad = jnp.zeros((N_KEYS - kc_ref.shape[0] - seq, kvw), F32)
    _stage_kv(jnp.concatenate([kc_ref[...], kn_ref[...], pad], axis=0),
              jnp.concatenate([vc_ref[...], vn_ref[...], pad], axis=0), kpad_ref, vtpad_ref)

    rows = [q_ref[:, p * LANES:(p + 1) * LANES] for p in range(n_pairs)]
    if n_pairs * seq < LANES:
        rows.append(jnp.zeros((LANES - n_pairs * seq, LANES), F32))
    qs = (jnp.concatenate(rows, axis=0) * HEAD_DIM ** -0.5).astype(BF16)

    key = lax.broadcasted_iota(jnp.int32, (N_KEYS, LANES), 0)
    lane = lax.broadcasted_iota(jnp.int32, (N_KEYS, LANES), 1)
    lane1 = lax.broadcasted_iota(jnp.int32, (1, LANES), 1)
    qry = jnp.bitwise_and(lane, seq - 1)
    valid = (key >= qry) & (key <= qry + WINDOW)

    probs, dens = [], []
    for half in range(2):
        s = jnp.zeros((N_KEYS, LANES), F32)
        for kv in range(n_kv):
            s_kv = lax.dot_general(kpad_ref[kv, half], qs, _TRANS_B, preferred_element_type=F32)
            s = jnp.where((lane >= kv * kv_lanes) & (lane < (kv + 1) * kv_lanes), s_kv, s)
        sink = jnp.zeros((1, LANES), F32)
        for p in range(n_pairs):
            sink = jnp.where((lane1 >= p * seq) & (lane1 < (p + 1) * seq), sinks_ref[2 * p + half], sink)
        s = jnp.where(valid, s + bias_ref[half], NEG)
        m = jnp.maximum(jnp.max(s, axis=0, keepdims=True), sink)
        e = jnp.exp(s - m)
        dens.append(jnp.sum(e, axis=0, keepdims=True) + jnp.exp(sink - m))
        probs.append(e.astype(BF16))

    row_t = lax.broadcasted_iota(jnp.int32, (LANES, LANES), 0)
    lane_t = lax.broadcasted_iota(jnp.int32, (LANES, LANES), 1)
    out_t = jnp.zeros((LANES, LANES), F32)
    for kv in range(n_kv):
        acc = (jnp.dot(vtpad_ref[kv, 0], probs[0], preferred_element_type=F32)
               + jnp.dot(vtpad_ref[kv, 1], probs[1], preferred_element_type=F32))
        out_t = jnp.where((lane_t >= kv * kv_lanes) & (lane_t < (kv + 1) * kv_lanes), acc, out_t)
    inv = jnp.where(row_t < HEAD_DIM, 1.0 / dens[0], 1.0 / dens[1])
    out = (out_t * inv).T
    for p in range(n_pairs):
        o_ref[:, p * LANES:(p + 1) * LANES] = out[p * seq:(p + 1) * seq].astype(o_ref.dtype)


def _attn_sample(qkv, att, cache_k, cache_v, bias, sinks, t_prompt, seq, n_heads):
    n_seq, kv_buf, kvw = cache_k.shape
    d = n_heads * HEAD_DIM
    assert kv_buf == BLOCK and seq == SUBLANES and t_prompt % seq == 0
    rb = t_prompt // seq
    kcol, vcol = d // kvw, d // kvw + 1
    return pl.pallas_call(
        _attn_sample_kernel,
        grid=(n_seq,),
        in_specs=[
            pl.BlockSpec(memory_space=pltpu.SMEM),
            pl.BlockSpec((seq, d), lambda n: (rb + n, 0)),
            pl.BlockSpec((seq, kvw), lambda n: (rb + n, kcol)),
            pl.BlockSpec((seq, kvw), lambda n: (rb + n, vcol)),
            pl.BlockSpec((None, kv_buf, kvw), lambda n: (n, 0, 0)),
            pl.BlockSpec((None, kv_buf, kvw), lambda n: (n, 0, 0)),
            pl.BlockSpec((2, N_KEYS, LANES), lambda n: (0, 0, 0)),
            pl.BlockSpec(memory_space=pl.ANY),
        ],
        out_specs=pl.BlockSpec((seq, d), lambda n: (rb + n, 0)),
        out_shape=jax.ShapeDtypeStruct(att.shape, att.dtype),
        scratch_shapes=[
            pltpu.VMEM((kvw // HEAD_DIM, 2, N_KEYS, LANES), BF16),
            pltpu.VMEM((kvw // HEAD_DIM, 2, LANES, N_KEYS), BF16),
        ],
        input_output_aliases={7: 0},
        compiler_params=_params("arbitrary"),
        name="attn_sample",
    )(sinks, qkv, qkv, qkv, cache_k, cache_v, bias, att)


def kernel(x_prompt, x_sample, state_conv, state_pool, cache_k, cache_v, norm_g, w_ffn_in,
           w_ffn_out, w_mix_in, conv_w, pool_w, pool_scale, w_mix_out, w_qkv, w_o,
           attn_sinks, rel_bias):
    n_p, seq_p, d = x_prompt.shape
    n_s, seq_s, _ = x_sample.shape
    depth = norm_g.shape[0]
    n_heads = d // HEAD_DIM
    n_kv = cache_k.shape[3]
    kv_buf = cache_k.shape[2]
    kvw = n_kv * HEAD_DIM
    conv_ctx = state_conv.shape[2]
    pool_ctx = state_pool.shape[2]
    t_p, t_s = n_p * seq_p, n_s * seq_s
    t = t_p + t_s
    tm = _divisor_tile(t, ROW_TILE, 16)
    tm_proj = _divisor_tile(t, PROJ_ROW_TILE, 16)

    d_ff = w_ffn_out.shape[2]
    ffn_tiles = _ffn_tiling(t, d_ff)
    last_ffn_tiles = _ffn_tiling(t, d_ff, align_rows=(t_p, t_s))
    x, h = _merge(x_prompt.reshape(t_p, d), x_sample.reshape(t_s, d), norm_g, 0, FFN1_PRE)
    bias_p, bias_s = _bias_tables(rel_bias, seq_s)
    out_tile = _divisor_tile(d, 512, LANES)
    w_mix_out = _column_tiles_bf16(w_mix_out, out_tile)
    w_o = _column_tiles_bf16(w_o, out_tile)

    conv_p, pool_p, k_p, v_p, conv_s, pool_s, k_s, v_s = ([] for _ in range(8))
    for i in range(depth):
        x, h = _ffn(h, x, norm_g, w_ffn_in, w_ffn_out, i, 0, *ffn_tiles, nxt=(i, MIX_PRE))
        j = i // 2
        if i % 2 == 0:
            z = _proj(h, w_mix_in, j, tm_proj)
            m, cst, pst = _mix_prompt(z, conv_w[j], pool_w[j], pool_scale[j][None], n_p, seq_p, t)
            conv_p.append(cst[:, CONV_HALO - conv_ctx:])
            pool_p.append(pst[:, POOL_HALO - pool_ctx:])
            cst_in = jnp.pad(state_conv[j], ((0, 0), (CONV_HALO - conv_ctx, 0), (0, 0)))
            pst_in = jnp.pad(state_pool[j], ((0, 0), (POOL_HALO - pool_ctx, 0), (0, 0)))
            m, cst, pst = _mix_sample(z, m, cst_in, pst_in, conv_w[j], pool_w[j], pool_scale[j][None],
                                      t_p, seq_s, PAST_LEN)
            conv_s.append(cst[:, CONV_HALO - conv_ctx:])
            pool_s.append(pst[:, POOL_HALO - pool_ctx:])
            x, h = _out(m, w_mix_out, j, x, norm_g, i, tm)
        else:
            qkv = _proj(h, w_qkv, j, tm_proj)
            att = _attn_prompt(qkv, bias_p, attn_sinks[j], n_p, seq_p, n_heads, n_kv, t)
            ck = cache_k[j].reshape(n_s, kv_buf, kvw)
            cv = cache_v[j].reshape(n_s, kv_buf, kvw)
            att = _attn_sample(qkv, att, ck, cv, bias_s, attn_sinks[j], t_p, seq_s, n_heads)
            k_new, v_new = qkv[:, d:d + kvw], qkv[:, d + kvw:]
            kv_p = lambda a: a[:t_p].reshape(n_p, seq_p, n_kv, HEAD_DIM)[:, seq_p - kv_buf:]
            k_p.append(kv_p(k_new))
            v_p.append(kv_p(v_new))
            kv_s = lambda c, a: jnp.concatenate(
                [c, a[t_p:].reshape(n_s, seq_s, kvw)], axis=1)[:, seq_s:].reshape(n_s, kv_buf, n_kv, HEAD_DIM)
            k_s.append(kv_s(ck, k_new))
            v_s.append(kv_s(cv, v_new))
            x, h = _out(att, w_o, j, x, norm_g, i, tm)
        if i + 1 < depth:
            x, h = _ffn(h, x, norm_g, w_ffn_in, w_ffn_out, i, 1, *ffn_tiles, nxt=(i + 1, FFN1_PRE))
        else:
            y_p, y_s = _ffn(h, x, norm_g, w_ffn_in, w_ffn_out, i, 1, *last_ffn_tiles, head_rows=t_p)

    y_p = y_p.reshape(n_p, seq_p, d)
    y_s = y_s.reshape(n_s, seq_s, d)
    st = jnp.stack
    return (y_p, y_s, st(conv_p), st(pool_p), st(k_p), st(v_p), st(conv_s), st(pool_s), st(k_s), st(v_s))
```

```python
import functools
import math

import jax
import jax.numpy as jnp
from jax import lax
from jax.experimental import pallas as pl
from jax.experimental.pallas import tpu as pltpu

F32 = jnp.float32
BF16 = jnp.bfloat16

EPS = 1e-6
NEG = -1e30
HEAD_DIM = 64
WINDOW = 128
BLOCK = WINDOW
CONV_K = 3
POOL_WINDOWS = (2, 4, 8, 16)
N_BUCKETS = 32
MAX_DISTANCE = 128
PAST_LEN = 16384

LANES = 128
SUBLANES = 8
BF16_ROWS = 2 * SUBLANES
COL_TILE = 4 * LANES
CONV_HALO = SUBLANES
POOL_HALO = 2 * SUBLANES
MIX_ROWS_MAX = 256
NORM_ROWS_MAX = 176
BIAS_HEADS_PER_STEP = 8
FFN_TILINGS = ((1408, 2 * LANES), (1056, 2 * LANES))
FINISH_ROWS_MAX = 64
VMEM_LIMIT_BYTES = 58 * 1024 * 1024
ROW_TILE = 1056
PROJ_ROW_TILE = 2112


def _divisor_tile(n, target, mult):
    best = None
    for t in range(mult, min(n, target) + 1, mult):
        if n % t == 0:
            best = t
    assert best is not None, (n, target, mult)
    return best


def _params(*sem):
    return pltpu.CompilerParams(dimension_semantics=sem, vmem_limit_bytes=VMEM_LIMIT_BYTES)


def _rms(xf, g):
    ms = jnp.mean(xf * xf, axis=-1, keepdims=True)
    return xf * lax.rsqrt(ms + EPS) * g


def _for_row_chunks(n_rows, body):
    rc = _divisor_tile(n_rows, NORM_ROWS_MAX, BF16_ROWS)

    def step(c, carry):
        body(pl.ds(pl.multiple_of(c * rc, rc), rc))
        return carry

    lax.fori_loop(0, n_rows // rc, step, 0)


def _merge_kernel(xp_ref, xs_ref, g_ref, x_ref, h_ref, *, n_prompt_blocks, row):
    x = jnp.where(pl.program_id(0) < n_prompt_blocks, xp_ref[...], xs_ref[...])
    x_ref[...] = x
    h_ref[...] = _rms(x, g_ref[row:row + 1, :]).astype(BF16)


def _merge(x_prompt, x_sample, norm_g, layer, row):
    (t_p, d), t_s = x_prompt.shape, x_sample.shape[0]
    assert t_p % t_s == 0 and t_s % BF16_ROWS == 0
    nbp = t_p // t_s
    row_block = pl.BlockSpec((t_s, d), lambda i: (i, 0))
    return pl.pallas_call(
        functools.partial(_merge_kernel, n_prompt_blocks=nbp, row=row),
        grid=(nbp + 1,),
        in_specs=[
            pl.BlockSpec((t_s, d), lambda i: (jnp.minimum(i, nbp - 1), 0)),
            pl.BlockSpec((t_s, d), lambda i: (0, 0)),
            pl.BlockSpec((None,) + norm_g.shape[1:], lambda i: (layer, 0, 0)),
        ],
        out_specs=[row_block, row_block],
        out_shape=[jax.ShapeDtypeStruct((t_p + t_s, d), F32), jax.ShapeDtypeStruct((t_p + t_s, d), BF16)],
        compiler_params=_params("parallel"),
        name="merge_rows",
    )(x_prompt, x_sample, norm_g)


def _ffn_step(step, nb, nj):
    real = step < nb * nj
    return jnp.where(real, step // nj, nb), jnp.where(real, step % nj, step - nb * nj)


def _ffn_kernel(h_ref, xr_ref, g_ref, gn_ref, wg_ref, *rest, d_ff, ff_tile, nb, nc, post, nxt, n_head_chunks):
    assert nxt is None or n_head_chunks is None
    pieces = ff_tile // LANES
    wu_refs, wo_refs, rest = rest[:pieces], rest[pieces:2 * pieces], rest[2 * pieces:]
    xo_ref, rest = rest[0], rest[1:]
    o2_ref, acc_ref = rest if len(rest) == 2 else (None,) + rest
    i, j = _ffn_step(pl.program_id(0), nb, pl.cdiv(d_ff, ff_tile))
    tm, d = h_ref.shape
    rc = xr_ref.shape[0]
    cur = lax.rem(i, 2)
    prv = 1 - cur

    @pl.when(pl.program_id(0) == 0)
    def _():
        def zero(rows):
            acc_ref[0, rows, :] = jnp.zeros((rows.size, d), F32)
            acc_ref[1, rows, :] = jnp.zeros((rows.size, d), F32)
        _for_row_chunks(tm, zero)

    n_full = d_ff // ff_tile
    end_pieces = (d_ff - n_full * ff_tile) // LANES
    wc = _divisor_tile(d, COL_TILE, LANES)
    row_halves = [slice(0, tm // 2), slice(tm // 2, tm)]

    def swiglu(row_sets, w, wo):
        width = wo.shape[0]
        hidden = []
        for rows in row_sets:
            r = jnp.dot(h_ref[rows, :], w, preferred_element_type=F32)
            gate, up = r[:, :width], r[:, width:]
            hidden.append(((gate * jax.nn.sigmoid(gate)) * up).astype(BF16))
        for rows, a in zip(row_sets, hidden):
            for c in range(d // wc):
                acc_ref[cur, rows, c * wc:(c + 1) * wc] += jnp.dot(a, wo[:, c * wc:(c + 1) * wc],
                                                                   preferred_element_type=F32)

    def tile(n_pieces=pieces):
        w = jnp.concatenate([wg_ref[:, :n_pieces * LANES].astype(BF16)]
                            + [r[...].astype(BF16) for r in wu_refs[:n_pieces]], axis=1)
        wo = jnp.concatenate([r[...].astype(BF16) for r in wo_refs[:n_pieces]], axis=0)
        swiglu(row_halves, w, wo)

    def end_tile():
        tile(end_pieces)

    def finish_chunk(then=None):
        rows = pl.ds(pl.multiple_of(j * rc, rc), rc)
        y = xr_ref[...] + 0.5 * _rms(acc_ref[prv, rows, :], g_ref[post:post + 1, :])
        acc_ref[prv, rows, :] = jnp.zeros((rc, d), F32)
        if n_head_chunks is None:
            xo_ref[...] = y
            if nxt is not None:
                o2_ref[...] = _rms(y, gn_ref[nxt:nxt + 1, :]).astype(BF16)
        if then is not None:
            then()
        if n_head_chunks is not None:
            in_head = (i - 1) * nc + j < n_head_chunks

            @pl.when(in_head)
            def _():
                xo_ref[...] = y

            @pl.when(jnp.logical_not(in_head))
            def _():
                o2_ref[...] = y

    real = i < nb

    @pl.when(real & (j < nc) & (j < n_full))
    def _():
        finish_chunk(tile)

    @pl.when(real & (j >= nc) & (j < n_full))
    def _():
        tile()

    if end_pieces:
        @pl.when(real & (j == n_full))
        def _():
            if n_full < nc:
                finish_chunk(end_tile)
            else:
                end_tile()

    @pl.when(jnp.logical_not(real) & (j < nc))
    def _():
        finish_chunk()


def _finish_rows(tm, n_steps, align_rows=()):
    fits = [c for c in range(BF16_ROWS, FINISH_ROWS_MAX + 1, BF16_ROWS)
            if tm % c == 0 and tm // c <= n_steps and all(r % c == 0 for r in align_rows)]
    return max(fits) if fits else None


def _ffn_tiling(t, d_ff, align_rows=()):
    for tm_target, ff_tile in FFN_TILINGS:
        tm = _divisor_tile(t, tm_target, BF16_ROWS)
        if _finish_rows(tm, pl.cdiv(d_ff, ff_tile), align_rows) is not None:
            return tm, ff_tile
    raise ValueError((t, d_ff, align_rows))


def _ffn(h, x, norm_g, w_in, w_out, layer, which, tm, ff_tile, nxt=None, head_rows=None):
    t, d = x.shape
    d_ff = w_out.shape[2]
    pieces = ff_tile // LANES
    assert ff_tile % LANES == 0 and d_ff % LANES == 0 and w_in.shape[2:] == (d, 2 * d_ff) and t % tm == 0
    n_pieces = d_ff // LANES
    nj = pl.cdiv(d_ff, ff_tile)
    nb = t // tm
    rc = _finish_rows(tm, nj, () if head_rows is None else (head_rows, t - head_rows))
    nc = tm // rc
    nxt_layer, nxt_row = nxt if nxt is not None else (layer, None)
    n_head = None
    if head_rows is not None:
        assert nxt is None
        n_head = head_rows // rc
    kern = functools.partial(_ffn_kernel, d_ff=d_ff, ff_tile=ff_tile, nb=nb, nc=nc, post=4 * which + 1,
                             nxt=nxt_row, n_head_chunks=n_head)
    spec = lambda shape, ij_map: pl.BlockSpec(shape, lambda step: ij_map(*_ffn_step(step, nb, nj)))
    jw = lambda i, j: jnp.where(i < nb, j, nj - 1)
    w_in_spec = lambda width, col: spec((None, None, d, width), lambda i, j: (layer, which, 0, col(jw(i, j))))
    w_out_spec = lambda row: spec((None, None, LANES, d), lambda i, j: (layer, which, row(jw(i, j)), 0))
    up_piece = lambda q: (lambda j: jnp.minimum(n_pieces + pieces * j + q, 2 * n_pieces - 1))
    down_piece = lambda q: (lambda j: jnp.minimum(pieces * j + q, n_pieces - 1))
    chunk = lambda i, j: (jnp.maximum((i - 1) * nc + jnp.minimum(j, nc - 1), 0), 0)
    g_spec = lambda l: spec((None,) + norm_g.shape[1:], lambda i, j: (l, 0, 0))
    if nxt is not None:
        out_specs = [spec((rc, d), chunk), spec((rc, d), chunk)]
        out_shape = [jax.ShapeDtypeStruct((t, d), F32), jax.ShapeDtypeStruct((t, d), BF16)]
    elif n_head is not None:
        out_specs = [spec((rc, d), lambda i, j: (jnp.minimum(chunk(i, j)[0], n_head - 1), 0)),
                     spec((rc, d), lambda i, j: (jnp.maximum(chunk(i, j)[0] - n_head, 0), 0))]
        out_shape = [jax.ShapeDtypeStruct((head_rows, d), F32), jax.ShapeDtypeStruct((t - head_rows, d), F32)]
    else:
        out_specs = [spec((rc, d), chunk)]
        out_shape = [jax.ShapeDtypeStruct((t, d), F32)]
    res = pl.pallas_call(
        kern,
        grid=(nb * nj + nc,),
        in_specs=[
            spec((tm, d), lambda i, j: (jnp.minimum(i, nb - 1), 0)),
            spec((rc, d), chunk),
            g_spec(layer),
            g_spec(nxt_layer),
            w_in_spec(ff_tile, lambda j: j),
            *[w_in_spec(LANES, up_piece(q)) for q in range(pieces)],
            *[w_out_spec(down_piece(q)) for q in range(pieces)],
        ],
        out_specs=out_specs,
        out_shape=out_shape,
        scratch_shapes=[pltpu.VMEM((2, tm, d), F32)],
        compiler_params=_params("arbitrary"),
        name="ffn",
    )(h, x, norm_g, norm_g, w_in, *([w_in] * pieces), *([w_out] * pieces))
    return (res[0], res[1]) if len(res) == 2 else (res[0], None)


FFN1_PRE, MIX_PRE, MIX_POST, FFN2_PRE = 0, 2, 3, 4


def _proj_kernel(h_ref, w_ref, o_ref):
    o_ref[...] = jnp.dot(h_ref[...], w_ref[...].astype(BF16), preferred_element_type=F32)


def _proj(h, w, widx, tm):
    t, d = h.shape
    n = w.shape[2]
    tn = _divisor_tile(n, COL_TILE, LANES)
    return pl.pallas_call(
        _proj_kernel,
        grid=(t // tm, n // tn),
        in_specs=[
            pl.BlockSpec((tm, d), lambda i, j: (i, 0)),
            pl.BlockSpec((None, d, tn), lambda i, j: (widx, 0, j)),
        ],
        out_specs=pl.BlockSpec((tm, tn), lambda i, j: (i, j)),
        out_shape=jax.ShapeDtypeStruct((t, n), F32),
        compiler_params=_params("parallel", "arbitrary"),
        name="proj",
    )(h, w)


def _column_tiles_kernel(w_ref, o_ref):
    o_ref[...] = w_ref[...].astype(BF16)


def _column_tiles_bf16(w, tn):
    n, kdim, d = w.shape
    return pl.pallas_call(
        _column_tiles_kernel,
        grid=(n, d // tn),
        in_specs=[pl.BlockSpec((None, kdim, tn), lambda a, j: (a, 0, j))],
        out_specs=pl.BlockSpec((None, None, kdim, tn), lambda a, j: (a, j, 0, 0)),
        out_shape=jax.ShapeDtypeStruct((n, d // tn, kdim, tn), BF16),
        compiler_params=_params("parallel", "parallel"),
        name="column_tiles_bf16",
    )(w)


def _out_kernel(m_ref, w_ref, xr_ref, g_ref, xo_ref, ho_ref, acc_ref, *, nb, nc):
    i, j = pl.program_id(0), pl.program_id(1)
    ns, tm, tn = acc_ref.shape[1:]
    rc = xr_ref.shape[0]
    cur = lax.rem(i, 2)
    prv = 1 - cur

    @pl.when((i == 0) & (j == 0))
    def _():
        acc_ref[1] = jnp.zeros(acc_ref.shape[1:], F32)

    def tile():
        acc_ref[cur, j] = jnp.dot(m_ref[...], w_ref[j], preferred_element_type=F32)

    def finish_chunk():
        sub = _divisor_tile(rc, NORM_ROWS_MAX, BF16_ROWS)
        for r in range(rc // sub):
            rows = pl.ds(pl.multiple_of(j * rc + r * sub, BF16_ROWS), sub)
            o = jnp.concatenate([acc_ref[prv, s, rows, :] for s in range(ns)], axis=1)
            y = xr_ref[r * sub:(r + 1) * sub, :] + _rms(o, g_ref[MIX_POST:MIX_POST + 1, :])
            xo_ref[r * sub:(r + 1) * sub, :] = y
            ho_ref[r * sub:(r + 1) * sub, :] = _rms(y, g_ref[FFN2_PRE:FFN2_PRE + 1, :]).astype(BF16)

    real = i < nb

    @pl.when(real & (j < nc))
    def _():
        finish_chunk()
        tile()

    @pl.when(real & (j >= nc))
    def _():
        tile()

    @pl.when(jnp.logical_not(real) & (j < nc))
    def _():
        finish_chunk()


def _out(m, w, widx, x, norm_g, layer, tm):
    t, d = x.shape
    ns, kdim, tn = w.shape[1:]
    assert ns * tn == d
    nb = t // tm
    rc = min(c for c in range(BF16_ROWS, tm + 1, BF16_ROWS) if tm % c == 0 and tm // c <= ns)
    nc = tm // rc
    chunk = pl.BlockSpec((rc, d), lambda i, j: (jnp.maximum((i - 1) * nc + jnp.minimum(j, nc - 1), 0), 0))
    return pl.pallas_call(
        functools.partial(_out_kernel, nb=nb, nc=nc),
        grid=(nb + 1, ns),
        in_specs=[
            pl.BlockSpec((tm, kdim), lambda i, j: (jnp.minimum(i, nb - 1), 0)),
            pl.BlockSpec((None, ns, kdim, tn), lambda i, j: (widx, 0, 0, 0), pipeline_mode=pl.Buffered(1)),
            chunk,
            pl.BlockSpec((None,) + norm_g.shape[1:], lambda i, j: (layer, 0, 0)),
        ],
        out_specs=[chunk, chunk],
        out_shape=[jax.ShapeDtypeStruct((t, d), F32), jax.ShapeDtypeStruct((t, d), BF16)],
        scratch_shapes=[pltpu.VMEM((2, ns, tm, tn), F32)],
        compiler_params=_params("arbitrary", "arbitrary"),
        name="outproj",
    )(m, w, x, norm_g)


def _conv3(ext, cw):
    return cw[0:1] * pltpu.roll(ext, 2, 0) + cw[1:2] * pltpu.roll(ext, 1, 0) + cw[2:3] * ext


def _window_sum(ext, w):
    s, k = ext, 1
    while k < w:
        s = s + pltpu.roll(s, k, 0)
        k *= 2
    return s


def _pool_group(win, cnt, ug, pw, scale):
    dlt = win / cnt - ug
    return jnp.dot(dlt.astype(BF16), pw.astype(BF16), preferred_element_type=F32) * scale


def _mix_prompt_kernel(hc_ref, gc_ref, gb_ref, u_ref, hch_ref, gch_ref, uh_ref, cw_ref, pw_ref,
                       ps_ref, m_ref, cst_ref, pst_ref, *, chunks_per_seq, n_chunks):
    rows, c = hc_ref.shape
    gw = c // len(POOL_WINDOWS)
    ci = pl.program_id(0) % chunks_per_seq
    first = ci == 0

    @pl.when(pl.program_id(0) >= n_chunks)
    def _():
        m_ref[...] = jnp.zeros(m_ref.shape, BF16)

    @pl.when(pl.program_id(0) < n_chunks)
    def _():
        v = gc_ref[...] * hc_ref[...]
        v_halo = jnp.where(first, 0.0, gch_ref[...] * hch_ref[...])
        y = _conv3(jnp.concatenate([v_halo, v], axis=0), cw_ref[...])[CONV_HALO:]
        m_ref[:, :c] = (gb_ref[...] * y).astype(BF16)
        cst_ref[...] = v[rows - CONV_HALO:]

        u = u_ref[...]
        u_ext = jnp.concatenate([jnp.where(first, 0.0, uh_ref[...]), u], axis=0)
        pst_ref[...] = u_ext[rows:]
        pos1 = ci * rows + lax.broadcasted_iota(jnp.int32, (rows, 1), 0) + 1
        for gi, w in enumerate(POOL_WINDOWS):
            sl = slice(gi * gw, (gi + 1) * gw)
            win = _window_sum(u_ext[:, sl], w)[POOL_HALO:]
            cnt = jnp.minimum(w, pos1).astype(F32)
            yp = _pool_group(win, cnt, u[:, sl], pw_ref[gi], ps_ref[:, sl])
            m_ref[:, c + gi * gw:c + (gi + 1) * gw] = yp.astype(BF16)


def _mix_prompt(z, conv_w, pool_w, pool_scale, n_seq, seq, t_total):
    c = conv_w.shape[1]
    assert z.shape[1] == 4 * c
    t_tail = t_total - n_seq * seq
    rows = max(r for r in range(POOL_HALO, MIX_ROWS_MAX + 1, POOL_HALO) if seq % r == 0 and t_tail % r == 0)
    cps = seq // rows
    n_chunks = n_seq * cps
    rh_c, rh_p = rows // CONV_HALO, rows // POOL_HALO

    def halo(ratio, col):
        return lambda i: (jnp.maximum(i * ratio - 1, 0), col)

    kern = functools.partial(_mix_prompt_kernel, chunks_per_seq=cps, n_chunks=n_chunks)
    gw = c // len(POOL_WINDOWS)
    state = lambda i: (jnp.minimum(i // cps, n_seq - 1), 0, 0)
    return pl.pallas_call(
        kern,
        grid=(n_chunks + t_tail // rows,),
        in_specs=[
            pl.BlockSpec((rows, c), lambda i: (i, 0)),
            pl.BlockSpec((rows, c), lambda i: (i, 1)),
            pl.BlockSpec((rows, c), lambda i: (i, 2)),
            pl.BlockSpec((rows, c), lambda i: (i, 3)),
            pl.BlockSpec((CONV_HALO, c), halo(rh_c, 0)),
            pl.BlockSpec((CONV_HALO, c), halo(rh_c, 1)),
            pl.BlockSpec((POOL_HALO, c), halo(rh_p, 3)),
            pl.BlockSpec((CONV_K, c), lambda i: (0, 0)),
            pl.BlockSpec((len(POOL_WINDOWS), gw, gw), lambda i: (0, 0, 0)),
            pl.BlockSpec((1, c), lambda i: (0, 0)),
        ],
        out_specs=[
            pl.BlockSpec((rows, 2 * c), lambda i: (i, 0)),
            pl.BlockSpec((None, CONV_HALO, c), state),
            pl.BlockSpec((None, POOL_HALO, c), state),
        ],
        out_shape=[
            jax.ShapeDtypeStruct((t_total, 2 * c), BF16),
            jax.ShapeDtypeStruct((n_seq, CONV_HALO, c), F32),
            jax.ShapeDtypeStruct((n_seq, POOL_HALO, c), F32),
        ],
        compiler_params=_params("arbitrary"),
        name="mix_prompt",
    )(z, z, z, z, z, z, z, conv_w, pool_w, pool_scale)


def _mix_sample_kernel(hc_ref, gc_ref, gb_ref, u_ref, cst_in_ref, pst_in_ref, cw_ref, pw_ref,
                       ps_ref, m_in_ref, m_ref, cst_ref, pst_ref, *, seq, past_len):
    del m_in_ref
    rows, c = hc_ref.shape
    n_seq = rows // seq
    gw = c // len(POOL_WINDOWS)

    def with_halo(halo3, x2):
        ext = jnp.concatenate([halo3, x2.reshape(n_seq, seq, c)], axis=1)
        return ext, ext.reshape(n_seq * ext.shape[1], c)

    def body_rows(flat, n_halo):
        return flat.reshape(n_seq, n_halo + seq, flat.shape[-1])[:, n_halo:].reshape(rows, flat.shape[-1])

    v = gc_ref[...] * hc_ref[...]
    v_ext3, v_ext = with_halo(cst_in_ref[...], v)
    y = body_rows(_conv3(v_ext, cw_ref[...]), CONV_HALO)
    m_ref[:, :c] = (gb_ref[...] * y).astype(BF16)
    cst_ref[...] = v_ext3[:, seq:]

    u = u_ref[...]
    u_ext3, u_ext = with_halo(pst_in_ref[...], u)
    pst_ref[...] = u_ext3[:, seq:]
    pos1 = past_len + jnp.bitwise_and(lax.broadcasted_iota(jnp.int32, (rows, 1), 0), seq - 1) + 1
    for gi, w in enumerate(POOL_WINDOWS):
        sl = slice(gi * gw, (gi + 1) * gw)
        win = body_rows(_window_sum(u_ext[:, sl], w), POOL_HALO)
        cnt = jnp.minimum(w, pos1).astype(F32)
        yp = _pool_group(win, cnt, u[:, sl], pw_ref[gi], ps_ref[:, sl])
        m_ref[:, c + gi * gw:c + (gi + 1) * gw] = yp.astype(BF16)


def _mix_sample(z, m, conv_state, pool_state, conv_w, pool_w, pool_scale, t_prompt, seq, past_len):
    c = conv_w.shape[1]
    n_seq = conv_state.shape[0]
    rows = n_seq * seq
    assert seq == SUBLANES and t_prompt % rows == 0
    rb = t_prompt // rows
    gw = c // len(POOL_WINDOWS)
    full3 = lambda i: (0, 0, 0)
    kern = functools.partial(_mix_sample_kernel, seq=seq, past_len=past_len)
    return pl.pallas_call(
        kern,
        grid=(1,),
        in_specs=[
            pl.BlockSpec((rows, c), lambda i: (rb, 0)),
            pl.BlockSpec((rows, c), lambda i: (rb, 1)),
            pl.BlockSpec((rows, c), lambda i: (rb, 2)),
            pl.BlockSpec((rows, c), lambda i: (rb, 3)),
            pl.BlockSpec((n_seq, CONV_HALO, c), full3),
            pl.BlockSpec((n_seq, POOL_HALO, c), full3),
            pl.BlockSpec((CONV_K, c), lambda i: (0, 0)),
            pl.BlockSpec((len(POOL_WINDOWS), gw, gw), full3),
            pl.BlockSpec((1, c), lambda i: (0, 0)),
            pl.BlockSpec(memory_space=pl.ANY),
        ],
        out_specs=[
            pl.BlockSpec((rows, 2 * c), lambda i: (rb, 0)),
            pl.BlockSpec((n_seq, CONV_HALO, c), full3),
            pl.BlockSpec((n_seq, POOL_HALO, c), full3),
        ],
        out_shape=[
            jax.ShapeDtypeStruct(m.shape, m.dtype),
            jax.ShapeDtypeStruct((n_seq, CONV_HALO, c), F32),
            jax.ShapeDtypeStruct((n_seq, POOL_HALO, c), F32),
        ],
        input_output_aliases={9: 0},
        compiler_params=_params("arbitrary"),
        name="mix_sample",
    )(z, z, z, z, conv_state, pool_state, conv_w, pool_w, pool_scale, m)


N_KEYS = 2 * BLOCK


def _t5_bucket(dist):
    max_exact = N_BUCKETS // 2
    n = jnp.maximum(dist, 0)
    ratio = jnp.log(jnp.maximum(n, 1).astype(F32) / max_exact) / math.log(MAX_DISTANCE / max_exact)
    large = jnp.minimum(max_exact + (ratio * (N_BUCKETS - max_exact)).astype(jnp.int32), N_BUCKETS - 1)
    return jnp.where(n < max_exact, n, large)


def _bias_prompt_kernel(rb_ref, o_ref):
    heads, n_keys, n_q = o_ref.shape
    s = lax.broadcasted_iota(jnp.int32, (n_keys, n_q), 0)
    q = lax.broadcasted_iota(jnp.int32, (n_keys, n_q), 1)
    bucket = _t5_bucket(BLOCK + q - s)
    for hh in range(heads):
        head = pl.program_id(0) * heads + hh
        acc = jnp.zeros((n_keys, n_q), F32)
        for b in range(N_BUCKETS):
            acc = jnp.where(bucket == b, rb_ref[b, head], acc)
        o_ref[hh] = acc


def _bias_sample_kernel(rb_ref, o_ref, *, seq, n_heads):
    half = pl.program_id(0)
    s = lax.broadcasted_iota(jnp.int32, o_ref.shape, 0)
    lane = lax.broadcasted_iota(jnp.int32, o_ref.shape, 1)
    bucket = _t5_bucket(BLOCK + jnp.bitwise_and(lane, seq - 1) - s)
    lane1 = lax.broadcasted_iota(jnp.int32, (1, LANES), 1)
    acc = jnp.zeros(o_ref.shape, F32)
    for b in range(N_BUCKETS):
        vec = jnp.zeros((1, LANES), F32)
        for slab in range(n_heads // 2):
            vec = jnp.where((lane1 >= slab * seq) & (lane1 < (slab + 1) * seq), rb_ref[b, 2 * slab + half], vec)
        acc = jnp.where(bucket == b, vec, acc)
    o_ref[...] = acc


def _bias_tables(rel_bias, seq_s):
    n_heads = rel_bias.shape[1]
    assert seq_s == SUBLANES and (n_heads // 2) * seq_s <= LANES
    heads_per_step = math.gcd(n_heads, BIAS_HEADS_PER_STEP)
    prompt = pl.pallas_call(
        _bias_prompt_kernel,
        grid=(n_heads // heads_per_step,),
        in_specs=[pl.BlockSpec(memory_space=pltpu.SMEM)],
        out_specs=pl.BlockSpec((heads_per_step, N_KEYS, BLOCK), lambda h: (h, 0, 0)),
        out_shape=jax.ShapeDtypeStruct((n_heads, N_KEYS, BLOCK), F32),
        compiler_params=_params("arbitrary"),
        name="rel_bias_prompt",
    )(rel_bias)
    sample = pl.pallas_call(
        functools.partial(_bias_sample_kernel, seq=seq_s, n_heads=n_heads),
        grid=(2,),
        in_specs=[pl.BlockSpec(memory_space=pltpu.SMEM)],
        out_specs=pl.BlockSpec((None, N_KEYS, LANES), lambda h: (h, 0, 0)),
        out_shape=jax.ShapeDtypeStruct((2, N_KEYS, LANES), F32),
        compiler_params=_params("arbitrary"),
        name="rel_bias_sample",
    )(rel_bias)
    return prompt, sample


def _stage_kv(k_all, v_all, kpad_ref, vtpad_ref):
    n_keys, kvw = k_all.shape
    lane = lax.broadcasted_iota(jnp.int32, (n_keys, LANES), 1)
    row = lax.broadcasted_iota(jnp.int32, (LANES, n_keys), 0)
    for slab in range(kvw // LANES):
        ks = k_all[:, slab * LANES:(slab + 1) * LANES]
        vt = v_all[:, slab * LANES:(slab + 1) * LANES].T
        for own in range(2):
            kv = 2 * slab + own
            k_own = jnp.where((lane >= own * HEAD_DIM) & (lane < (own + 1) * HEAD_DIM), ks, 0.0)
            v_own = jnp.where((row >= own * HEAD_DIM) & (row < (own + 1) * HEAD_DIM), vt, 0.0)
            kpad_ref[kv, own] = k_own.astype(BF16)
            kpad_ref[kv, 1 - own] = pltpu.roll(k_own, HEAD_DIM, 1).astype(BF16)
            vtpad_ref[kv, own] = v_own.astype(BF16)
            vtpad_ref[kv, 1 - own] = pltpu.roll(v_own, HEAD_DIM, 0).astype(BF16)


_TRANS_B = (((1,), (1,)), ((), ()))


def _attn_prompt_kernel(*refs, blocks_per_seq, n_blocks):
    step = pl.program_id(0)

    @pl.when(step >= n_blocks)
    def _():
        o_ref = refs[7]
        o_ref[...] = jnp.zeros(o_ref.shape, o_ref.dtype)

    @pl.when(step < n_blocks)
    def _():
        _attn_prompt_block(lax.rem(step, blocks_per_seq) == 0, *refs)


def _attn_prompt_block(first, sinks_ref, q_ref, kc_ref, kp_ref, vc_ref, vp_ref, bias_ref, o_ref,
                       kpad_ref, vtpad_ref, s_ref, m_ref):
    n_heads = bias_ref.shape[0]
    group = n_heads // kpad_ref.shape[0]
    _stage_kv(jnp.concatenate([kp_ref[...], kc_ref[...]], axis=0),
              jnp.concatenate([vp_ref[...], vc_ref[...]], axis=0), kpad_ref, vtpad_ref)
    key = lax.broadcasted_iota(jnp.int32, (N_KEYS, BLOCK), 0)
    qry = lax.broadcasted_iota(jnp.int32, (N_KEYS, BLOCK), 1)
    valid = (key >= qry) & (key <= qry + WINDOW) & ((key >= BLOCK) | jnp.logical_not(first))

    for p in range(n_heads // 2):
        q_slab = (q_ref[:, p * LANES:(p + 1) * LANES] * HEAD_DIM ** -0.5).astype(BF16)
        for half in range(2):
            head = 2 * p + half
            s = lax.dot_general(kpad_ref[head // group, half], q_slab, _TRANS_B, preferred_element_type=F32)
            s = jnp.where(valid, s + bias_ref[head], NEG)
            s_ref[head] = s
            m_ref[head:head + 1, :] = jnp.maximum(jnp.max(s, axis=0, keepdims=True), sinks_ref[head])
    first_head = lax.broadcasted_iota(jnp.int32, (LANES, BLOCK), 0) < HEAD_DIM
    for p in range(n_heads // 2):
        kv = 2 * p // group
        probs, inv = [], []
        for head in (2 * p, 2 * p + 1):
            m = m_ref[head:head + 1, :]
            e = jnp.exp(s_ref[head] - m)
            inv.append(1.0 / (jnp.sum(e, axis=0, keepdims=True) + jnp.exp(sinks_ref[head] - m)))
            probs.append(e.astype(BF16))
        acc = (jnp.dot(vtpad_ref[kv, 0], probs[0], preferred_element_type=F32)
               + jnp.dot(vtpad_ref[kv, 1], probs[1], preferred_element_type=F32))
        o_ref[:, p * LANES:(p + 1) * LANES] = (acc * jnp.where(first_head, inv[0], inv[1])).T.astype(o_ref.dtype)


def _attn_prompt(qkv, bias, sinks, n_seq, seq, n_heads, n_kv, t_total):
    d = n_heads * HEAD_DIM
    kvw = n_kv * HEAD_DIM
    assert kvw % LANES == 0 and d % kvw == 0 and seq % BLOCK == 0
    nb = seq // BLOCK
    kcol, vcol = d // kvw, d // kvw + 1
    n_blocks = n_seq * nb
    n_tail = pl.cdiv(t_total - n_seq * seq, BLOCK)
    cur = lambda col: (lambda s: (s, col))
    prev = lambda col: (lambda s: (jnp.maximum(s - 1, 0), col))
    return pl.pallas_call(
        functools.partial(_attn_prompt_kernel, blocks_per_seq=nb, n_blocks=n_blocks),
        grid=(n_blocks + n_tail,),
        in_specs=[
            pl.BlockSpec(memory_space=pltpu.SMEM),
            pl.BlockSpec((BLOCK, d), cur(0)),
            pl.BlockSpec((BLOCK, kvw), cur(kcol)),
            pl.BlockSpec((BLOCK, kvw), prev(kcol)),
            pl.BlockSpec((BLOCK, kvw), cur(vcol)),
            pl.BlockSpec((BLOCK, kvw), prev(vcol)),
            pl.BlockSpec((n_heads, N_KEYS, BLOCK), lambda s: (0, 0, 0), pipeline_mode=pl.Buffered(1)),
        ],
        out_specs=pl.BlockSpec((BLOCK, d), cur(0)),
        out_shape=jax.ShapeDtypeStruct((t_total, d), BF16),
        scratch_shapes=[
            pltpu.VMEM((n_kv, 2, N_KEYS, LANES), BF16),
            pltpu.VMEM((n_kv, 2, LANES, N_KEYS), BF16),
            pltpu.VMEM((n_heads, N_KEYS, BLOCK), F32),
            pltpu.VMEM((n_heads, BLOCK), F32),
        ],
        compiler_params=_params("arbitrary"),
        name="attn_prompt",
    )(sinks, qkv, qkv, qkv, qkv, qkv, bias)


def _attn_sample_kernel(sinks_ref, q_ref, kn_ref, vn_ref, kc_ref, vc_ref, bias_ref, att_in_ref, o_ref,
                        kpad_ref, vtpad_ref):
    del att_in_ref
    seq, kvw = kn_ref.shape
    n_pairs = q_ref.shape[1] // LANES
    n_kv = kvw // HEAD_DIM
    kv_lanes = n_pairs // n_kv * seq
    pad = jnp.zeros((N_KEYS - kc_ref.shape[0] - seq, kvw), F32)
    _stage_kv(jnp.concatenate([kc_ref[...], kn_ref[...], pad], axis=0),
              jnp.concatenate([vc_ref[...], vn_ref[...], pad], axis=0), kpad_ref, vtpad_ref)

    rows = [q_ref[:, p * LANES:(p + 1) * LANES] for p in range(n_pairs)]
    if n_pairs * seq < LANES:
        rows.append(jnp.zeros((LANES - n_pairs * seq, LANES), F32))
    qs = (jnp.concatenate(rows, axis=0) * HEAD_DIM ** -0.5).astype(BF16)

    key = lax.broadcasted_iota(jnp.int32, (N_KEYS, LANES), 0)
    lane = lax.broadcasted_iota(jnp.int32, (N_KEYS, LANES), 1)
    lane1 = lax.broadcasted_iota(jnp.int32, (1, LANES), 1)
    qry = jnp.bitwise_and(lane, seq - 1)
    valid = (key >= qry) & (key <= qry + WINDOW)

    probs, dens = [], []
    for half in range(2):
        s = jnp.zeros((N_KEYS, LANES), F32)
        for kv in range(n_kv):
            s_kv = lax.dot_general(kpad_ref[kv, half], qs, _TRANS_B, preferred_element_type=F32)
            s = jnp.where((lane >= kv * kv_lanes) & (lane < (kv + 1) * kv_lanes), s_kv, s)
        sink = jnp.zeros((1, LANES), F32)
        for p in range(n_pairs):
            sink = jnp.where((lane1 >= p * seq) & (lane1 < (p + 1) * seq), sinks_ref[2 * p + half], sink)
        s = jnp.where(valid, s + bias_ref[half], NEG)
        m = jnp.maximum(jnp.max(s, axis=0, keepdims=True), sink)
        e = jnp.exp(s - m)
        dens.append(jnp.sum(e, axis=0, keepdims=True) + jnp.exp(sink - m))
        probs.append(e.astype(BF16))

    row_t = lax.broadcasted_iota(jnp.int32, (LANES, LANES), 0)
    lane_t = lax.broadcasted_iota(jnp.int32, (LANES, LANES), 1)
    out_t = jnp.zeros((LANES, LANES), F32)
    for kv in range(n_kv):
        acc = (jnp.dot(vtpad_ref[kv, 0], probs[0], preferred_element_type=F32)
               + jnp.dot(vtpad_ref[kv, 1], probs[1], preferred_element_type=F32))
        out_t = jnp.where((lane_t >= kv * kv_lanes) & (lane_t < (kv + 1) * kv_lanes), acc, out_t)
    inv = jnp.where(row_t < HEAD_DIM, 1.0 / dens[0], 1.0 / dens[1])
    out = (out_t * inv).T
    for p in range(n_pairs):
        o_ref[:, p * LANES:(p + 1) * LANES] = out[p * seq:(p + 1) * seq].astype(o_ref.dtype)


def _attn_sample(qkv, att, cache_k, cache_v, bias, sinks, t_prompt, seq, n_heads):
    n_seq, kv_buf, kvw = cache_k.shape
    d = n_heads * HEAD_DIM
    assert kv_buf == BLOCK and seq == SUBLANES and t_prompt % seq == 0
    rb = t_prompt // seq
    kcol, vcol = d // kvw, d // kvw + 1
    return pl.pallas_call(
        _attn_sample_kernel,
        grid=(n_seq,),
        in_specs=[
            pl.BlockSpec(memory_space=pltpu.SMEM),
            pl.BlockSpec((seq, d), lambda n: (rb + n, 0)),
            pl.BlockSpec((seq, kvw), lambda n: (rb + n, kcol)),
            pl.BlockSpec((seq, kvw), lambda n: (rb + n, vcol)),
            pl.BlockSpec((None, kv_buf, kvw), lambda n: (n, 0, 0)),
            pl.BlockSpec((None, kv_buf, kvw), lambda n: (n, 0, 0)),
            pl.BlockSpec((2, N_KEYS, LANES), lambda n: (0, 0, 0)),
            pl.BlockSpec(memory_space=pl.ANY),
        ],
        out_specs=pl.BlockSpec((seq, d), lambda n: (rb + n, 0)),
        out_shape=jax.ShapeDtypeStruct(att.shape, att.dtype),
        scratch_shapes=[
            pltpu.VMEM((kvw // HEAD_DIM, 2, N_KEYS, LANES), BF16),
            pltpu.VMEM((kvw // HEAD_DIM, 2, LANES, N_KEYS), BF16),
        ],
        input_output_aliases={7: 0},
        compiler_params=_params("arbitrary"),
        name="attn_sample",
    )(sinks, qkv, qkv, qkv, cache_k, cache_v, bias, att)


def kernel(x_prompt, x_sample, state_conv, state_pool, cache_k, cache_v, norm_g, w_ffn_in,
           w_ffn_out, w_mix_in, conv_w, pool_w, pool_scale, w_mix_out, w_qkv, w_o,
           attn_sinks, rel_bias):
    n_p, seq_p, d = x_prompt.shape
    n_s, seq_s, _ = x_sample.shape
    depth = norm_g.shape[0]
    n_heads = d // HEAD_DIM
    n_kv = cache_k.shape[3]
    kv_buf = cache_k.shape[2]
    kvw = n_kv * HEAD_DIM
    conv_ctx = state_conv.shape[2]
    pool_ctx = state_pool.shape[2]
    t_p, t_s = n_p * seq_p, n_s * seq_s
    t = t_p + t_s
    tm = _divisor_tile(t, ROW_TILE, BF16_ROWS)
    tm_proj = _divisor_tile(t, PROJ_ROW_TILE, BF16_ROWS)
    d_ff = w_ffn_out.shape[2]
    ffn_tiles = _ffn_tiling(t, d_ff)
    last_ffn_tiles = _ffn_tiling(t, d_ff, align_rows=(t_p, t_s))
    x, h = _merge(x_prompt.reshape(t_p, d), x_sample.reshape(t_s, d), norm_g, 0, FFN1_PRE)
    bias_p, bias_s = _bias_tables(rel_bias, seq_s)
    out_tile = _divisor_tile(d, COL_TILE, LANES)
    w_mix_out = _column_tiles_bf16(w_mix_out, out_tile)
    w_o = _column_tiles_bf16(w_o, out_tile)

    conv_p, pool_p, k_p, v_p, conv_s, pool_s, k_s, v_s = ([] for _ in range(8))
    for i in range(depth):
        x, h = _ffn(h, x, norm_g, w_ffn_in, w_ffn_out, i, 0, *ffn_tiles, nxt=(i, MIX_PRE))
        j = i // 2
        if i % 2 == 0:
            z = _proj(h, w_mix_in, j, tm_proj)
            m, cst, pst = _mix_prompt(z, conv_w[j], pool_w[j], pool_scale[j][None], n_p, seq_p, t)
            conv_p.append(cst[:, CONV_HALO - conv_ctx:])
            pool_p.append(pst[:, POOL_HALO - pool_ctx:])
            cst_in = jnp.pad(state_conv[j], ((0, 0), (CONV_HALO - conv_ctx, 0), (0, 0)))
            pst_in = jnp.pad(state_pool[j], ((0, 0), (POOL_HALO - pool_ctx, 0), (0, 0)))
            m, cst, pst = _mix_sample(z, m, cst_in, pst_in, conv_w[j], pool_w[j], pool_scale[j][None],
                                      t_p, seq_s, PAST_LEN)
            conv_s.append(cst[:, CONV_HALO - conv_ctx:])
            pool_s.append(pst[:, POOL_HALO - pool_ctx:])
            x, h = _out(m, w_mix_out, j, x, norm_g, i, tm)
        else:
            qkv = _proj(h, w_qkv, j, tm_proj)
            att = _attn_prompt(qkv, bias_p, attn_sinks[j], n_p, seq_p, n_heads, n_kv, t)
            ck = cache_k[j].reshape(n_s, kv_buf, kvw)
            cv = cache_v[j].reshape(n_s, kv_buf, kvw)
            att = _attn_sample(qkv, att, ck, cv, bias_s, attn_sinks[j], t_p, seq_s, n_heads)
            k_new, v_new = qkv[:, d:d + kvw], qkv[:, d + kvw:]
            kv_p = lambda a: a[:t_p].reshape(n_p, seq_p, n_kv, HEAD_DIM)[:, seq_p - kv_buf:]
            k_p.append(kv_p(k_new))
            v_p.append(kv_p(v_new))
            kv_s = lambda c, a: jnp.concatenate(
                [c, a[t_p:].reshape(n_s, seq_s, kvw)], axis=1)[:, seq_s:].reshape(n_s, kv_buf, n_kv, HEAD_DIM)
            k_s.append(kv_s(ck, k_new))
            v_s.append(kv_s(cv, v_new))
            x, h = _out(att, w_o, j, x, norm_g, i, tm)
        if i + 1 < depth:
            x, h = _ffn(h, x, norm_g, w_ffn_in, w_ffn_out, i, 1, *ffn_tiles, nxt=(i + 1, FFN1_PRE))
        else:
            y_p, y_s = _ffn(h, x, norm_g, w_ffn_in, w_ffn_out, i, 1, *last_ffn_tiles, head_rows=t_p)

    y_p = y_p.reshape(n_p, seq_p, d)
    y_s = y_s.reshape(n_s, seq_s, d)
    st = jnp.stack
    return (y_p, y_s, st(conv_p), st(pool_p), st(k_p), st(v_p), st(conv_s), st(pool_s), st(k_s), st(v_s))
```

```python
import functools
import math

import jax
import jax.numpy as jnp
from jax import lax
from jax.experimental import pallas as pl
from jax.experimental.pallas import tpu as pltpu

F32 = jnp.float32
BF16 = jnp.bfloat16

EPS = 1e-6
NEG = -1e30
HEAD_DIM = 64
WINDOW = 128
BLOCK = WINDOW
CONV_K = 3
POOL_WINDOWS = (2, 4, 8, 16)
N_BUCKETS = 32
MAX_DISTANCE = 128
PAST_LEN = 16384

LANES = 128
SUBLANES = 8
BF16_ROWS = 2 * SUBLANES
COL_TILE = 4 * LANES
CONV_HALO = SUBLANES
POOL_HALO = 2 * SUBLANES
MIX_ROWS_MAX = 256
NORM_ROWS_MAX = 176
BIAS_HEADS_PER_STEP = 8
FFN_TILINGS = ((1408, 2 * LANES), (1056, 2 * LANES))
FINISH_ROWS_MAX = 64
VMEM_LIMIT_BYTES = 58 * 1024 * 1024
ROW_TILE = 1056
PROJ_ROW_TILE = 2112


def _divisor_tile(n, target, mult):
    best = None
    for t in range(mult, min(n, target) + 1, mult):
        if n % t == 0:
            best = t
    assert best is not None, (n, target, mult)
    return best


def _params(*sem):
    return pltpu.CompilerParams(dimension_semantics=sem, vmem_limit_bytes=VMEM_LIMIT_BYTES)


def _rms(xf, g):
    ms = jnp.mean(xf * xf, axis=-1, keepdims=True)
    return xf * lax.rsqrt(ms + EPS) * g


def _for_row_chunks(n_rows, body):
    rc = _divisor_tile(n_rows, NORM_ROWS_MAX, BF16_ROWS)

    def step(c, carry):
        body(pl.ds(pl.multiple_of(c * rc, rc), rc))
        return carry

    lax.fori_loop(0, n_rows // rc, step, 0)


def _merge_kernel(xp_ref, xs_ref, g_ref, x_ref, h_ref, *, n_prompt_blocks, row):
    x = jnp.where(pl.program_id(0) < n_prompt_blocks, xp_ref[...], xs_ref[...])
    x_ref[...] = x
    h_ref[...] = _rms(x, g_ref[row:row + 1, :]).astype(BF16)


def _merge(x_prompt, x_sample, norm_g, layer, row):
    (t_p, d), t_s = x_prompt.shape, x_sample.shape[0]
    assert t_p % t_s == 0 and t_s % BF16_ROWS == 0
    nbp = t_p // t_s
    row_block = pl.BlockSpec((t_s, d), lambda i: (i, 0))
    return pl.pallas_call(
        functools.partial(_merge_kernel, n_prompt_blocks=nbp, row=row),
        grid=(nbp + 1,),
        in_specs=[
            pl.BlockSpec((t_s, d), lambda i: (jnp.minimum(i, nbp - 1), 0)),
            pl.BlockSpec((t_s, d), lambda i: (0, 0)),
            pl.BlockSpec((None,) + norm_g.shape[1:], lambda i: (layer, 0, 0)),
        ],
        out_specs=[row_block, row_block],
        out_shape=[jax.ShapeDtypeStruct((t_p + t_s, d), F32), jax.ShapeDtypeStruct((t_p + t_s, d), BF16)],
        compiler_params=_params("parallel"),
        name="merge_rows",
    )(x_prompt, x_sample, norm_g)


def _ffn_step(step, nb, nj):
    real = step < nb * nj
    return jnp.where(real, step // nj, nb), jnp.where(real, step % nj, step - nb * nj)


def _ffn_kernel(h_ref, xr_ref, g_ref, gn_ref, wg_ref, *rest, d_ff, ff_tile, nb, nc, post, nxt, n_head_chunks):
    assert nxt is None or n_head_chunks is None
    pieces = ff_tile // LANES
    wu_refs, wo_refs, rest = rest[:pieces], rest[pieces:2 * pieces], rest[2 * pieces:]
    xo_ref, rest = rest[0], rest[1:]
    o2_ref, acc_ref = rest if len(rest) == 2 else (None,) + rest
    i, j = _ffn_step(pl.program_id(0), nb, pl.cdiv(d_ff, ff_tile))
    tm, d = h_ref.shape
    rc = xr_ref.shape[0]
    cur = lax.rem(i, 2)
    prv = 1 - cur

    @pl.when(pl.program_id(0) == 0)
    def _():
        def zero(rows):
            acc_ref[0, rows, :] = jnp.zeros((rows.size, d), F32)
            acc_ref[1, rows, :] = jnp.zeros((rows.size, d), F32)
        _for_row_chunks(tm, zero)

    n_full = d_ff // ff_tile
    end_pieces = (d_ff - n_full * ff_tile) // LANES
    wc = _divisor_tile(d, COL_TILE, LANES)
    row_halves = [slice(0, tm // 2), slice(tm // 2, tm)]

    def swiglu(row_sets, w, wo):
        width = wo.shape[0]
        hidden = []
        for rows in row_sets:
            r = jnp.dot(h_ref[rows, :], w, preferred_element_type=F32)
            gate, up = r[:, :width], r[:, width:]
            hidden.append(((gate * jax.nn.sigmoid(gate)) * up).astype(BF16))
        for rows, a in zip(row_sets, hidden):
            for c in range(d // wc):
                acc_ref[cur, rows, c * wc:(c + 1) * wc] += jnp.dot(a, wo[:, c * wc:(c + 1) * wc],
                                                                   preferred_element_type=F32)

    def tile(n_pieces=pieces):
        w = jnp.concatenate([wg_ref[:, :n_pieces * LANES].astype(BF16)]
                            + [r[...].astype(BF16) for r in wu_refs[:n_pieces]], axis=1)
        wo = jnp.concatenate([r[...].astype(BF16) for r in wo_refs[:n_pieces]], axis=0)
        swiglu(row_halves, w, wo)

    def end_tile():
        tile(end_pieces)

    def finish_chunk(then=None):
        rows = pl.ds(pl.multiple_of(j * rc, rc), rc)
        y = xr_ref[...] + 0.5 * _rms(acc_ref[prv, rows, :], g_ref[post:post + 1, :])
        acc_ref[prv, rows, :] = jnp.zeros((rc, d), F32)
        if n_head_chunks is None:
            xo_ref[...] = y
            if nxt is not None:
                o2_ref[...] = _rms(y, gn_ref[nxt:nxt + 1, :]).astype(BF16)
        if then is not None:
            then()
        if n_head_chunks is not None:
            in_head = (i - 1) * nc + j < n_head_chunks

            @pl.when(in_head)
            def _():
                xo_ref[...] = y

            @pl.when(jnp.logical_not(in_head))
            def _():
                o2_ref[...] = y

    real = i < nb

    @pl.when(real & (j < nc) & (j < n_full))
    def _():
        finish_chunk(tile)

    @pl.when(real & (j >= nc) & (j < n_full))
    def _():
        tile()

    if end_pieces:
        @pl.when(real & (j == n_full))
        def _():
            if n_full < nc:
                finish_chunk(end_tile)
            else:
                end_tile()

    @pl.when(jnp.logical_not(real) & (j < nc))
    def _():
        finish_chunk()


def _finish_rows(tm, n_steps, align_rows=()):
    fits = [c for c in range(BF16_ROWS, FINISH_ROWS_MAX + 1, BF16_ROWS)
            if tm % c == 0 and tm // c <= n_steps and all(r % c == 0 for r in align_rows)]
    return max(fits) if fits else None


def _ffn_tiling(t, d_ff, align_rows=()):
    for tm_target, ff_tile in FFN_TILINGS:
        tm = _divisor_tile(t, tm_target, BF16_ROWS)
        if _finish_rows(tm, pl.cdiv(d_ff, ff_tile), align_rows) is not None:
            return tm, ff_tile
    raise ValueError((t, d_ff, align_rows))


def _ffn(h, x, norm_g, w_in, w_out, layer, which, tm, ff_tile, nxt=None, head_rows=None):
    t, d = x.shape
    d_ff = w_out.shape[2]
    pieces = ff_tile // LANES
    assert ff_tile % LANES == 0 and d_ff % LANES == 0 and w_in.shape[2:] == (d, 2 * d_ff) and t % tm == 0
    n_pieces = d_ff // LANES
    nj = pl.cdiv(d_ff, ff_tile)
    nb = t // tm
    rc = _finish_rows(tm, nj, () if head_rows is None else (head_rows, t - head_rows))
    nc = tm // rc
    nxt_layer, nxt_row = nxt if nxt is not None else (layer, None)
    n_head = None
    if head_rows is not None:
        assert nxt is None
        n_head = head_rows // rc
    kern = functools.partial(_ffn_kernel, d_ff=d_ff, ff_tile=ff_tile, nb=nb, nc=nc, post=4 * which + 1,
                             nxt=nxt_row, n_head_chunks=n_head)
    spec = lambda shape, ij_map: pl.BlockSpec(shape, lambda step: ij_map(*_ffn_step(step, nb, nj)))
    jw = lambda i, j: jnp.where(i < nb, j, nj - 1)
    w_in_spec = lambda width, col: spec((None, None, d, width), lambda i, j: (layer, which, 0, col(jw(i, j))))
    w_out_spec = lambda row: spec((None, None, LANES, d), lambda i, j: (layer, which, row(jw(i, j)), 0))
    up_piece = lambda q: (lambda j: jnp.minimum(n_pieces + pieces * j + q, 2 * n_pieces - 1))
    down_piece = lambda q: (lambda j: jnp.minimum(pieces * j + q, n_pieces - 1))
    chunk = lambda i, j: (jnp.maximum((i - 1) * nc + jnp.minimum(j, nc - 1), 0), 0)
    g_spec = lambda l: spec((None,) + norm_g.shape[1:], lambda i, j: (l, 0, 0))
    if nxt is not None:
        out_specs = [spec((rc, d), chunk), spec((rc, d), chunk)]
        out_shape = [jax.ShapeDtypeStruct((t, d), F32), jax.ShapeDtypeStruct((t, d), BF16)]
    elif n_head is not None:
        out_specs = [spec((rc, d), lambda i, j: (jnp.minimum(chunk(i, j)[0], n_head - 1), 0)),
                     spec((rc, d), lambda i, j: (jnp.maximum(chunk(i, j)[0] - n_head, 0), 0))]
        out_shape = [jax.ShapeDtypeStruct((head_rows, d), F32), jax.ShapeDtypeStruct((t - head_rows, d), F32)]
    else:
        out_specs = [spec((rc, d), chunk)]
        out_shape = [jax.ShapeDtypeStruct((t, d), F32)]
    res = pl.pallas_call(
        kern,
        grid=(nb * nj + nc,),
        in_specs=[
            spec((tm, d), lambda i, j: (jnp.minimum(i, nb - 1), 0)),
            spec((rc, d), chunk),
            g_spec(layer),
            g_spec(nxt_layer),
            w_in_spec(ff_tile, lambda j: j),
            *[w_in_spec(LANES, up_piece(q)) for q in range(pieces)],
            *[w_out_spec(down_piece(q)) for q in range(pieces)],
        ],
        out_specs=out_specs,
        out_shape=out_shape,
        scratch_shapes=[pltpu.VMEM((2, tm, d), F32)],
        compiler_params=_params("arbitrary"),
        name="ffn",
    )(h, x, norm_g, norm_g, w_in, *([w_in] * pieces), *([w_out] * pieces))
    return (res[0], res[1]) if len(res) == 2 else (res[0], None)


FFN1_PRE, MIX_PRE, MIX_POST, FFN2_PRE = 0, 2, 3, 4


def _proj_kernel(h_ref, w_ref, o_ref):
    o_ref[...] = jnp.dot(h_ref[...], w_ref[...].astype(BF16), preferred_element_type=F32)


def _proj(h, w, widx, tm):
    t, d = h.shape
    n = w.shape[2]
    tn = _divisor_tile(n, COL_TILE, LANES)
    return pl.pallas_call(
        _proj_kernel,
        grid=(t // tm, n // tn),
        in_specs=[
            pl.BlockSpec((tm, d), lambda i, j: (i, 0)),
            pl.BlockSpec((None, d, tn), lambda i, j: (widx, 0, j)),
        ],
        out_specs=pl.BlockSpec((tm, tn), lambda i, j: (i, j)),
        out_shape=jax.ShapeDtypeStruct((t, n), F32),
        compiler_params=_params("parallel", "arbitrary"),
        name="proj",
    )(h, w)


def _column_tiles_kernel(w_ref, o_ref):
    o_ref[...] = w_ref[...].astype(BF16)


def _column_tiles_bf16(w, tn):
    n, kdim, d = w.shape
    return pl.pallas_call(
        _column_tiles_kernel,
        grid=(n, d // tn),
        in_specs=[pl.BlockSpec((None, kdim, tn), lambda a, j: (a, 0, j))],
        out_specs=pl.BlockSpec((None, None, kdim, tn), lambda a, j: (a, j, 0, 0)),
        out_shape=jax.ShapeDtypeStruct((n, d // tn, kdim, tn), BF16),
        compiler_params=_params("parallel", "parallel"),
        name="column_tiles_bf16",
    )(w)


def _out_kernel(m_ref, w_ref, xr_ref, g_ref, xo_ref, ho_ref, acc_ref, *, nb, nc):
    i, j = pl.program_id(0), pl.program_id(1)
    ns, tm, tn = acc_ref.shape[1:]
    rc = xr_ref.shape[0]
    cur = lax.rem(i, 2)
    prv = 1 - cur

    @pl.when((i == 0) & (j == 0))
    def _():
        acc_ref[1] = jnp.zeros(acc_ref.shape[1:], F32)

    def tile():
        acc_ref[cur, j] = jnp.dot(m_ref[...], w_ref[j], preferred_element_type=F32)

    def finish_chunk():
        sub = _divisor_tile(rc, NORM_ROWS_MAX, BF16_ROWS)
        for r in range(rc // sub):
            rows = pl.ds(pl.multiple_of(j * rc + r * sub, BF16_ROWS), sub)
            o = jnp.concatenate([acc_ref[prv, s, rows, :] for s in range(ns)], axis=1)
            y = xr_ref[r * sub:(r + 1) * sub, :] + _rms(o, g_ref[MIX_POST:MIX_POST + 1, :])
            xo_ref[r * sub:(r + 1) * sub, :] = y
            ho_ref[r * sub:(r + 1) * sub, :] = _rms(y, g_ref[FFN2_PRE:FFN2_PRE + 1, :]).astype(BF16)

    real = i < nb

    @pl.when(real & (j < nc))
    def _():
        finish_chunk()
        tile()

    @pl.when(real & (j >= nc))
    def _():
        tile()

    @pl.when(jnp.logical_not(real) & (j < nc))
    def _():
        finish_chunk()


def _out(m, w, widx, x, norm_g, layer, tm):
    t, d = x.shape
    ns, kdim, tn = w.shape[1:]
    assert ns * tn == d
    nb = t // tm
    rc = min(c for c in range(BF16_ROWS, tm + 1, BF16_ROWS) if tm % c == 0 and tm // c <= ns)
    nc = tm // rc
    chunk = pl.BlockSpec((rc, d), lambda i, j: (jnp.maximum((i - 1) * nc + jnp.minimum(j, nc - 1), 0), 0))
    return pl.pallas_call(
        functools.partial(_out_kernel, nb=nb, nc=nc),
        grid=(nb + 1, ns),
        in_specs=[
            pl.BlockSpec((tm, kdim), lambda i, j: (jnp.minimum(i, nb - 1), 0)),
            pl.BlockSpec((None, ns, kdim, tn), lambda i, j: (widx, 0, 0, 0), pipeline_mode=pl.Buffered(1)),
            chunk,
            pl.BlockSpec((None,) + norm_g.shape[1:], lambda i, j: (layer, 0, 0)),
        ],
        out_specs=[chunk, chunk],
        out_shape=[jax.ShapeDtypeStruct((t, d), F32), jax.ShapeDtypeStruct((t, d), BF16)],
        scratch_shapes=[pltpu.VMEM((2, ns, tm, tn), F32)],
        compiler_params=_params("arbitrary", "arbitrary"),
        name="outproj",
    )(m, w, x, norm_g)


def _conv3(ext, cw):
    return cw[0:1] * pltpu.roll(ext, 2, 0) + cw[1:2] * pltpu.roll(ext, 1, 0) + cw[2:3] * ext


def _window_sum(ext, w):
    s, k = ext, 1
    while k < w:
        s = s + pltpu.roll(s, k, 0)
        k *= 2
    return s


def _pool_group(win, cnt, ug, pw, scale):
    dlt = win / cnt - ug
    return jnp.dot(dlt.astype(BF16), pw.astype(BF16), preferred_element_type=F32) * scale


def _mix_prompt_kernel(hc_ref, gc_ref, gb_ref, u_ref, hch_ref, gch_ref, uh_ref, cw_ref, pw_ref,
                       ps_ref, m_ref, cst_ref, pst_ref, *, chunks_per_seq, n_chunks):
    rows, c = hc_ref.shape
    gw = c // len(POOL_WINDOWS)
    ci = pl.program_id(0) % chunks_per_seq
    first = ci == 0

    @pl.when(pl.program_id(0) >= n_chunks)
    def _():
        m_ref[...] = jnp.zeros(m_ref.shape, BF16)

    @pl.when(pl.program_id(0) < n_chunks)
    def _():
        v = gc_ref[...] * hc_ref[...]
        v_halo = jnp.where(first, 0.0, gch_ref[...] * hch_ref[...])
        y = _conv3(jnp.concatenate([v_halo, v], axis=0), cw_ref[...])[CONV_HALO:]
        m_ref[:, :c] = (gb_ref[...] * y).astype(BF16)
        cst_ref[...] = v[rows - CONV_HALO:]

        u = u_ref[...]
        u_ext = jnp.concatenate([jnp.where(first, 0.0, uh_ref[...]), u], axis=0)
        pst_ref[...] = u_ext[rows:]
        pos1 = ci * rows + lax.broadcasted_iota(jnp.int32, (rows, 1), 0) + 1
        for gi, w in enumerate(POOL_WINDOWS):
            sl = slice(gi * gw, (gi + 1) * gw)
            win = _window_sum(u_ext[:, sl], w)[POOL_HALO:]
            cnt = jnp.minimum(w, pos1).astype(F32)
            yp = _pool_group(win, cnt, u[:, sl], pw_ref[gi], ps_ref[:, sl])
            m_ref[:, c + gi * gw:c + (gi + 1) * gw] = yp.astype(BF16)


def _mix_prompt(z, conv_w, pool_w, pool_scale, n_seq, seq, t_total):
    c = conv_w.shape[1]
    assert z.shape[1] == 4 * c
    t_tail = t_total - n_seq * seq
    rows = max(r for r in range(POOL_HALO, MIX_ROWS_MAX + 1, POOL_HALO) if seq % r == 0 and t_tail % r == 0)
    cps = seq // rows
    n_chunks = n_seq * cps
    rh_c, rh_p = rows // CONV_HALO, rows // POOL_HALO

    def halo(ratio, col):
        return lambda i: (jnp.maximum(i * ratio - 1, 0), col)

    kern = functools.partial(_mix_prompt_kernel, chunks_per_seq=cps, n_chunks=n_chunks)
    gw = c // len(POOL_WINDOWS)
    state = lambda i: (jnp.minimum(i // cps, n_seq - 1), 0, 0)
    return pl.pallas_call(
        kern,
        grid=(n_chunks + t_tail // rows,),
        in_specs=[
            pl.BlockSpec((rows, c), lambda i: (i, 0)),
            pl.BlockSpec((rows, c), lambda i: (i, 1)),
            pl.BlockSpec((rows, c), lambda i: (i, 2)),
            pl.BlockSpec((rows, c), lambda i: (i, 3)),
            pl.BlockSpec((CONV_HALO, c), halo(rh_c, 0)),
            pl.BlockSpec((CONV_HALO, c), halo(rh_c, 1)),
            pl.BlockSpec((POOL_HALO, c), halo(rh_p, 3)),
            pl.BlockSpec((CONV_K, c), lambda i: (0, 0)),
            pl.BlockSpec((len(POOL_WINDOWS), gw, gw), lambda i: (0, 0, 0)),
            pl.BlockSpec((1, c), lambda i: (0, 0)),
        ],
        out_specs=[
            pl.BlockSpec((rows, 2 * c), lambda i: (i, 0)),
            pl.BlockSpec((None, CONV_HALO, c), state),
            pl.BlockSpec((None, POOL_HALO, c), state),
        ],
        out_shape=[
            jax.ShapeDtypeStruct((t_total, 2 * c), BF16),
            jax.ShapeDtypeStruct((n_seq, CONV_HALO, c), F32),
            jax.ShapeDtypeStruct((n_seq, POOL_HALO, c), F32),
        ],
        compiler_params=_params("arbitrary"),
        name="mix_prompt",
    )(z, z, z, z, z, z, z, conv_w, pool_w, pool_scale)


def _mix_sample_kernel(hc_ref, gc_ref, gb_ref, u_ref, cst_in_ref, pst_in_ref, cw_ref, pw_ref,
                       ps_ref, m_in_ref, m_ref, cst_ref, pst_ref, *, seq, past_len):
    del m_in_ref
    rows, c = hc_ref.shape
    n_seq = rows // seq
    gw = c // len(POOL_WINDOWS)

    def with_halo(halo3, x2):
        ext = jnp.concatenate([halo3, x2.reshape(n_seq, seq, c)], axis=1)
        return ext, ext.reshape(n_seq * ext.shape[1], c)

    def body_rows(flat, n_halo):
        return flat.reshape(n_seq, n_halo + seq, flat.shape[-1])[:, n_halo:].reshape(rows, flat.shape[-1])

    v = gc_ref[...] * hc_ref[...]
    v_ext3, v_ext = with_halo(cst_in_ref[...], v)
    y = body_rows(_conv3(v_ext, cw_ref[...]), CONV_HALO)
    m_ref[:, :c] = (gb_ref[...] * y).astype(BF16)
    cst_ref[...] = v_ext3[:, seq:]

    u = u_ref[...]
    u_ext3, u_ext = with_halo(pst_in_ref[...], u)
    pst_ref[...] = u_ext3[:, seq:]
    pos1 = past_len + jnp.bitwise_and(lax.broadcasted_iota(jnp.int32, (rows, 1), 0), seq - 1) + 1
    for gi, w in enumerate(POOL_WINDOWS):
        sl = slice(gi * gw, (gi + 1) * gw)
        win = body_rows(_window_sum(u_ext[:, sl], w), POOL_HALO)
        cnt = jnp.minimum(w, pos1).astype(F32)
        yp = _pool_group(win, cnt, u[:, sl], pw_ref[gi], ps_ref[:, sl])
        m_ref[:, c + gi * gw:c + (gi + 1) * gw] = yp.astype(BF16)


def _mix_sample(z, m, conv_state, pool_state, conv_w, pool_w, pool_scale, t_prompt, seq, past_len):
    c = conv_w.shape[1]
    n_seq = conv_state.shape[0]
    rows = n_seq * seq
    assert seq == SUBLANES and t_prompt % rows == 0
    rb = t_prompt // rows
    gw = c // len(POOL_WINDOWS)
    full3 = lambda i: (0, 0, 0)
    kern = functools.partial(_mix_sample_kernel, seq=seq, past_len=past_len)
    return pl.pallas_call(
        kern,
        grid=(1,),
        in_specs=[
            pl.BlockSpec((rows, c), lambda i: (rb, 0)),
            pl.BlockSpec((rows, c), lambda i: (rb, 1)),
            pl.BlockSpec((rows, c), lambda i: (rb, 2)),
            pl.BlockSpec((rows, c), lambda i: (rb, 3)),
            pl.BlockSpec((n_seq, CONV_HALO, c), full3),
            pl.BlockSpec((n_seq, POOL_HALO, c), full3),
            pl.BlockSpec((CONV_K, c), lambda i: (0, 0)),
            pl.BlockSpec((len(POOL_WINDOWS), gw, gw), full3),
            pl.BlockSpec((1, c), lambda i: (0, 0)),
            pl.BlockSpec(memory_space=pl.ANY),
        ],
        out_specs=[
            pl.BlockSpec((rows, 2 * c), lambda i: (rb, 0)),
            pl.BlockSpec((n_seq, CONV_HALO, c), full3),
            pl.BlockSpec((n_seq, POOL_HALO, c), full3),
        ],
        out_shape=[
            jax.ShapeDtypeStruct(m.shape, m.dtype),
            jax.ShapeDtypeStruct((n_seq, CONV_HALO, c), F32),
            jax.ShapeDtypeStruct((n_seq, POOL_HALO, c), F32),
        ],
        input_output_aliases={9: 0},
        compiler_params=_params("arbitrary"),
        name="mix_sample",
    )(z, z, z, z, conv_state, pool_state, conv_w, pool_w, pool_scale, m)


N_KEYS = 2 * BLOCK


def _t5_bucket(dist):
    max_exact = N_BUCKETS // 2
    n = jnp.maximum(dist, 0)
    ratio = jnp.log(jnp.maximum(n, 1).astype(F32) / max_exact) / math.log(MAX_DISTANCE / max_exact)
    large = jnp.minimum(max_exact + (ratio * (N_BUCKETS - max_exact)).astype(jnp.int32), N_BUCKETS - 1)
    return jnp.where(n < max_exact, n, large)


def _bias_prompt_kernel(rb_ref, o_ref):
    heads, n_keys, n_q = o_ref.shape
    s = lax.broadcasted_iota(jnp.int32, (n_keys, n_q), 0)
    q = lax.broadcasted_iota(jnp.int32, (n_keys, n_q), 1)
    bucket = _t5_bucket(BLOCK + q - s)
    for hh in range(heads):
        head = pl.program_id(0) * heads + hh
        acc = jnp.zeros((n_keys, n_q), F32)
        for b in range(N_BUCKETS):
            acc = jnp.where(bucket == b, rb_ref[b, head], acc)
        o_ref[hh] = acc


def _bias_sample_kernel(rb_ref, o_ref, *, seq, n_heads):
    half = pl.program_id(0)
    s = lax.broadcasted_iota(jnp.int32, o_ref.shape, 0)
    lane = lax.broadcasted_iota(jnp.int32, o_ref.shape, 1)
    bucket = _t5_bucket(BLOCK + jnp.bitwise_and(lane, seq - 1) - s)
    lane1 = lax.broadcasted_iota(jnp.int32, (1, LANES), 1)
    acc = jnp.zeros(o_ref.shape, F32)
    for b in range(N_BUCKETS):
        vec = jnp.zeros((1, LANES), F32)
        for slab in range(n_heads // 2):
            vec = jnp.where((lane1 >= slab * seq) & (lane1 < (slab + 1) * seq), rb_ref[b, 2 * slab + half], vec)
        acc = jnp.where(bucket == b, vec, acc)
    o_ref[...] = acc


def _bias_tables(rel_bias, seq_s):
    n_heads = rel_bias.shape[1]
    assert seq_s == SUBLANES and (n_heads // 2) * seq_s <= LANES
    heads_per_step = math.gcd(n_heads, BIAS_HEADS_PER_STEP)
    prompt = pl.pallas_call(
        _bias_prompt_kernel,
        grid=(n_heads // heads_per_step,),
        in_specs=[pl.BlockSpec(memory_space=pltpu.SMEM)],
        out_specs=pl.BlockSpec((heads_per_step, N_KEYS, BLOCK), lambda h: (h, 0, 0)),
        out_shape=jax.ShapeDtypeStruct((n_heads, N_KEYS, BLOCK), F32),
        compiler_params=_params("arbitrary"),
        name="rel_bias_prompt",
    )(rel_bias)
    sample = pl.pallas_call(
        functools.partial(_bias_sample_kernel, seq=seq_s, n_heads=n_heads),
        grid=(2,),
        in_specs=[pl.BlockSpec(memory_space=pltpu.SMEM)],
        out_specs=pl.BlockSpec((None, N_KEYS, LANES), lambda h: (h, 0, 0)),
        out_shape=jax.ShapeDtypeStruct((2, N_KEYS, LANES), F32),
        compiler_params=_params("arbitrary"),
        name="rel_bias_sample",
    )(rel_bias)
    return prompt, sample


def _stage_kv(k_all, v_all, kpad_ref, vtpad_ref):
    n_keys, kvw = k_all.shape
    lane = lax.broadcasted_iota(jnp.int32, (n_keys, LANES), 1)
    row = lax.broadcasted_iota(jnp.int32, (LANES, n_keys), 0)
    for slab in range(kvw // LANES):
        ks = k_all[:, slab * LANES:(slab + 1) * LANES]
        vt = v_all[:, slab * LANES:(slab + 1) * LANES].T
        for own in range(2):
            kv = 2 * slab + own
            k_own = jnp.where((lane >= own * HEAD_DIM) & (lane < (own + 1) * HEAD_DIM), ks, 0.0)
            v_own = jnp.where((row >= own * HEAD_DIM) & (row < (own + 1) * HEAD_DIM), vt, 0.0)
            kpad_ref[kv, own] = k_own.astype(BF16)
            kpad_ref[kv, 1 - own] = pltpu.roll(k_own, HEAD_DIM, 1).astype(BF16)
            vtpad_ref[kv, own] = v_own.astype(BF16)
            vtpad_ref[kv, 1 - own] = pltpu.roll(v_own, HEAD_DIM, 0).astype(BF16)


_TRANS_B = (((1,), (1,)), ((), ()))


def _attn_prompt_kernel(*refs, blocks_per_seq, n_blocks):
    step = pl.program_id(0)

    @pl.when(step >= n_blocks)
    def _():
        o_ref = refs[7]
        o_ref[...] = jnp.zeros(o_ref.shape, o_ref.dtype)

    @pl.when(step < n_blocks)
    def _():
        _attn_prompt_block(lax.rem(step, blocks_per_seq) == 0, *refs)


def _attn_prompt_block(first, sinks_ref, q_ref, kc_ref, kp_ref, vc_ref, vp_ref, bias_ref, o_ref,
                       kpad_ref, vtpad_ref, s_ref, m_ref):
    n_heads = bias_ref.shape[0]
    group = n_heads // kpad_ref.shape[0]
    _stage_kv(jnp.concatenate([kp_ref[...], kc_ref[...]], axis=0),
              jnp.concatenate([vp_ref[...], vc_ref[...]], axis=0), kpad_ref, vtpad_ref)
    key = lax.broadcasted_iota(jnp.int32, (N_KEYS, BLOCK), 0)
    qry = lax.broadcasted_iota(jnp.int32, (N_KEYS, BLOCK), 1)
    valid = (key >= qry) & (key <= qry + WINDOW) & ((key >= BLOCK) | jnp.logical_not(first))

    for p in range(n_heads // 2):
        q_slab = (q_ref[:, p * LANES:(p + 1) * LANES] * HEAD_DIM ** -0.5).astype(BF16)
        for half in range(2):
            head = 2 * p + half
            s = lax.dot_general(kpad_ref[head // group, half], q_slab, _TRANS_B, preferred_element_type=F32)
            s = jnp.where(valid, s + bias_ref[head], NEG)
            s_ref[head] = s
            m_ref[head:head + 1, :] = jnp.maximum(jnp.max(s, axis=0, keepdims=True), sinks_ref[head])
    first_head = lax.broadcasted_iota(jnp.int32, (LANES, BLOCK), 0) < HEAD_DIM
    for p in range(n_heads // 2):
        kv = 2 * p // group
        probs, inv = [], []
        for head in (2 * p, 2 * p + 1):
            m = m_ref[head:head + 1, :]
            e = jnp.exp(s_ref[head] - m)
            inv.append(1.0 / (jnp.sum(e, axis=0, keepdims=True) + jnp.exp(sinks_ref[head] - m)))
            probs.append(e.astype(BF16))
        acc = (jnp.dot(vtpad_ref[kv, 0], probs[0], preferred_element_type=F32)
               + jnp.dot(vtpad_ref[kv, 1], probs[1], preferred_element_type=F32))
        o_ref[:, p * LANES:(p + 1) * LANES] = (acc * jnp.where(first_head, inv[0], inv[1])).T.astype(o_ref.dtype)


def _attn_prompt(qkv, bias, sinks, n_seq, seq, n_heads, n_kv, t_total):
    d = n_heads * HEAD_DIM
    kvw = n_kv * HEAD_DIM
    assert kvw % LANES == 0 and d % kvw == 0 and seq % BLOCK == 0
    nb = seq // BLOCK
    kcol, vcol = d // kvw, d // kvw + 1
    n_blocks = n_seq * nb
    n_tail = pl.cdiv(t_total - n_seq * seq, BLOCK)
    cur = lambda col: (lambda s: (s, col))
    prev = lambda col: (lambda s: (jnp.maximum(s - 1, 0), col))
    return pl.pallas_call(
        functools.partial(_attn_prompt_kernel, blocks_per_seq=nb, n_blocks=n_blocks),
        grid=(n_blocks + n_tail,),
        in_specs=[
            pl.BlockSpec(memory_space=pltpu.SMEM),
            pl.BlockSpec((BLOCK, d), cur(0)),
            pl.BlockSpec((BLOCK, kvw), cur(kcol)),
            pl.BlockSpec((BLOCK, kvw), prev(kcol)),
            pl.BlockSpec((BLOCK, kvw), cur(vcol)),
            pl.BlockSpec((BLOCK, kvw), prev(vcol)),
            pl.BlockSpec((n_heads, N_KEYS, BLOCK), lambda s: (0, 0, 0), pipeline_mode=pl.Buffered(1)),
        ],
        out_specs=pl.BlockSpec((BLOCK, d), cur(0)),
        out_shape=jax.ShapeDtypeStruct((t_total, d), BF16),
        scratch_shapes=[
            pltpu.VMEM((n_kv, 2, N_KEYS, LANES), BF16),
            pltpu.VMEM((n_kv, 2, LANES, N_KEYS), BF16),
            pltpu.VMEM((n_heads, N_KEYS, BLOCK), F32),
            pltpu.VMEM((n_heads, BLOCK), F32),
        ],
        compiler_params=_params("arbitrary"),
        name="attn_prompt",
    )(sinks, qkv, qkv, qkv, qkv, qkv, bias)


def _attn_sample_kernel(sinks_ref, q_ref, kn_ref, vn_ref, kc_ref, vc_ref, bias_ref, att_in_ref, o_ref,
                        kpad_ref, vtpad_ref):
    del att_in_ref
    seq, kvw = kn_ref.shape
    n_pairs = q_ref.shape[1] // LANES
    n_kv = kvw // HEAD_DIM
    kv_lanes = n_pairs // n_kv * seq
    pad = jnp.zeros((N_KEYS - kc_ref.shape[0] - seq, kvw), F32)
    cached = lambda ref: jnp.concatenate([ref[:, kv, :] for kv in range(n_kv)], axis=1)
    _stage_kv(jnp.concatenate([cached(kc_ref), kn_ref[...], pad], axis=0),
              jnp.concatenate([cached(vc_ref), vn_ref[...], pad], axis=0), kpad_ref, vtpad_ref)

    rows = [q_ref[:, p * LANES:(p + 1) * LANES] for p in range(n_pairs)]
    if n_pairs * seq < LANES:
        rows.append(jnp.zeros((LANES - n_pairs * seq, LANES), F32))
    qs = (jnp.concatenate(rows, axis=0) * HEAD_DIM ** -0.5).astype(BF16)

    key = lax.broadcasted_iota(jnp.int32, (N_KEYS, LANES), 0)
    lane = lax.broadcasted_iota(jnp.int32, (N_KEYS, LANES), 1)
    lane1 = lax.broadcasted_iota(jnp.int32, (1, LANES), 1)
    qry = jnp.bitwise_and(lane, seq - 1)
    valid = (key >= qry) & (key <= qry + WINDOW)

    probs, dens = [], []
    for half in range(2):
        s = jnp.zeros((N_KEYS, LANES), F32)
        for kv in range(n_kv):
            s_kv = lax.dot_general(kpad_ref[kv, half], qs, _TRANS_B, preferred_element_type=F32)
            s = jnp.where((lane >= kv * kv_lanes) & (lane < (kv + 1) * kv_lanes), s_kv, s)
        sink = jnp.zeros((1, LANES), F32)
        for p in range(n_pairs):
            sink = jnp.where((lane1 >= p * seq) & (lane1 < (p + 1) * seq), sinks_ref[2 * p + half], sink)
        s = jnp.where(valid, s + bias_ref[half], NEG)
        m = jnp.maximum(jnp.max(s, axis=0, keepdims=True), sink)
        e = jnp.exp(s - m)
        dens.append(jnp.sum(e, axis=0, keepdims=True) + jnp.exp(sink - m))
        probs.append(e.astype(BF16))

    row_t = lax.broadcasted_iota(jnp.int32, (LANES, LANES), 0)
    lane_t = lax.broadcasted_iota(jnp.int32, (LANES, LANES), 1)
    out_t = jnp.zeros((LANES, LANES), F32)
    for kv in range(n_kv):
        acc = (jnp.dot(vtpad_ref[kv, 0], probs[0], preferred_element_type=F32)
               + jnp.dot(vtpad_ref[kv, 1], probs[1], preferred_element_type=F32))
        out_t = jnp.where((lane_t >= kv * kv_lanes) & (lane_t < (kv + 1) * kv_lanes), acc, out_t)
    inv = jnp.where(row_t < HEAD_DIM, 1.0 / dens[0], 1.0 / dens[1])
    out = (out_t * inv).T
    for p in range(n_pairs):
        o_ref[:, p * LANES:(p + 1) * LANES] = out[p * seq:(p + 1) * seq].astype(o_ref.dtype)


def _attn_sample(qkv, att, cache_k, cache_v, layer, bias, sinks, t_prompt, seq, n_heads):
    n_seq, kv_buf, n_kv = cache_k.shape[1:4]
    kvw = n_kv * HEAD_DIM
    d = n_heads * HEAD_DIM
    assert kv_buf == BLOCK and seq == SUBLANES and t_prompt % seq == 0
    cache_spec = pl.BlockSpec((None, None, kv_buf, n_kv, HEAD_DIM), lambda n: (layer, n, 0, 0, 0))
    rb = t_prompt // seq
    kcol, vcol = d // kvw, d // kvw + 1
    return pl.pallas_call(
        _attn_sample_kernel,
        grid=(n_seq,),
        in_specs=[
            pl.BlockSpec(memory_space=pltpu.SMEM),
            pl.BlockSpec((seq, d), lambda n: (rb + n, 0)),
            pl.BlockSpec((seq, kvw), lambda n: (rb + n, kcol)),
            pl.BlockSpec((seq, kvw), lambda n: (rb + n, vcol)),
            cache_spec,
            cache_spec,
            pl.BlockSpec((2, N_KEYS, LANES), lambda n: (0, 0, 0)),
            pl.BlockSpec(memory_space=pl.ANY),
        ],
        out_specs=pl.BlockSpec((seq, d), lambda n: (rb + n, 0)),
        out_shape=jax.ShapeDtypeStruct(att.shape, att.dtype),
        scratch_shapes=[
            pltpu.VMEM((kvw // HEAD_DIM, 2, N_KEYS, LANES), BF16),
            pltpu.VMEM((kvw // HEAD_DIM, 2, LANES, N_KEYS), BF16),
        ],
        input_output_aliases={7: 0},
        compiler_params=_params("arbitrary"),
        name="attn_sample",
    )(sinks, qkv, qkv, qkv, cache_k, cache_v, bias, att)


def kernel(x_prompt, x_sample, state_conv, state_pool, cache_k, cache_v, norm_g, w_ffn_in,
           w_ffn_out, w_mix_in, conv_w, pool_w, pool_scale, w_mix_out, w_qkv, w_o,
           attn_sinks, rel_bias):
    n_p, seq_p, d = x_prompt.shape
    n_s, seq_s, _ = x_sample.shape
    depth = norm_g.shape[0]
    n_heads = d // HEAD_DIM
    n_kv = cache_k.shape[3]
    kv_buf = cache_k.shape[2]
    kvw = n_kv * HEAD_DIM
    conv_ctx = state_conv.shape[2]
    pool_ctx = state_pool.shape[2]
    t_p, t_s = n_p * seq_p, n_s * seq_s
    t = t_p + t_s
    tm = _divisor_tile(t, ROW_TILE, BF16_ROWS)
    tm_proj = _divisor_tile(t, PROJ_ROW_TILE, BF16_ROWS)
    d_ff = w_ffn_out.shape[2]
    ffn_tiles = _ffn_tiling(t, d_ff)
    last_ffn_tiles = _ffn_tiling(t, d_ff, align_rows=(t_p, t_s))
    x, h = _merge(x_prompt.reshape(t_p, d), x_sample.reshape(t_s, d), norm_g, 0, FFN1_PRE)
    bias_p, bias_s = _bias_tables(rel_bias, seq_s)
    out_tile = _divisor_tile(d, COL_TILE, LANES)
    w_mix_out = _column_tiles_bf16(w_mix_out, out_tile)
    w_o = _column_tiles_bf16(w_o, out_tile)

    conv_p, pool_p, k_p, v_p, conv_s, pool_s, k_s, v_s = ([] for _ in range(8))
    for i in range(depth):
        x, h = _ffn(h, x, norm_g, w_ffn_in, w_ffn_out, i, 0, *ffn_tiles, nxt=(i, MIX_PRE))
        j = i // 2
        if i % 2 == 0:
            z = _proj(h, w_mix_in, j, tm_proj)
            m, cst, pst = _mix_prompt(z, conv_w[j], pool_w[j], pool_scale[j][None], n_p, seq_p, t)
            conv_p.append(cst[:, CONV_HALO - conv_ctx:])
            pool_p.append(pst[:, POOL_HALO - pool_ctx:])
            cst_in = jnp.pad(state_conv[j], ((0, 0), (CONV_HALO - conv_ctx, 0), (0, 0)))
            pst_in = jnp.pad(state_pool[j], ((0, 0), (POOL_HALO - pool_ctx, 0), (0, 0)))
            m, cst, pst = _mix_sample(z, m, cst_in, pst_in, conv_w[j], pool_w[j], pool_scale[j][None],
                                      t_p, seq_s, PAST_LEN)
            conv_s.append(cst[:, CONV_HALO - conv_ctx:])
            pool_s.append(pst[:, POOL_HALO - pool_ctx:])
            x, h = _out(m, w_mix_out, j, x, norm_g, i, tm)
        else:
            qkv = _proj(h, w_qkv, j, tm_proj)
            att = _attn_prompt(qkv, bias_p, attn_sinks[j], n_p, seq_p, n_heads, n_kv, t)
            att = _attn_sample(qkv, att, cache_k, cache_v, j, bias_s, attn_sinks[j], t_p, seq_s, n_heads)
            k_new, v_new = qkv[:, d:d + kvw], qkv[:, d + kvw:]
            kv_p = lambda a: a[:t_p].reshape(n_p, seq_p, n_kv, HEAD_DIM)[:, seq_p - kv_buf:]
            k_p.append(kv_p(k_new))
            v_p.append(kv_p(v_new))
            kv_s = lambda c, a: jnp.concatenate(
                [c[:, seq_s:], a[t_p:].reshape(n_s, seq_s, n_kv, HEAD_DIM)], axis=1)
            k_s.append(kv_s(cache_k[j], k_new))
            v_s.append(kv_s(cache_v[j], v_new))
            x, h = _out(att, w_o, j, x, norm_g, i, tm)
        if i + 1 < depth:
            x, h = _ffn(h, x, norm_g, w_ffn_in, w_ffn_out, i, 1, *ffn_tiles, nxt=(i + 1, FFN1_PRE))
        else:
            y_p, y_s = _ffn(h, x, norm_g, w_ffn_in, w_ffn_out, i, 1, *last_ffn_tiles, head_rows=t_p)

    y_p = y_p.reshape(n_p, seq_p, d)
    y_s = y_s.reshape(n_s, seq_s, d)
    st = jnp.stack
    return (y_p, y_s, st(conv_p), st(pool_p), st(k_p), st(v_p), st(conv_s), st(pool_s), st(k_s), st(v_s))
```

```python
import functools
import math

import jax
import jax.numpy as jnp
from jax import lax
from jax.experimental import pallas as pl
from jax.experimental.pallas import tpu as pltpu

F32 = jnp.float32
BF16 = jnp.bfloat16

EPS = 1e-6
NEG = -1e30
HEAD_DIM = 64
WINDOW = 128
BLOCK = WINDOW
CONV_K = 3
POOL_WINDOWS = (2, 4, 8, 16)
N_BUCKETS = 32
MAX_DISTANCE = 128
PAST_LEN = 16384

LANES = 128
SUBLANES = 8
BF16_ROWS = 2 * SUBLANES
COL_TILE = 4 * LANES
CONV_HALO = SUBLANES
POOL_HALO = 2 * SUBLANES
MIX_ROWS_MAX = 256
NORM_ROWS_MAX = 176
BIAS_HEADS_PER_STEP = 8
FFN_TILINGS = ((1408, 2 * LANES), (1056, 2 * LANES))
FINISH_ROWS_MAX = 64
VMEM_LIMIT_BYTES = 58 * 1024 * 1024
ROW_TILE = 1056
PROJ_ROW_TILE = 2112


def _divisor_tile(n, target, mult):
    best = None
    for t in range(mult, min(n, target) + 1, mult):
        if n % t == 0:
            best = t
    assert best is not None, (n, target, mult)
    return best


def _params(*sem):
    return pltpu.CompilerParams(dimension_semantics=sem, vmem_limit_bytes=VMEM_LIMIT_BYTES)


def _rms(xf, g):
    ms = jnp.mean(xf * xf, axis=-1, keepdims=True)
    return xf * lax.rsqrt(ms + EPS) * g


def _for_row_chunks(n_rows, body):
    rc = _divisor_tile(n_rows, NORM_ROWS_MAX, BF16_ROWS)

    def step(c, carry):
        body(pl.ds(pl.multiple_of(c * rc, rc), rc))
        return carry

    lax.fori_loop(0, n_rows // rc, step, 0)


def _merge_kernel(xp_ref, xs_ref, g_ref, h_ref, *, n_prompt_blocks, row):
    x = jnp.where(pl.program_id(0) < n_prompt_blocks, xp_ref[...], xs_ref[...])
    h_ref[...] = _rms(x, g_ref[row:row + 1, :]).astype(BF16)


def _merge(x_prompt, x_sample, norm_g, layer, row):
    (t_p, d), t_s = x_prompt.shape, x_sample.shape[0]
    assert t_p % t_s == 0 and t_s % BF16_ROWS == 0
    nbp = t_p // t_s
    row_block = pl.BlockSpec((t_s, d), lambda i: (i, 0))
    return pl.pallas_call(
        functools.partial(_merge_kernel, n_prompt_blocks=nbp, row=row),
        grid=(nbp + 1,),
        in_specs=[
            pl.BlockSpec((t_s, d), lambda i: (jnp.minimum(i, nbp - 1), 0)),
            pl.BlockSpec((t_s, d), lambda i: (0, 0)),
            pl.BlockSpec((None,) + norm_g.shape[1:], lambda i: (layer, 0, 0)),
        ],
        out_specs=row_block,
        out_shape=jax.ShapeDtypeStruct((t_p + t_s, d), BF16),
        compiler_params=_params("parallel"),
        name="merge_rows",
    )(x_prompt, x_sample, norm_g)


def _ffn_step(step, nb, nj):
    real = step < nb * nj
    return jnp.where(real, step // nj, nb), jnp.where(real, step % nj, step - nb * nj)


def _ffn_kernel(h_ref, xr_ref, *rest, d_ff, ff_tile, nb, nc, post, nxt, n_head_chunks, n_res_head_chunks):
    assert nxt is None or n_head_chunks is None
    xr2_ref = None
    if n_res_head_chunks is not None:
        xr2_ref, rest = rest[0], rest[1:]
    (g_ref, gn_ref, wg_ref), rest = rest[:3], rest[3:]
    pieces = ff_tile // LANES
    wu_refs, wo_refs, rest = rest[:pieces], rest[pieces:2 * pieces], rest[2 * pieces:]
    xo_ref, rest = rest[0], rest[1:]
    o2_ref, acc_ref = rest if len(rest) == 2 else (None,) + rest
    i, j = _ffn_step(pl.program_id(0), nb, pl.cdiv(d_ff, ff_tile))
    tm, d = h_ref.shape
    rc = xr_ref.shape[0]
    cur = lax.rem(i, 2)
    prv = 1 - cur

    @pl.when(pl.program_id(0) == 0)
    def _():
        def zero(rows):
            acc_ref[0, rows, :] = jnp.zeros((rows.size, d), F32)
            acc_ref[1, rows, :] = jnp.zeros((rows.size, d), F32)
        _for_row_chunks(tm, zero)

    n_full = d_ff // ff_tile
    end_pieces = (d_ff - n_full * ff_tile) // LANES
    wc = _divisor_tile(d, COL_TILE, LANES)
    row_halves = [slice(0, tm // 2), slice(tm // 2, tm)]

    def swiglu(row_sets, w, wo):
        width = wo.shape[0]
        hidden = []
        for rows in row_sets:
            r = jnp.dot(h_ref[rows, :], w, preferred_element_type=F32)
            gate, up = r[:, :width], r[:, width:]
            hidden.append(((gate * jax.nn.sigmoid(gate)) * up).astype(BF16))
        for rows, a in zip(row_sets, hidden):
            for c in range(d // wc):
                acc_ref[cur, rows, c * wc:(c + 1) * wc] += jnp.dot(a, wo[:, c * wc:(c + 1) * wc],
                                                                   preferred_element_type=F32)

    def tile(n_pieces=pieces):
        w = jnp.concatenate([wg_ref[:, :n_pieces * LANES].astype(BF16)]
                            + [r[...].astype(BF16) for r in wu_refs[:n_pieces]], axis=1)
        wo = jnp.concatenate([r[...].astype(BF16) for r in wo_refs[:n_pieces]], axis=0)
        swiglu(row_halves, w, wo)

    def end_tile():
        tile(end_pieces)

    def finish_chunk(then=None):
        rows = pl.ds(pl.multiple_of(j * rc, rc), rc)
        res = xr_ref[...]
        if xr2_ref is not None:
            res = jnp.where((i - 1) * nc + j < n_res_head_chunks, res, xr2_ref[...])
        y = res + 0.5 * _rms(acc_ref[prv, rows, :], g_ref[post:post + 1, :])
        acc_ref[prv, rows, :] = jnp.zeros((rc, d), F32)
        if n_head_chunks is None:
            xo_ref[...] = y
            if nxt is not None:
                o2_ref[...] = _rms(y, gn_ref[nxt:nxt + 1, :]).astype(BF16)
        if then is not None:
            then()
        if n_head_chunks is not None:
            in_head = (i - 1) * nc + j < n_head_chunks

            @pl.when(in_head)
            def _():
                xo_ref[...] = y

            @pl.when(jnp.logical_not(in_head))
            def _():
                o2_ref[...] = y

    real = i < nb

    @pl.when(real & (j < nc) & (j < n_full))
    def _():
        finish_chunk(tile)

    @pl.when(real & (j >= nc) & (j < n_full))
    def _():
        tile()

    if end_pieces:
        @pl.when(real & (j == n_full))
        def _():
            if n_full < nc:
                finish_chunk(end_tile)
            else:
                end_tile()

    @pl.when(jnp.logical_not(real) & (j < nc))
    def _():
        finish_chunk()


def _finish_rows(tm, n_steps, align_rows=()):
    fits = [c for c in range(BF16_ROWS, FINISH_ROWS_MAX + 1, BF16_ROWS)
            if tm % c == 0 and tm // c <= n_steps and all(r % c == 0 for r in align_rows)]
    return max(fits) if fits else None


def _ffn_tiling(t, d_ff, align_rows=()):
    for tm_target, ff_tile in FFN_TILINGS:
        tm = _divisor_tile(t, tm_target, BF16_ROWS)
        if _finish_rows(tm, pl.cdiv(d_ff, ff_tile), align_rows) is not None:
            return tm, ff_tile
    raise ValueError((t, d_ff, align_rows))


def _ffn(h, x, norm_g, w_in, w_out, layer, which, tm, ff_tile, nxt=None, head_rows=None):
    xs = x if isinstance(x, tuple) else (x,)
    t, d = h.shape
    assert sum(a.shape[0] for a in xs) == t and len(xs) <= 2
    d_ff = w_out.shape[2]
    pieces = ff_tile // LANES
    assert ff_tile % LANES == 0 and d_ff % LANES == 0 and w_in.shape[2:] == (d, 2 * d_ff) and t % tm == 0
    n_pieces = d_ff // LANES
    nj = pl.cdiv(d_ff, ff_tile)
    nb = t // tm
    split_rows = (() if head_rows is None else (head_rows, t - head_rows)) + tuple(a.shape[0] for a in xs[:-1])
    rc = _finish_rows(tm, nj, split_rows)
    nc = tm // rc
    nxt_layer, nxt_row = nxt if nxt is not None else (layer, None)
    n_head = None
    if head_rows is not None:
        assert nxt is None
        n_head = head_rows // rc
    n_res_head = xs[0].shape[0] // rc if len(xs) == 2 else None
    kern = functools.partial(_ffn_kernel, d_ff=d_ff, ff_tile=ff_tile, nb=nb, nc=nc, post=4 * which + 1,
                             nxt=nxt_row, n_head_chunks=n_head, n_res_head_chunks=n_res_head)
    spec = lambda shape, ij_map: pl.BlockSpec(shape, lambda step: ij_map(*_ffn_step(step, nb, nj)))
    jw = lambda i, j: jnp.where(i < nb, j, nj - 1)
    w_in_spec = lambda width, col: spec((None, None, d, width), lambda i, j: (layer, which, 0, col(jw(i, j))))
    w_out_spec = lambda row: spec((None, None, LANES, d), lambda i, j: (layer, which, row(jw(i, j)), 0))
    up_piece = lambda q: (lambda j: jnp.minimum(n_pieces + pieces * j + q, 2 * n_pieces - 1))
    down_piece = lambda q: (lambda j: jnp.minimum(pieces * j + q, n_pieces - 1))
    chunk = lambda i, j: (jnp.maximum((i - 1) * nc + jnp.minimum(j, nc - 1), 0), 0)
    g_spec = lambda l: spec((None,) + norm_g.shape[1:], lambda i, j: (l, 0, 0))
    if nxt is not None:
        out_specs = [spec((rc, d), chunk), spec((rc, d), chunk)]
        out_shape = [jax.ShapeDtypeStruct((t, d), F32), jax.ShapeDtypeStruct((t, d), BF16)]
    elif n_head is not None:
        out_specs = [spec((rc, d), lambda i, j: (jnp.minimum(chunk(i, j)[0], n_head - 1), 0)),
                     spec((rc, d), lambda i, j: (jnp.maximum(chunk(i, j)[0] - n_head, 0), 0))]
        out_shape = [jax.ShapeDtypeStruct((head_rows, d), F32), jax.ShapeDtypeStruct((t - head_rows, d), F32)]
    else:
        out_specs = [spec((rc, d), chunk)]
        out_shape = [jax.ShapeDtypeStruct((t, d), F32)]
    if n_res_head is None:
        res_specs = [spec((rc, d), chunk)]
    else:
        res_specs = [spec((rc, d), lambda i, j: (jnp.minimum(chunk(i, j)[0], n_res_head - 1), 0)),
                     spec((rc, d), lambda i, j: (jnp.maximum(chunk(i, j)[0] - n_res_head, 0), 0))]
    res = pl.pallas_call(
        kern,
        grid=(nb * nj + nc,),
        in_specs=[
            spec((tm, d), lambda i, j: (jnp.minimum(i, nb - 1), 0)),
            *res_specs,
            g_spec(layer),
            g_spec(nxt_layer),
            w_in_spec(ff_tile, lambda j: j),
            *[w_in_spec(LANES, up_piece(q)) for q in range(pieces)],
            *[w_out_spec(down_piece(q)) for q in range(pieces)],
        ],
        out_specs=out_specs,
        out_shape=out_shape,
        scratch_shapes=[pltpu.VMEM((2, tm, d), F32)],
        compiler_params=_params("arbitrary"),
        name="ffn",
    )(h, *xs, norm_g, norm_g, w_in, *([w_in] * pieces), *([w_out] * pieces))
    return (res[0], res[1]) if len(res) == 2 else (res[0], None)


FFN1_PRE, MIX_PRE, MIX_POST, FFN2_PRE = 0, 2, 3, 4


def _proj_kernel(h_ref, w_ref, o_ref):
    o_ref[...] = jnp.dot(h_ref[...], w_ref[...].astype(BF16), preferred_element_type=F32)


def _proj(h, w, widx, tm):
    t, d = h.shape
    n = w.shape[2]
    tn = _divisor_tile(n, COL_TILE, LANES)
    return pl.pallas_call(
        _proj_kernel,
        grid=(t // tm, n // tn),
        in_specs=[
            pl.BlockSpec((tm, d), lambda i, j: (i, 0)),
            pl.BlockSpec((None, d, tn), lambda i, j: (widx, 0, j)),
        ],
        out_specs=pl.BlockSpec((tm, tn), lambda i, j: (i, j)),
        out_shape=jax.ShapeDtypeStruct((t, n), F32),
        compiler_params=_params("parallel", "arbitrary"),
        name="proj",
    )(h, w)


def _column_tiles_kernel(w_ref, o_ref):
    o_ref[...] = w_ref[...].astype(BF16)


def _column_tiles_bf16(w, tn):
    n, kdim, d = w.shape
    return pl.pallas_call(
        _column_tiles_kernel,
        grid=(n, d // tn),
        in_specs=[pl.BlockSpec((None, kdim, tn), lambda a, j: (a, 0, j))],
        out_specs=pl.BlockSpec((None, None, kdim, tn), lambda a, j: (a, j, 0, 0)),
        out_shape=jax.ShapeDtypeStruct((n, d // tn, kdim, tn), BF16),
        compiler_params=_params("parallel", "parallel"),
        name="column_tiles_bf16",
    )(w)


def _out_kernel(m_ref, w_ref, xr_ref, g_ref, xo_ref, ho_ref, acc_ref, *, nb, nc):
    i, j = pl.program_id(0), pl.program_id(1)
    ns, tm, tn = acc_ref.shape[1:]
    rc = xr_ref.shape[0]
    cur = lax.rem(i, 2)
    prv = 1 - cur

    @pl.when((i == 0) & (j == 0))
    def _():
        acc_ref[1] = jnp.zeros(acc_ref.shape[1:], F32)

    def tile():
        acc_ref[cur, j] = jnp.dot(m_ref[...], w_ref[j], preferred_element_type=F32)

    def finish_chunk():
        sub = _divisor_tile(rc, NORM_ROWS_MAX, BF16_ROWS)
        for r in range(rc // sub):
            rows = pl.ds(pl.multiple_of(j * rc + r * sub, BF16_ROWS), sub)
            o = jnp.concatenate([acc_ref[prv, s, rows, :] for s in range(ns)], axis=1)
            y = xr_ref[r * sub:(r + 1) * sub, :] + _rms(o, g_ref[MIX_POST:MIX_POST + 1, :])
            xo_ref[r * sub:(r + 1) * sub, :] = y
            ho_ref[r * sub:(r + 1) * sub, :] = _rms(y, g_ref[FFN2_PRE:FFN2_PRE + 1, :]).astype(BF16)

    real = i < nb

    @pl.when(real & (j < nc))
    def _():
        finish_chunk()
        tile()

    @pl.when(real & (j >= nc))
    def _():
        tile()

    @pl.when(jnp.logical_not(real) & (j < nc))
    def _():
        finish_chunk()


def _out(m, w, widx, x, norm_g, layer, tm):
    t, d = x.shape
    ns, kdim, tn = w.shape[1:]
    assert ns * tn == d
    nb = t // tm
    rc = min(c for c in range(BF16_ROWS, tm + 1, BF16_ROWS) if tm % c == 0 and tm // c <= ns)
    nc = tm // rc
    chunk = pl.BlockSpec((rc, d), lambda i, j: (jnp.maximum((i - 1) * nc + jnp.minimum(j, nc - 1), 0), 0))
    return pl.pallas_call(
        functools.partial(_out_kernel, nb=nb, nc=nc),
        grid=(nb + 1, ns),
        in_specs=[
            pl.BlockSpec((tm, kdim), lambda i, j: (jnp.minimum(i, nb - 1), 0)),
            pl.BlockSpec((None, ns, kdim, tn), lambda i, j: (widx, 0, 0, 0), pipeline_mode=pl.Buffered(1)),
            chunk,
            pl.BlockSpec((None,) + norm_g.shape[1:], lambda i, j: (layer, 0, 0)),
        ],
        out_specs=[chunk, chunk],
        out_shape=[jax.ShapeDtypeStruct((t, d), F32), jax.ShapeDtypeStruct((t, d), BF16)],
        scratch_shapes=[pltpu.VMEM((2, ns, tm, tn), F32)],
        compiler_params=_params("arbitrary", "arbitrary"),
        name="outproj",
    )(m, w, x, norm_g)


def _conv3(ext, cw):
    return cw[0:1] * pltpu.roll(ext, 2, 0) + cw[1:2] * pltpu.roll(ext, 1, 0) + cw[2:3] * ext


def _window_sum(ext, w):
    s, k = ext, 1
    while k < w:
        s = s + pltpu.roll(s, k, 0)
        k *= 2
    return s


def _pool_group(win, cnt, ug, pw, scale):
    dlt = win / cnt - ug
    return jnp.dot(dlt.astype(BF16), pw.astype(BF16), preferred_element_type=F32) * scale


def _mix_prompt_kernel(hc_ref, gc_ref, gb_ref, u_ref, hch_ref, gch_ref, uh_ref, cw_ref, pw_ref,
                       ps_ref, m_ref, cst_ref, pst_ref, *, chunks_per_seq, n_chunks):
    rows, c = hc_ref.shape
    gw = c // len(POOL_WINDOWS)
    ci = pl.program_id(0) % chunks_per_seq
    first = ci == 0

    @pl.when(pl.program_id(0) >= n_chunks)
    def _():
        m_ref[...] = jnp.zeros(m_ref.shape, BF16)

    @pl.when(pl.program_id(0) < n_chunks)
    def _():
        v = gc_ref[...] * hc_ref[...]
        v_halo = jnp.where(first, 0.0, gch_ref[...] * hch_ref[...])
        y = _conv3(jnp.concatenate([v_halo, v], axis=0), cw_ref[...])[CONV_HALO:]
        m_ref[:, :c] = (gb_ref[...] * y).astype(BF16)
        cst_ref[...] = v[rows - CONV_HALO:]

        u = u_ref[...]
        u_ext = jnp.concatenate([jnp.where(first, 0.0, uh_ref[...]), u], axis=0)
        pst_ref[...] = u_ext[rows:]
        pos1 = ci * rows + lax.broadcasted_iota(jnp.int32, (rows, 1), 0) + 1
        for gi, w in enumerate(POOL_WINDOWS):
            sl = slice(gi * gw, (gi + 1) * gw)
            win = _window_sum(u_ext[:, sl], w)[POOL_HALO:]
            cnt = jnp.minimum(w, pos1).astype(F32)
            yp = _pool_group(win, cnt, u[:, sl], pw_ref[gi], ps_ref[:, sl])
            m_ref[:, c + gi * gw:c + (gi + 1) * gw] = yp.astype(BF16)


def _mix_prompt(z, conv_w, pool_w, pool_scale, n_seq, seq, t_total):
    c = conv_w.shape[1]
    assert z.shape[1] == 4 * c
    t_tail = t_total - n_seq * seq
    rows = max(r for r in range(POOL_HALO, MIX_ROWS_MAX + 1, POOL_HALO) if seq % r == 0 and t_tail % r == 0)
    cps = seq // rows
    n_chunks = n_seq * cps
    rh_c, rh_p = rows // CONV_HALO, rows // POOL_HALO

    def halo(ratio, col):
        return lambda i: (jnp.maximum(i * ratio - 1, 0), col)

    kern = functools.partial(_mix_prompt_kernel, chunks_per_seq=cps, n_chunks=n_chunks)
    gw = c // len(POOL_WINDOWS)
    state = lambda i: (jnp.minimum(i // cps, n_seq - 1), 0, 0)
    return pl.pallas_call(
        kern,
        grid=(n_chunks + t_tail // rows,),
        in_specs=[
            pl.BlockSpec((rows, c), lambda i: (i, 0)),
            pl.BlockSpec((rows, c), lambda i: (i, 1)),
            pl.BlockSpec((rows, c), lambda i: (i, 2)),
            pl.BlockSpec((rows, c), lambda i: (i, 3)),
            pl.BlockSpec((CONV_HALO, c), halo(rh_c, 0)),
            pl.BlockSpec((CONV_HALO, c), halo(rh_c, 1)),
            pl.BlockSpec((POOL_HALO, c), halo(rh_p, 3)),
            pl.BlockSpec((CONV_K, c), lambda i: (0, 0)),
            pl.BlockSpec((len(POOL_WINDOWS), gw, gw), lambda i: (0, 0, 0)),
            pl.BlockSpec((1, c), lambda i: (0, 0)),
        ],
        out_specs=[
            pl.BlockSpec((rows, 2 * c), lambda i: (i, 0)),
            pl.BlockSpec((None, CONV_HALO, c), state),
            pl.BlockSpec((None, POOL_HALO, c), state),
        ],
        out_shape=[
            jax.ShapeDtypeStruct((t_total, 2 * c), BF16),
            jax.ShapeDtypeStruct((n_seq, CONV_HALO, c), F32),
            jax.ShapeDtypeStruct((n_seq, POOL_HALO, c), F32),
        ],
        compiler_params=_params("arbitrary"),
        name="mix_prompt",
    )(z, z, z, z, z, z, z, conv_w, pool_w, pool_scale)


def _mix_sample_kernel(hc_ref, gc_ref, gb_ref, u_ref, cst_in_ref, pst_in_ref, cw_ref, pw_ref,
                       ps_ref, m_in_ref, m_ref, cst_ref, pst_ref, *, seq, past_len):
    del m_in_ref
    rows, c = hc_ref.shape
    n_seq = rows // seq
    gw = c // len(POOL_WINDOWS)

    def with_halo(halo3, x2):
        ext = jnp.concatenate([halo3, x2.reshape(n_seq, seq, c)], axis=1)
        return ext, ext.reshape(n_seq * ext.shape[1], c)

    def body_rows(flat, n_halo):
        return flat.reshape(n_seq, n_halo + seq, flat.shape[-1])[:, n_halo:].reshape(rows, flat.shape[-1])

    v = gc_ref[...] * hc_ref[...]
    v_ext3, v_ext = with_halo(cst_in_ref[...], v)
    y = body_rows(_conv3(v_ext, cw_ref[...]), CONV_HALO)
    m_ref[:, :c] = (gb_ref[...] * y).astype(BF16)
    cst_ref[...] = v_ext3[:, seq:]

    u = u_ref[...]
    u_ext3, u_ext = with_halo(pst_in_ref[...], u)
    pst_ref[...] = u_ext3[:, seq:]
    pos1 = past_len + jnp.bitwise_and(lax.broadcasted_iota(jnp.int32, (rows, 1), 0), seq - 1) + 1
    for gi, w in enumerate(POOL_WINDOWS):
        sl = slice(gi * gw, (gi + 1) * gw)
        win = body_rows(_window_sum(u_ext[:, sl], w), POOL_HALO)
        cnt = jnp.minimum(w, pos1).astype(F32)
        yp = _pool_group(win, cnt, u[:, sl], pw_ref[gi], ps_ref[:, sl])
        m_ref[:, c + gi * gw:c + (gi + 1) * gw] = yp.astype(BF16)


def _mix_sample(z, m, conv_state, pool_state, conv_w, pool_w, pool_scale, t_prompt, seq, past_len):
    c = conv_w.shape[1]
    n_seq = conv_state.shape[0]
    rows = n_seq * seq
    assert seq == SUBLANES and t_prompt % rows == 0
    rb = t_prompt // rows
    gw = c // len(POOL_WINDOWS)
    full3 = lambda i: (0, 0, 0)
    kern = functools.partial(_mix_sample_kernel, seq=seq, past_len=past_len)
    return pl.pallas_call(
        kern,
        grid=(1,),
        in_specs=[
            pl.BlockSpec((rows, c), lambda i: (rb, 0)),
            pl.BlockSpec((rows, c), lambda i: (rb, 1)),
            pl.BlockSpec((rows, c), lambda i: (rb, 2)),
            pl.BlockSpec((rows, c), lambda i: (rb, 3)),
            pl.BlockSpec((n_seq, CONV_HALO, c), full3),
            pl.BlockSpec((n_seq, POOL_HALO, c), full3),
            pl.BlockSpec((CONV_K, c), lambda i: (0, 0)),
            pl.BlockSpec((len(POOL_WINDOWS), gw, gw), full3),
            pl.BlockSpec((1, c), lambda i: (0, 0)),
            pl.BlockSpec(memory_space=pl.ANY),
        ],
        out_specs=[
            pl.BlockSpec((rows, 2 * c), lambda i: (rb, 0)),
            pl.BlockSpec((n_seq, CONV_HALO, c), full3),
            pl.BlockSpec((n_seq, POOL_HALO, c), full3),
        ],
        out_shape=[
            jax.ShapeDtypeStruct(m.shape, m.dtype),
            jax.ShapeDtypeStruct((n_seq, CONV_HALO, c), F32),
            jax.ShapeDtypeStruct((n_seq, POOL_HALO, c), F32),
        ],
        input_output_aliases={9: 0},
        compiler_params=_params("arbitrary"),
        name="mix_sample",
    )(z, z, z, z, conv_state, pool_state, conv_w, pool_w, pool_scale, m)


N_KEYS = 2 * BLOCK


def _t5_bucket(dist):
    max_exact = N_BUCKETS // 2
    n = jnp.maximum(dist, 0)
    ratio = jnp.log(jnp.maximum(n, 1).astype(F32) / max_exact) / math.log(MAX_DISTANCE / max_exact)
    large = jnp.minimum(max_exact + (ratio * (N_BUCKETS - max_exact)).astype(jnp.int32), N_BUCKETS - 1)
    return jnp.where(n < max_exact, n, large)


def _bias_prompt_kernel(rb_ref, o_ref):
    heads, n_keys, n_q = o_ref.shape
    s = lax.broadcasted_iota(jnp.int32, (n_keys, n_q), 0)
    q = lax.broadcasted_iota(jnp.int32, (n_keys, n_q), 1)
    bucket = _t5_bucket(BLOCK + q - s)
    for hh in range(heads):
        head = pl.program_id(0) * heads + hh
        acc = jnp.zeros((n_keys, n_q), F32)
        for b in range(N_BUCKETS):
            acc = jnp.where(bucket == b, rb_ref[b, head], acc)
        o_ref[hh] = acc


def _bias_sample_kernel(rb_ref, o_ref, *, seq, n_heads):
    half = pl.program_id(0)
    s = lax.broadcasted_iota(jnp.int32, o_ref.shape, 0)
    lane = lax.broadcasted_iota(jnp.int32, o_ref.shape, 1)
    bucket = _t5_bucket(BLOCK + jnp.bitwise_and(lane, seq - 1) - s)
    lane1 = lax.broadcasted_iota(jnp.int32, (1, LANES), 1)
    acc = jnp.zeros(o_ref.shape, F32)
    for b in range(N_BUCKETS):
        vec = jnp.zeros((1, LANES), F32)
        for slab in range(n_heads // 2):
            vec = jnp.where((lane1 >= slab * seq) & (lane1 < (slab + 1) * seq), rb_ref[b, 2 * slab + half], vec)
        acc = jnp.where(bucket == b, vec, acc)
    o_ref[...] = acc


def _bias_tables(rel_bias, seq_s):
    n_heads = rel_bias.shape[1]
    assert seq_s == SUBLANES and (n_heads // 2) * seq_s <= LANES
    heads_per_step = math.gcd(n_heads, BIAS_HEADS_PER_STEP)
    prompt = pl.pallas_call(
        _bias_prompt_kernel,
        grid=(n_heads // heads_per_step,),
        in_specs=[pl.BlockSpec(memory_space=pltpu.SMEM)],
        out_specs=pl.BlockSpec((heads_per_step, N_KEYS, BLOCK), lambda h: (h, 0, 0)),
        out_shape=jax.ShapeDtypeStruct((n_heads, N_KEYS, BLOCK), F32),
        compiler_params=_params("arbitrary"),
        name="rel_bias_prompt",
    )(rel_bias)
    sample = pl.pallas_call(
        functools.partial(_bias_sample_kernel, seq=seq_s, n_heads=n_heads),
        grid=(2,),
        in_specs=[pl.BlockSpec(memory_space=pltpu.SMEM)],
        out_specs=pl.BlockSpec((None, N_KEYS, LANES), lambda h: (h, 0, 0)),
        out_shape=jax.ShapeDtypeStruct((2, N_KEYS, LANES), F32),
        compiler_params=_params("arbitrary"),
        name="rel_bias_sample",
    )(rel_bias)
    return prompt, sample


def _stage_kv(k_all, v_all, kpad_ref, vtpad_ref):
    n_keys, kvw = k_all.shape
    lane = lax.broadcasted_iota(jnp.int32, (n_keys, LANES), 1)
    row = lax.broadcasted_iota(jnp.int32, (LANES, n_keys), 0)
    for slab in range(kvw // LANES):
        ks = k_all[:, slab * LANES:(slab + 1) * LANES]
        vt = v_all[:, slab * LANES:(slab + 1) * LANES].T
        for own in range(2):
            kv = 2 * slab + own
            k_own = jnp.where((lane >= own * HEAD_DIM) & (lane < (own + 1) * HEAD_DIM), ks, 0.0)
            v_own = jnp.where((row >= own * HEAD_DIM) & (row < (own + 1) * HEAD_DIM), vt, 0.0)
            kpad_ref[kv, own] = k_own.astype(BF16)
            kpad_ref[kv, 1 - own] = pltpu.roll(k_own, HEAD_DIM, 1).astype(BF16)
            vtpad_ref[kv, own] = v_own.astype(BF16)
            vtpad_ref[kv, 1 - own] = pltpu.roll(v_own, HEAD_DIM, 0).astype(BF16)


_TRANS_B = (((1,), (1,)), ((), ()))


def _attn_prompt_kernel(*refs, blocks_per_seq, n_blocks):
    step = pl.program_id(0)

    @pl.when(step >= n_blocks)
    def _():
        o_ref = refs[7]
        o_ref[...] = jnp.zeros(o_ref.shape, o_ref.dtype)

    @pl.when(step < n_blocks)
    def _():
        _attn_prompt_block(lax.rem(step, blocks_per_seq) == 0, *refs)


def _attn_prompt_block(first, sinks_ref, q_ref, kc_ref, kp_ref, vc_ref, vp_ref, bias_ref, o_ref,
                       kpad_ref, vtpad_ref, s_ref, m_ref):
    n_heads = bias_ref.shape[0]
    group = n_heads // kpad_ref.shape[0]
    _stage_kv(jnp.concatenate([kp_ref[...], kc_ref[...]], axis=0),
              jnp.concatenate([vp_ref[...], vc_ref[...]], axis=0), kpad_ref, vtpad_ref)
    key = lax.broadcasted_iota(jnp.int32, (N_KEYS, BLOCK), 0)
    qry = lax.broadcasted_iota(jnp.int32, (N_KEYS, BLOCK), 1)
    valid = (key >= qry) & (key <= qry + WINDOW) & ((key >= BLOCK) | jnp.logical_not(first))

    for p in range(n_heads // 2):
        q_slab = (q_ref[:, p * LANES:(p + 1) * LANES] * HEAD_DIM ** -0.5).astype(BF16)
        for half in range(2):
            head = 2 * p + half
            s = lax.dot_general(kpad_ref[head // group, half], q_slab, _TRANS_B, preferred_element_type=F32)
            s = jnp.where(valid, s + bias_ref[head], NEG)
            s_ref[head] = s
            m_ref[head:head + 1, :] = jnp.maximum(jnp.max(s, axis=0, keepdims=True), sinks_ref[head])
    first_head = lax.broadcasted_iota(jnp.int32, (LANES, BLOCK), 0) < HEAD_DIM
    for p in range(n_heads // 2):
        kv = 2 * p // group
        probs, inv = [], []
        for head in (2 * p, 2 * p + 1):
            m = m_ref[head:head + 1, :]
            e = jnp.exp(s_ref[head] - m)
            inv.append(1.0 / (jnp.sum(e, axis=0, keepdims=True) + jnp.exp(sinks_ref[head] - m)))
            probs.append(e.astype(BF16))
        acc = (jnp.dot(vtpad_ref[kv, 0], probs[0], preferred_element_type=F32)
               + jnp.dot(vtpad_ref[kv, 1], probs[1], preferred_element_type=F32))
        o_ref[:, p * LANES:(p + 1) * LANES] = (acc * jnp.where(first_head, inv[0], inv[1])).T.astype(o_ref.dtype)


def _attn_prompt(qkv, bias, sinks, n_seq, seq, n_heads, n_kv, t_total):
    d = n_heads * HEAD_DIM
    kvw = n_kv * HEAD_DIM
    assert kvw % LANES == 0 and d % kvw == 0 and seq % BLOCK == 0
    nb = seq // BLOCK
    kcol, vcol = d // kvw, d // kvw + 1
    n_blocks = n_seq * nb
    n_tail = pl.cdiv(t_total - n_seq * seq, BLOCK)
    cur = lambda col: (lambda s: (s, col))
    prev = lambda col: (lambda s: (jnp.maximum(s - 1, 0), col))
    return pl.pallas_call(
        functools.partial(_attn_prompt_kernel, blocks_per_seq=nb, n_blocks=n_blocks),
        grid=(n_blocks + n_tail,),
        in_specs=[
            pl.BlockSpec(memory_space=pltpu.SMEM),
            pl.BlockSpec((BLOCK, d), cur(0)),
            pl.BlockSpec((BLOCK, kvw), cur(kcol)),
            pl.BlockSpec((BLOCK, kvw), prev(kcol)),
            pl.BlockSpec((BLOCK, kvw), cur(vcol)),
            pl.BlockSpec((BLOCK, kvw), prev(vcol)),
            pl.BlockSpec((n_heads, N_KEYS, BLOCK), lambda s: (0, 0, 0), pipeline_mode=pl.Buffered(1)),
        ],
        out_specs=pl.BlockSpec((BLOCK, d), cur(0)),
        out_shape=jax.ShapeDtypeStruct((t_total, d), BF16),
        scratch_shapes=[
            pltpu.VMEM((n_kv, 2, N_KEYS, LANES), BF16),
            pltpu.VMEM((n_kv, 2, LANES, N_KEYS), BF16),
            pltpu.VMEM((n_heads, N_KEYS, BLOCK), F32),
            pltpu.VMEM((n_heads, BLOCK), F32),
        ],
        compiler_params=_params("arbitrary"),
        name="attn_prompt",
    )(sinks, qkv, qkv, qkv, qkv, qkv, bias)


def _attn_sample_kernel(sinks_ref, q_ref, kn_ref, vn_ref, kc_ref, vc_ref, bias_ref, att_in_ref, o_ref,
                        kpad_ref, vtpad_ref):
    del att_in_ref
    seq, kvw = kn_ref.shape
    n_pairs = q_ref.shape[1] // LANES
    n_kv = kvw // HEAD_DIM
    kv_lanes = n_pairs // n_kv * seq
    pad = jnp.zeros((N_KEYS - kc_ref.shape[0] - seq, kvw), F32)
    _stage_kv(jnp.concatenate([kc_ref[...], kn_ref[...], pad], axis=0),
              jnp.concatenate([vc_ref[...], vn_ref[...], pad], axis=0), kpad_ref, vtpad_ref)

    rows = [q_ref[:, p * LANES:(p + 1) * LANES] for p in range(n_pairs)]
    if n_pairs * seq < LANES:
        rows.append(jnp.zeros((LANES - n_pairs * seq, LANES), F32))
    qs = (jnp.concatenate(rows, axis=0) * HEAD_DIM ** -0.5).astype(BF16)

    key = lax.broadcasted_iota(jnp.int32, (N_KEYS, LANES), 0)
    lane = lax.broadcasted_iota(jnp.int32, (N_KEYS, LANES), 1)
    lane1 = lax.broadcasted_iota(jnp.int32, (1, LANES), 1)
    qry = jnp.bitwise_and(lane, seq - 1)
    valid = (key >= qry) & (key <= qry + WINDOW)

    probs, dens = [], []
    for half in range(2):
        s = jnp.zeros((N_KEYS, LANES), F32)
        for kv in range(n_kv):
            s_kv = lax.dot_general(kpad_ref[kv, half], qs, _TRANS_B, preferred_element_type=F32)
            s = jnp.where((lane >= kv * kv_lanes) & (lane < (kv + 1) * kv_lanes), s_kv, s)
        sink = jnp.zeros((1, LANES), F32)
        for p in range(n_pairs):
            sink = jnp.where((lane1 >= p * seq) & (lane1 < (p + 1) * seq), sinks_ref[2 * p + half], sink)
        s = jnp.where(valid, s + bias_ref[half], NEG)
        m = jnp.maximum(jnp.max(s, axis=0, keepdims=True), sink)
        e = jnp.exp(s - m)
        dens.append(jnp.sum(e, axis=0, keepdims=True) + jnp.exp(sink - m))
        probs.append(e.astype(BF16))

    row_t = lax.broadcasted_iota(jnp.int32, (LANES, LANES), 0)
    lane_t = lax.broadcasted_iota(jnp.int32, (LANES, LANES), 1)
    out_t = jnp.zeros((LANES, LANES), F32)
    for kv in range(n_kv):
        acc = (jnp.dot(vtpad_ref[kv, 0], probs[0], preferred_element_type=F32)
               + jnp.dot(vtpad_ref[kv, 1], probs[1], preferred_element_type=F32))
        out_t = jnp.where((lane_t >= kv * kv_lanes) & (lane_t < (kv + 1) * kv_lanes), acc, out_t)
    inv = jnp.where(row_t < HEAD_DIM, 1.0 / dens[0], 1.0 / dens[1])
    out = (out_t * inv).T
    for p in range(n_pairs):
        o_ref[:, p * LANES:(p + 1) * LANES] = out[p * seq:(p + 1) * seq].astype(o_ref.dtype)


def _attn_sample(qkv, att, cache_k, cache_v, bias, sinks, t_prompt, seq, n_heads):
    n_seq, kv_buf, kvw = cache_k.shape
    d = n_heads * HEAD_DIM
    assert kv_buf == BLOCK and seq == SUBLANES and t_prompt % seq == 0
    rb = t_prompt // seq
    kcol, vcol = d // kvw, d // kvw + 1
    return pl.pallas_call(
        _attn_sample_kernel,
        grid=(n_seq,),
        in_specs=[
            pl.BlockSpec(memory_space=pltpu.SMEM),
            pl.BlockSpec((seq, d), lambda n: (rb + n, 0)),
            pl.BlockSpec((seq, kvw), lambda n: (rb + n, kcol)),
            pl.BlockSpec((seq, kvw), lambda n: (rb + n, vcol)),
            pl.BlockSpec((None, kv_buf, kvw), lambda n: (n, 0, 0)),
            pl.BlockSpec((None, kv_buf, kvw), lambda n: (n, 0, 0)),
            pl.BlockSpec((2, N_KEYS, LANES), lambda n: (0, 0, 0)),
            pl.BlockSpec(memory_space=pl.ANY),
        ],
        out_specs=pl.BlockSpec((seq, d), lambda n: (rb + n, 0)),
        out_shape=jax.ShapeDtypeStruct(att.shape, att.dtype),
        scratch_shapes=[
            pltpu.VMEM((kvw // HEAD_DIM, 2, N_KEYS, LANES), BF16),
            pltpu.VMEM((kvw // HEAD_DIM, 2, LANES, N_KEYS), BF16),
        ],
        input_output_aliases={7: 0},
        compiler_params=_params("arbitrary"),
        name="attn_sample",
    )(sinks, qkv, qkv, qkv, cache_k, cache_v, bias, att)


def kernel(x_prompt, x_sample, state_conv, state_pool, cache_k, cache_v, norm_g, w_ffn_in,
           w_ffn_out, w_mix_in, conv_w, pool_w, pool_scale, w_mix_out, w_qkv, w_o,
           attn_sinks, rel_bias):
    n_p, seq_p, d = x_prompt.shape
    n_s, seq_s, _ = x_sample.shape
    depth = norm_g.shape[0]
    n_heads = d // HEAD_DIM
    n_kv = cache_k.shape[3]
    kv_buf = cache_k.shape[2]
    kvw = n_kv * HEAD_DIM
    conv_ctx = state_conv.shape[2]
    pool_ctx = state_pool.shape[2]
    t_p, t_s = n_p * seq_p, n_s * seq_s
    t = t_p + t_s
    tm = _divisor_tile(t, ROW_TILE, BF16_ROWS)
    tm_proj = _divisor_tile(t, PROJ_ROW_TILE, BF16_ROWS)
    d_ff = w_ffn_out.shape[2]
    ffn_tiles = _ffn_tiling(t, d_ff)
    split_ffn_tiles = _ffn_tiling(t, d_ff, align_rows=(t_p, t_s))
    x = (x_prompt.reshape(t_p, d), x_sample.reshape(t_s, d))
    h = _merge(*x, norm_g, 0, FFN1_PRE)
    bias_p, bias_s = _bias_tables(rel_bias, seq_s)
    out_tile = _divisor_tile(d, COL_TILE, LANES)
    w_mix_out = _column_tiles_bf16(w_mix_out, out_tile)
    w_o = _column_tiles_bf16(w_o, out_tile)

    conv_p, pool_p, k_p, v_p, conv_s, pool_s, k_s, v_s = ([] for _ in range(8))
    for i in range(depth):
        x, h = _ffn(h, x, norm_g, w_ffn_in, w_ffn_out, i, 0, *(split_ffn_tiles if i == 0 else ffn_tiles),
                    nxt=(i, MIX_PRE))
        j = i // 2
        if i % 2 == 0:
            z = _proj(h, w_mix_in, j, tm_proj)
            m, cst, pst = _mix_prompt(z, conv_w[j], pool_w[j], pool_scale[j][None], n_p, seq_p, t)
            conv_p.append(cst[:, CONV_HALO - conv_ctx:])
            pool_p.append(pst[:, POOL_HALO - pool_ctx:])
            cst_in = jnp.pad(state_conv[j], ((0, 0), (CONV_HALO - conv_ctx, 0), (0, 0)))
            pst_in = jnp.pad(state_pool[j], ((0, 0), (POOL_HALO - pool_ctx, 0), (0, 0)))
            m, cst, pst = _mix_sample(z, m, cst_in, pst_in, conv_w[j], pool_w[j], pool_scale[j][None],
                                      t_p, seq_s, PAST_LEN)
            conv_s.append(cst[:, CONV_HALO - conv_ctx:])
            pool_s.append(pst[:, POOL_HALO - pool_ctx:])
            x, h = _out(m, w_mix_out, j, x, norm_g, i, tm)
        else:
            qkv = _proj(h, w_qkv, j, tm_proj)
            att = _attn_prompt(qkv, bias_p, attn_sinks[j], n_p, seq_p, n_heads, n_kv, t)
            ck = cache_k[j].reshape(n_s, kv_buf, kvw)
            cv = cache_v[j].reshape(n_s, kv_buf, kvw)
            att = _attn_sample(qkv, att, ck, cv, bias_s, attn_sinks[j], t_p, seq_s, n_heads)
            k_new, v_new = qkv[:, d:d + kvw], qkv[:, d + kvw:]
            kv_p = lambda a: a[:t_p].reshape(n_p, seq_p, n_kv, HEAD_DIM)[:, seq_p - kv_buf:]
            k_p.append(kv_p(k_new))
            v_p.append(kv_p(v_new))
            kv_s = lambda c, a: jnp.concatenate(
                [c, a[t_p:].reshape(n_s, seq_s, kvw)], axis=1)[:, seq_s:].reshape(n_s, kv_buf, n_kv, HEAD_DIM)
            k_s.append(kv_s(ck, k_new))
            v_s.append(kv_s(cv, v_new))
            x, h = _out(att, w_o, j, x, norm_g, i, tm)
        if i + 1 < depth:
            x, h = _ffn(h, x, norm_g, w_ffn_in, w_ffn_out, i, 1, *ffn_tiles, nxt=(i + 1, FFN1_PRE))
        else:
            y_p, y_s = _ffn(h, x, norm_g, w_ffn_in, w_ffn_out, i, 1, *split_ffn_tiles, head_rows=t_p)

    y_p = y_p.reshape(n_p, seq_p, d)
    y_s = y_s.reshape(n_s, seq_s, d)
    st = jnp.stack
    return (y_p, y_s, st(conv_p), st(pool_p), st(k_p), st(v_p), st(conv_s), st(pool_s), st(k_s), st(v_s))
```
